```python
import math
import jax
import jax.numpy as jnp
from jax import lax
import numpy as np

D_MODEL = 2048
BATCH = 16
SEQ = 2048
DEPTH = 4

DIFF_WIDTH = D_MODEL // 2
DN_WIDTH = D_MODEL - DIFF_WIDTH
DIFF_HEAD_DIM = 64
N_DIFF_HEADS = DIFF_WIDTH // (2 * DIFF_HEAD_DIM)
DIFF_QK = 2 * N_DIFF_HEADS * DIFF_HEAD_DIM
DIFF_V = N_DIFF_HEADS * 2 * DIFF_HEAD_DIM
DN_HEAD_DIM = 128
N_DN_HEADS = DN_WIDTH // DN_HEAD_DIM
MIX_WIDTH = DIFF_V + DN_WIDTH
D_FF = 5632
CONV_K = 4
CHUNK = 64
Q_BLOCK = 128
ROPE_THETA = 500000.0
ROPE_DIM = DIFF_HEAD_DIM // 4
DEEPNORM_ALPHA = (2 * DEPTH) ** 0.25
DEEPNORM_BETA = (8 * DEPTH) ** -0.25
LN_EPS = 1e-5
SUBLN_EPS = 1e-5
GATED_NORM_EPS = 1e-6
L2_EPS = 1e-6
IN_SIZES = (DIFF_QK, DIFF_QK, DIFF_V, DN_WIDTH, DN_WIDTH, DN_WIDTH, DN_WIDTH, N_DN_HEADS, N_DN_HEADS)
IN_WIDTH = sum(IN_SIZES)

kernel_name = 'hymba_diffattn_gdn_macaron_deepnorm'


def layer_norm(x, g, b):
    xf = x.astype(jnp.float32)
    mu = jnp.mean(xf, axis=-1, keepdims=True)
    var = jnp.mean(jnp.square(xf - mu), axis=-1, keepdims=True)
    return ((xf - mu) * lax.rsqrt(var + LN_EPS) * g + b).astype(x.dtype)


def rms_norm(x, g, eps):
    xf = x.astype(jnp.float32)
    return (xf * lax.rsqrt(jnp.mean(jnp.square(xf), axis=-1, keepdims=True) + eps) * g).astype(x.dtype)


def l2_norm(x):
    return x * lax.rsqrt(jnp.sum(jnp.square(x), axis=-1, keepdims=True) + L2_EPS)


def swiglu(x, w_in, w_out):
    gate, up = jnp.split(x @ w_in, 2, axis=-1)
    return (jax.nn.silu(gate) * up) @ w_out


def rope_tables(positions):
    inv_freq = ROPE_THETA ** (-jnp.arange(0, ROPE_DIM, 2, dtype=jnp.float32) / ROPE_DIM)
    ang = positions.astype(jnp.float32)[..., None] * inv_freq
    return jnp.cos(ang)[:, :, None, :], jnp.sin(ang)[:, :, None, :]


def partial_rope(x, cos, sin):
    half = ROPE_DIM // 2
    x1 = x[..., :half].astype(jnp.float32)
    x2 = x[..., half:ROPE_DIM].astype(jnp.float32)
    rot = jnp.concatenate([x1 * cos - x2 * sin, x2 * cos + x1 * sin], axis=-1).astype(x.dtype)
    return jnp.concatenate([rot, x[..., ROPE_DIM:]], axis=-1)


def causal_depthwise_conv(u, w):
    S = u.shape[1]
    u_pad = jnp.pad(u, ((0, 0), (CONV_K - 1, 0), (0, 0)))
    return sum(w[j] * u_pad[:, j:j + S] for j in range(CONV_K))


def diff_attention(q, k, v, lam):
    S = q.shape[3]
    n_blk = S // Q_BLOCK
    scale = DIFF_HEAD_DIM ** -0.5
    k_pos = jnp.arange(S)

    def one_block(i):
        q_blk = lax.dynamic_slice_in_dim(q, i * Q_BLOCK, Q_BLOCK, axis=3)
        s = jnp.einsum('bhcqd,bhckd->bhcqk', q_blk, k).astype(jnp.float32) * scale
        q_pos = i * Q_BLOCK + jnp.arange(Q_BLOCK)
        s = jnp.where(k_pos[None, :] <= q_pos[:, None], s, -jnp.inf)
        p = jax.nn.softmax(s, axis=-1)
        p = p[:, :, 0] - lam * p[:, :, 1]
        return jnp.einsum('bhqk,bhkd->bhqd', p.astype(v.dtype), v)

    o = lax.map(one_block, jnp.arange(n_blk))
    B, H = q.shape[0], q.shape[1]
    return jnp.transpose(o, (1, 0, 3, 2, 4)).reshape(B, S, H, v.shape[-1])


def chunk_gated_delta_rule(q, k, v, beta, g):
    B, H, S, dk = q.shape
    dv = v.shape[-1]
    n = S // CHUNK
    q = q * dk ** -0.5
    q, k, v = (t.reshape(B, H, n, CHUNK, t.shape[-1]) for t in (q, k, v))
    beta = beta.reshape(B, H, n, CHUNK)
    g = jnp.cumsum(g.reshape(B, H, n, CHUNK), axis=-1)
    idx = jnp.arange(CHUNK)
    incl = idx[:, None] >= idx[None, :]
    strict = idx[:, None] > idx[None, :]
    decay = jnp.exp(jnp.where(incl, g[..., :, None] - g[..., None, :], -jnp.inf))
    k_beta = k * beta[..., None]
    L = jnp.where(strict, jnp.einsum('bhncd,bhnkd->bhnck', k_beta, k) * decay, 0.0)
    a = L + jnp.eye(CHUNK, dtype=jnp.float32)
    rhs = jnp.concatenate([v * beta[..., None], k_beta * jnp.exp(g)[..., None]], axis=-1)
    sol = lax.linalg.triangular_solve(a, rhs, left_side=True, lower=True, unit_diagonal=True)
    u, w = sol[..., :dv], sol[..., dv:]
    attn_intra = jnp.einsum('bhncd,bhnkd->bhnck', q, k) * decay
    g_last = g[..., -1:]
    q_dec = q * jnp.exp(g)[..., None]
    k_dec = k * jnp.exp(g_last - g)[..., None]
    chunk_decay = jnp.exp(g_last[..., 0])
    xs = tuple(jnp.moveaxis(t, 2, 0) for t in (q_dec, k_dec, u, w, attn_intra, chunk_decay))

    def step(state, inp):
        qd, kd, u_c, w_c, A, cd = inp
        v_new = u_c - jnp.einsum('bhck,bhkv->bhcv', w_c, state)
        o = jnp.einsum('bhck,bhkv->bhcv', qd, state) + jnp.einsum('bhcj,bhjv->bhcv', A, v_new)
        state = state * cd[..., None, None] + jnp.einsum('bhck,bhcv->bhkv', kd, v_new)
        return state, o

    s0 = jnp.zeros((B, H, dk, dv), jnp.float32)
    _, o = lax.scan(step, s0, xs)
    return jnp.moveaxis(o, 0, 2).reshape(B, H, S, dv)


def hybrid_mixer(x, cos, sin, w_in, conv_w, a_log, dt_bias, lam_q1, lam_k1, lam_q2, lam_k2,
                 diff_norm_g, delta_norm_g, w_out, lam_init):
    B, S, _ = x.shape
    offsets = tuple(int(o) for o in np.cumsum(IN_SIZES)[:-1])
    aq, ak, av, dq, dk, dv, dz, db, da = jnp.split(x @ w_in, offsets, axis=-1)

    aq = partial_rope(aq.reshape(B, S, 2 * N_DIFF_HEADS, DIFF_HEAD_DIM), cos, sin)
    ak = partial_rope(ak.reshape(B, S, 2 * N_DIFF_HEADS, DIFF_HEAD_DIM), cos, sin)
    aq = jnp.transpose(aq.reshape(B, S, N_DIFF_HEADS, 2, DIFF_HEAD_DIM), (0, 2, 3, 1, 4))
    ak = jnp.transpose(ak.reshape(B, S, N_DIFF_HEADS, 2, DIFF_HEAD_DIM), (0, 2, 3, 1, 4))
    av = jnp.transpose(av.reshape(B, S, N_DIFF_HEADS, 2 * DIFF_HEAD_DIM), (0, 2, 1, 3))
    lam = (jnp.exp(jnp.sum(lam_q1.astype(jnp.float32) * lam_k1.astype(jnp.float32)))
           - jnp.exp(jnp.sum(lam_q2.astype(jnp.float32) * lam_k2.astype(jnp.float32))) + lam_init)
    o_diff = diff_attention(aq, ak, av, lam)
    o_diff = (rms_norm(o_diff, diff_norm_g, SUBLN_EPS) * (1.0 - lam_init)).reshape(B, S, DIFF_V)

    qkv = jax.nn.silu(causal_depthwise_conv(jnp.concatenate([dq, dk, dv], axis=-1), conv_w))
    dq, dk, dv = jnp.split(qkv, 3, axis=-1)
    heads = lambda t: jnp.transpose(t.reshape(B, S, N_DN_HEADS, DN_HEAD_DIM), (0, 2, 1, 3)).astype(jnp.float32)
    dq, dk, dv = l2_norm(heads(dq)), l2_norm(heads(dk)), heads(dv)
    beta = jnp.transpose(jax.nn.sigmoid(db.astype(jnp.float32)), (0, 2, 1))
    g = -jnp.exp(a_log.astype(jnp.float32)) * jax.nn.softplus(da.astype(jnp.float32) + dt_bias.astype(jnp.float32))
    g = jnp.transpose(g, (0, 2, 1))
    o_dn = chunk_gated_delta_rule(dq, dk, dv, beta, g)
    o_dn = jnp.transpose(o_dn, (0, 2, 1, 3)).astype(x.dtype)
    o_dn = rms_norm(o_dn, delta_norm_g, GATED_NORM_EPS) * jax.nn.silu(dz.reshape(B, S, N_DN_HEADS, DN_HEAD_DIM))
    o_dn = o_dn.reshape(B, S, DN_WIDTH)

    return jnp.concatenate([o_diff, o_dn], axis=-1) @ w_out


def setup_inputs(seed: int = 0) -> dict:
    key = jax.random.key(seed)
    ks = jax.random.split(key, 24)
    f32 = jnp.float32

    def nrm(k, shape, scale):
        return jax.random.normal(k, shape, f32) * scale

    x = nrm(ks[0], (BATCH, SEQ, D_MODEL), 1.0)
    offset = jax.random.randint(ks[1], (BATCH, 1), 0, 4096, dtype=jnp.int32)
    positions = offset + jnp.arange(SEQ, dtype=jnp.int32)[None, :]
    ffn1_w_in = nrm(ks[2], (DEPTH, D_MODEL, 2 * D_FF), D_MODEL ** -0.5)
    ffn1_w_out = nrm(ks[3], (DEPTH, D_FF, D_MODEL), D_FF ** -0.5 * DEEPNORM_BETA)
    ln1_g = 1.0 + nrm(ks[4], (DEPTH, D_MODEL), 0.02)
    ln1_b = nrm(ks[5], (DEPTH, D_MODEL), 0.02)
    w_in = nrm(ks[6], (DEPTH, D_MODEL, IN_WIDTH), D_MODEL ** -0.5)
    conv_w = nrm(ks[7], (DEPTH, CONV_K, 3 * DN_WIDTH), CONV_K ** -0.5)
    a_log = jnp.log(jax.random.uniform(ks[8], (DEPTH, N_DN_HEADS), f32, 1.0, 16.0))
    dt = jnp.exp(jax.random.uniform(ks[9], (DEPTH, N_DN_HEADS), f32, math.log(1e-3), math.log(1e-1)))
    dt_bias = dt + jnp.log(-jnp.expm1(-dt))
    lam_q1 = nrm(ks[10], (DEPTH, DIFF_HEAD_DIM), 0.1)
    lam_k1 = nrm(ks[11], (DEPTH, DIFF_HEAD_DIM), 0.1)
    lam_q2 = nrm(ks[12], (DEPTH, DIFF_HEAD_DIM), 0.1)
    lam_k2 = nrm(ks[13], (DEPTH, DIFF_HEAD_DIM), 0.1)
    diff_norm_g = 1.0 + nrm(ks[14], (DEPTH, 2 * DIFF_HEAD_DIM), 0.02)
    delta_norm_g = 1.0 + nrm(ks[15], (DEPTH, DN_HEAD_DIM), 0.02)
    w_out = nrm(ks[16], (DEPTH, MIX_WIDTH, D_MODEL), MIX_WIDTH ** -0.5 * DEEPNORM_BETA)
    ln2_g = 1.0 + nrm(ks[17], (DEPTH, D_MODEL), 0.02)
    ln2_b = nrm(ks[18], (DEPTH, D_MODEL), 0.02)
    ffn2_w_in = nrm(ks[19], (DEPTH, D_MODEL, 2 * D_FF), D_MODEL ** -0.5)
    ffn2_w_out = nrm(ks[20], (DEPTH, D_FF, D_MODEL), D_FF ** -0.5 * DEEPNORM_BETA)
    ln3_g = 1.0 + nrm(ks[21], (DEPTH, D_MODEL), 0.02)
    ln3_b = nrm(ks[22], (DEPTH, D_MODEL), 0.02)
    return {'x': x, 'positions': positions, 'ffn1_w_in': ffn1_w_in, 'ffn1_w_out': ffn1_w_out,
            'ln1_g': ln1_g, 'ln1_b': ln1_b, 'w_in': w_in, 'conv_w': conv_w, 'a_log': a_log,
            'dt_bias': dt_bias, 'lam_q1': lam_q1, 'lam_k1': lam_k1, 'lam_q2': lam_q2, 'lam_k2': lam_k2,
            'diff_norm_g': diff_norm_g, 'delta_norm_g': delta_norm_g, 'w_out': w_out,
            'ln2_g': ln2_g, 'ln2_b': ln2_b, 'ffn2_w_in': ffn2_w_in, 'ffn2_w_out': ffn2_w_out,
            'ln3_g': ln3_g, 'ln3_b': ln3_b}


def reference(x, positions, ffn1_w_in, ffn1_w_out, ln1_g, ln1_b, w_in, conv_w, a_log, dt_bias,
              lam_q1, lam_k1, lam_q2, lam_k2, diff_norm_g, delta_norm_g, w_out, ln2_g, ln2_b,
              ffn2_w_in, ffn2_w_out, ln3_g, ln3_b):
    cos, sin = rope_tables(positions)
    for l in range(DEPTH):
        lam_init = 0.8 - 0.6 * math.exp(-0.3 * l)
        x = layer_norm(DEEPNORM_ALPHA * x + 0.5 * swiglu(x, ffn1_w_in[l], ffn1_w_out[l]), ln1_g[l], ln1_b[l])
        mix = hybrid_mixer(x, cos, sin, w_in[l], conv_w[l], a_log[l], dt_bias[l], lam_q1[l], lam_k1[l],
                           lam_q2[l], lam_k2[l], diff_norm_g[l], delta_norm_g[l], w_out[l], lam_init)
        x = layer_norm(DEEPNORM_ALPHA * x + mix, ln2_g[l], ln2_b[l])
        x = layer_norm(DEEPNORM_ALPHA * x + 0.5 * swiglu(x, ffn2_w_in[l], ffn2_w_out[l]), ln3_g[l], ln3_b[l])
    return x
```

```python
import functools
import math

import jax
import jax.numpy as jnp
from jax import lax
from jax.experimental import pallas as pl
from jax.experimental.pallas import tpu as pltpu

F32 = jnp.float32
BF16 = jnp.bfloat16

LANES = 128
DIFF_HEAD_DIM = 64
DN_HEAD_DIM = 128
CONV_K = 4
CHUNK = 64
SUPER = 2 * CHUNK
ROPE_THETA = 500000.0
ROPE_DIM = DIFF_HEAD_DIM // 4
ROPE_HALF = ROPE_DIM // 2
LN_EPS = 1e-5
SUBLN_EPS = 1e-5
GATED_NORM_EPS = 1e-6
L2_EPS = 1e-6
VMEM_LIMIT_BYTES = 56 * 1024 * 1024

_NT = (((1,), (1,)), ((), ()))


def _params(sem):
    return pltpu.CompilerParams(dimension_semantics=sem, vmem_limit_bytes=VMEM_LIMIT_BYTES)


def _layer_norm(y, g, b):
    mu = jnp.mean(y, axis=-1, keepdims=True)
    yc = y - mu
    var = jnp.mean(yc * yc, axis=-1, keepdims=True)
    return yc * lax.rsqrt(var + LN_EPS) * g + b


def _silu(x):
    return x * jax.nn.sigmoid(x)


def _rope_table_kernel(pos_ref, inv_ref, c_ref, sa_ref, sb_ref):
    ang = pos_ref[...].astype(F32) * inv_ref[...]
    lane = lax.broadcasted_iota(jnp.int32, ang.shape, 1) % DIFF_HEAD_DIM
    cos, sin = jnp.cos(ang), jnp.sin(ang)
    c_ref[...] = jnp.where(lane < ROPE_DIM, cos, 1.0)
    sa_ref[...] = jnp.where(lane < ROPE_HALF, -sin, 0.0)
    sb_ref[...] = jnp.where((lane >= ROPE_HALF) & (lane < ROPE_DIM), sin, 0.0)


def _rope_tables(positions, tm):
    n = positions.size
    pos = positions.reshape(n, 1)
    inv_freq = ROPE_THETA ** (-jnp.arange(0, ROPE_DIM, 2, dtype=F32) / ROPE_DIM)
    lane = jnp.arange(LANES) % DIFF_HEAD_DIM
    inv = jnp.where(lane < ROPE_DIM, inv_freq[lane % ROPE_HALF], 0.0).reshape(1, LANES).astype(F32)
    out = jax.ShapeDtypeStruct((n, LANES), F32)
    spec = pl.BlockSpec((tm, LANES), lambda i: (i, 0))
    return pl.pallas_call(
        _rope_table_kernel,
        grid=(n // tm,),
        in_specs=[pl.BlockSpec((tm, 1), lambda i: (i, 0)), pl.BlockSpec((1, LANES), lambda i: (0, 0))],
        out_specs=[spec, spec, spec],
        out_shape=[out, out, out],
        compiler_params=_params(("parallel",)),
        name="rope_tables",
    )(pos, inv)


def _ffn_ln_kernel(x_ref, wg_ref, wu_ref, wo_ref, g_ref, b_ref, *rest, alpha, emit_bf16):
    if emit_bf16:
        y_ref, yb_ref, xb_ref, acc_ref = rest
    else:
        y_ref, xb_ref, acc_ref = rest
    j = pl.program_id(1)

    @pl.when(j == 0)
    def _init():
        xb_ref[...] = x_ref[...].astype(BF16)
        acc_ref[...] = jnp.zeros_like(acc_ref)

    xb = xb_ref[...]
    hg = jnp.dot(xb, wg_ref[...], preferred_element_type=F32)
    hu = jnp.dot(xb, wu_ref[...], preferred_element_type=F32)
    a = (_silu(hg) * hu).astype(BF16)
    acc_ref[...] += jnp.dot(a, wo_ref[...], preferred_element_type=F32)

    @pl.when(j == pl.num_programs(1) - 1)
    def _fin():
        y = _layer_norm(alpha * x_ref[...] + 0.5 * acc_ref[...], g_ref[...], b_ref[...])
        y_ref[...] = y
        if emit_bf16:
            yb_ref[...] = y.astype(BF16)


def _ffn_ln(x, w_in, w_out, g, b, layer, *, alpha, tm, tf, emit_bf16):
    n, d = x.shape
    f = w_out.shape[1]
    nf = f // tf
    row = pl.BlockSpec((tm, d), lambda i, j: (i, 0))
    vec = pl.BlockSpec((None, 1, d), lambda i, j: (layer, 0, 0))
    out_shape = [jax.ShapeDtypeStruct((n, d), F32)]
    out_specs = [row]
    if emit_bf16:
        out_shape.append(jax.ShapeDtypeStruct((n, d), BF16))
        out_specs.append(row)
    return pl.pallas_call(
        functools.partial(_ffn_ln_kernel, alpha=alpha, emit_bf16=emit_bf16),
        grid=(n // tm, nf),
        in_specs=[
            row,
            pl.BlockSpec((None, d, tf), lambda i, j: (layer, 0, j)),
            pl.BlockSpec((None, d, tf), lambda i, j: (layer, 0, j + nf)),
            pl.BlockSpec((None, tf, d), lambda i, j: (layer, j, 0)),
            vec, vec,
        ],
        out_specs=out_specs,
        out_shape=out_shape,
        scratch_shapes=[pltpu.VMEM((tm, d), BF16), pltpu.VMEM((tm, d), F32)],
        compiler_params=_params(("parallel", "arbitrary")),
        name="ffn_ln",
    )(x, w_in, w_in, w_out, g, b)


def _proj_qk_kernel(xb_ref, w_ref, c_ref, sa_ref, sb_ref, o_ref, *, n_q_tiles):
    acc = jnp.dot(xb_ref[...], w_ref[...], preferred_element_type=F32)
    scale = jnp.where(pl.program_id(1) < n_q_tiles, DIFF_HEAD_DIM ** -0.5, 1.0)
    c, sa, sb = c_ref[...] * scale, sa_ref[...] * scale, sb_ref[...] * scale
    for t in range(acc.shape[1] // LANES):
        seg = acc[:, t * LANES:(t + 1) * LANES]
        rot = seg * c + pltpu.roll(seg, LANES - ROPE_HALF, 1) * sa + pltpu.roll(seg, ROPE_HALF, 1) * sb
        o_ref[:, t * LANES:(t + 1) * LANES] = rot.astype(o_ref.dtype)


def _proj_plain_kernel(xb_ref, w_ref, o_ref):
    o_ref[...] = jnp.dot(xb_ref[...], w_ref[...], preferred_element_type=F32).astype(o_ref.dtype)


def _proj_gate_kernel(xb_ref, w_ref, wt_ref, oc_ref, or_ref):
    xb = xb_ref[...]
    oc_ref[...] = jnp.dot(xb, w_ref[...], preferred_element_type=F32)
    or_ref[...] = lax.dot_general(wt_ref[...], xb, _NT, preferred_element_type=F32)


def _proj(xb, w, layer, col0, ncols, out_dtype, *, tm, tn, rope=None):
    n, d = xb.shape
    xspec = pl.BlockSpec((tm, d), lambda i, j: (i, 0))
    wspec = pl.BlockSpec((None, d, tn), lambda i, j: (layer, 0, col0 // tn + j))
    ospec = pl.BlockSpec((tm, tn), lambda i, j: (i, j))
    if rope is None:
        body, in_specs, args = _proj_plain_kernel, [xspec, wspec], (xb, w)
    else:
        c, sa, sb, q_cols = rope
        tspec = pl.BlockSpec((tm, LANES), lambda i, j: (i, 0))
        body = functools.partial(_proj_qk_kernel, n_q_tiles=q_cols // tn)
        in_specs, args = [xspec, wspec, tspec, tspec, tspec], (xb, w, c, sa, sb)
    return pl.pallas_call(
        body,
        grid=(n // tm, ncols // tn),
        in_specs=in_specs,
        out_specs=ospec,
        out_shape=jax.ShapeDtypeStruct((n, ncols), out_dtype),
        compiler_params=_params(("parallel", "arbitrary")),
        name="proj",
    )(*args)


def _proj_gates(xb, w_bd, w_bd_t, layer, *, tm):
    n, d = xb.shape
    rows = w_bd_t.shape[1]
    return pl.pallas_call(
        _proj_gate_kernel,
        grid=(n // tm,),
        in_specs=[
            pl.BlockSpec((tm, d), lambda i: (i, 0)),
            pl.BlockSpec((None, d, LANES), lambda i: (layer, 0, 0)),
            pl.BlockSpec((None, rows, d), lambda i: (layer, 0, 0)),
        ],
        out_specs=[pl.BlockSpec((tm, LANES), lambda i: (i, 0)), pl.BlockSpec((rows, tm), lambda i: (0, i))],
        out_shape=[jax.ShapeDtypeStruct((n, LANES), F32), jax.ShapeDtypeStruct((rows, n), F32)],
        compiler_params=_params(("parallel",)),
        name="proj_gates",
    )(xb, w_bd, w_bd_t)


def _attn_kernel(lam_ref, q_ref, k_ref, v_ref, g_ref, o_ref, *, tq, lam_init):
    s_len = q_ref.shape[0]
    lq = lam_ref[...]
    lam = (jnp.exp(jnp.sum(lq[0:1] * lq[1:2], axis=-1, keepdims=True))
           - jnp.exp(jnp.sum(lq[2:3] * lq[3:4], axis=-1, keepdims=True)) + lam_init)
    lane = lax.broadcasted_iota(jnp.int32, (1, LANES), 1)
    map0 = lane < DIFF_HEAD_DIM
    gain = g_ref[...] * (1.0 - lam_init)
    for i in range(s_len // tq):
        skv = (i + 1) * tq
        q = q_ref[i * tq:(i + 1) * tq, :]
        zero = jnp.zeros_like(q)
        k = k_ref[0:skv, :]
        s0 = lax.dot_general(jnp.where(map0, q, zero), k, _NT, preferred_element_type=F32)
        s1 = lax.dot_general(jnp.where(map0, zero, q), k, _NT, preferred_element_type=F32)
        row = lax.broadcasted_iota(jnp.int32, (tq, skv), 0) + i * tq
        col = lax.broadcasted_iota(jnp.int32, (tq, skv), 1)
        keep = col <= row
        s0 = jnp.where(keep, s0, -jnp.inf)
        s1 = jnp.where(keep, s1, -jnp.inf)
        e0 = jnp.exp(s0 - jnp.max(s0, axis=-1, keepdims=True))
        e1 = jnp.exp(s1 - jnp.max(s1, axis=-1, keepdims=True))
        r0 = 1.0 / jnp.sum(e0, axis=-1, keepdims=True)
        r1 = lam / jnp.sum(e1, axis=-1, keepdims=True)
        p = (e0 * r0 - e1 * r1).astype(BF16)
        o = jnp.dot(p, v_ref[0:skv, :], preferred_element_type=F32)
        ms = jnp.mean(o * o, axis=-1, keepdims=True)
        o_ref[i * tq:(i + 1) * tq, :] = (o * lax.rsqrt(ms + SUBLN_EPS) * gain).astype(o_ref.dtype)


def _attn(qk, v, lam_vecs, norm_g, layer, *, batch, seq, n_heads, tq, lam_init):
    n = qk.shape[0]
    hd = 2 * DIFF_HEAD_DIM
    return pl.pallas_call(
        functools.partial(_attn_kernel, tq=tq, lam_init=lam_init),
        grid=(batch, n_heads),
        in_specs=[
            pl.BlockSpec((None, 4, DIFF_HEAD_DIM), lambda b, h: (layer, 0, 0)),
            pl.BlockSpec((seq, hd), lambda b, h: (b, h)),
            pl.BlockSpec((seq, hd), lambda b, h: (b, n_heads + h)),
            pl.BlockSpec((seq, hd), lambda b, h: (b, h)),
            pl.BlockSpec((None, 1, hd), lambda b, h: (layer, 0, 0)),
        ],
        out_specs=pl.BlockSpec((seq, hd), lambda b, h: (b, h)),
        out_shape=jax.ShapeDtypeStruct((n, n_heads * hd), BF16),
        compiler_params=_params(("parallel", "parallel")),
        name="diff_attn",
    )(lam_vecs, qk, qk, v, norm_g)


def _pick_lane(x, idx):
    lane = lax.broadcasted_iota(jnp.int32, x.shape, 1)
    return jnp.sum(jnp.where(lane == idx, x, 0.0), axis=1, keepdims=True)


def _pick_row(x, idx):
    row = lax.broadcasted_iota(jnp.int32, x.shape, 0)
    return jnp.sum(jnp.where(row == idx, x, 0.0), axis=0, keepdims=True)


def _chunk_scan(x, pos, axis, reverse=False):
    size = x.shape[axis]
    s = 1
    while s < CHUNK:
        if reverse:
            x = x + jnp.where(pos < CHUNK - s, pltpu.roll(x, size - s, axis), 0.0)
        else:
            x = x + jnp.where(pos >= s, pltpu.roll(x, s, axis), 0.0)
        s *= 2
    return x


def _gdn_kernel(gq_ref, gk_ref, gv_ref, gz_ref, cwq_ref, cwk_ref, cwv_ref, bdc_ref, bdr_ref,
                alc_ref, dtc_ref, alr_ref, dtr_ref, ng_ref, o_ref,
                qn_ref, kn_ref, vc_ref, beta_ref, eg_ref, ekd_ref, gcum_ref, grow_ref,
                u_ref, wq_ref, akt_ref, cd_ref, oacc_ref, *, hb, nh):
    s_len = gq_ref.shape[0]
    n_super = s_len // SUPER
    head0 = pl.program_id(1) * hb
    rows = lax.broadcasted_iota(jnp.int32, (s_len, 1), 0)

    def conv_silu(u_ref_, w_ref_):
        u = u_ref_[...]
        w = w_ref_[...]
        acc = u * w[CONV_K - 1:CONV_K]
        for j in range(CONV_K - 1):
            sh = CONV_K - 1 - j
            acc = acc + jnp.where(rows >= sh, pltpu.roll(u, sh, 0), 0.0) * w[j:j + 1]
        return _silu(acc)

    q = conv_silu(gq_ref, cwq_ref)
    k = conv_silu(gk_ref, cwk_ref)
    vc_ref[...] = conv_silu(gv_ref, cwv_ref)
    for hh in range(hb):
        sl = slice(hh * DN_HEAD_DIM, (hh + 1) * DN_HEAD_DIM)
        qh, kh = q[:, sl], k[:, sl]
        qn_ref[:, sl] = qh * (lax.rsqrt(jnp.sum(qh * qh, axis=-1, keepdims=True) + L2_EPS) * DN_HEAD_DIM ** -0.5)
        kn_ref[:, sl] = kh * lax.rsqrt(jnp.sum(kh * kh, axis=-1, keepdims=True) + L2_EPS)

    xg = bdc_ref[...]
    beta_ref[...] = jax.nn.sigmoid(xg)
    g_raw = -jnp.exp(alc_ref[...]) * jax.nn.softplus(xg + dtc_ref[...])
    g_cum = _chunk_scan(g_raw, rows % CHUNK, 0)
    g_rest = _chunk_scan(g_raw, rows % CHUNK, 0, reverse=True) - g_raw
    gcum_ref[...] = g_cum
    eg_ref[...] = jnp.exp(g_cum)
    ekd_ref[...] = jnp.exp(g_rest)
    yg = bdr_ref[...]
    cols = lax.broadcasted_iota(jnp.int32, (1, s_len), 1)
    gr_raw = -jnp.exp(alr_ref[...]) * jax.nn.softplus(yg + dtr_ref[...])
    grow_ref[...] = _chunk_scan(gr_raw, cols % CHUNK, 1)

    ii = lax.broadcasted_iota(jnp.int32, (SUPER, SUPER), 0)
    jj = lax.broadcasted_iota(jnp.int32, (SUPER, SUPER), 1)
    same = (ii // CHUNK) == (jj // CHUNK)
    incl = same & (ii >= jj)
    strict = same & (ii > jj)
    eye = (ii == jj).astype(F32)

    def lower_left_mask(s):
        return ((ii // (2 * s)) == (jj // (2 * s))) & ((ii // s) % 2 == 1) & ((jj // s) % 2 == 0)

    lower_left = {}
    s = 1
    while s < CHUNK:
        lower_left[s] = lower_left_mask(s)
        s *= 2

    def prep(m, carry):
        r0 = pl.multiple_of(m * SUPER, SUPER)
        rs = pl.ds(r0, SUPER)
        for hh in range(hb):
            sl = slice(hh * DN_HEAD_DIM, (hh + 1) * DN_HEAD_DIM)
            head = head0 + hh
            qn, kn, v = qn_ref[rs, sl], kn_ref[rs, sl], vc_ref[rs, sl]
            beta = _pick_lane(beta_ref[rs, :], head)
            eg = _pick_lane(eg_ref[rs, :], nh + head)
            ekd = _pick_lane(ekd_ref[rs, :], nh + head)
            gcol = _pick_lane(gcum_ref[rs, :], nh + head)
            grow = _pick_row(grow_ref[:, rs], nh + head)
            decay = jnp.where(incl, jnp.exp(jnp.where(incl, gcol - grow, 0.0)), 0.0)
            kb = kn * beta
            knb = kn.astype(BF16)
            kk = lax.dot_general(kb.astype(BF16), knb, _NT, preferred_element_type=F32)
            qk = lax.dot_general(qn.astype(BF16), knb, _NT, preferred_element_type=F32)
            lmat = jnp.where(strict, kk * decay, 0.0)
            t = eye - jnp.where(lower_left[1], lmat, 0.0)
            s = 2
            while s < CHUNK:
                c = jnp.where(lower_left[s], lmat, 0.0).astype(BF16)
                y = jnp.dot(c, t.astype(BF16), preferred_element_type=F32)
                t = t - jnp.dot(t.astype(BF16), y.astype(BF16), preferred_element_type=F32)
                s *= 2
            rhs = jnp.concatenate([v * beta, kb * eg], axis=1).astype(BF16)
            sol = jnp.dot(t.astype(BF16), rhs, preferred_element_type=F32)
            u_ref[hh, rs, :] = sol[:, :DN_HEAD_DIM]
            w = sol[:, DN_HEAD_DIM:].astype(BF16)
            qd = (qn * eg).astype(BF16)
            base = pl.multiple_of(m * (2 * SUPER), 2 * SUPER)
            for c2 in range(2):
                cs = slice(c2 * CHUNK, (c2 + 1) * CHUNK)
                wq_ref[hh, pl.ds(base + c2 * SUPER, CHUNK), :] = w[cs]
                wq_ref[hh, pl.ds(base + c2 * SUPER + CHUNK, CHUNK), :] = qd[cs]
            akt_ref[hh, pl.ds(base, SUPER), :] = (qk * decay).astype(BF16)
            akt_ref[hh, pl.ds(base + SUPER, SUPER), :] = (kn * ekd).T.astype(BF16)
            for c2 in range(2):
                last8 = eg[(c2 + 1) * CHUNK - 8:(c2 + 1) * CHUNK]
                cd_ref[hh, pl.ds(pl.multiple_of(m * 16 + c2 * 8, 8), 8), :] = jnp.broadcast_to(
                    last8, (8, DN_HEAD_DIM))
        return carry

    lax.fori_loop(0, n_super, prep, 0)

    zeros_half = jnp.zeros((CHUNK, DN_HEAD_DIM), BF16)

    def step(m, states):
        base = pl.multiple_of(m * (2 * SUPER), 2 * SUPER)
        r0 = pl.multiple_of(m * SUPER, SUPER)
        new_states = []
        for hh in range(hb):
            state = states[hh]
            akt = akt_ref[hh, pl.ds(base, 2 * SUPER), :]
            cd_rows = cd_ref[hh, pl.ds(pl.multiple_of(m * 16, 16), 16), :]
            for c2 in range(2):
                wq = wq_ref[hh, pl.ds(base + c2 * SUPER, SUPER), :]
                res1 = jnp.dot(wq, state.astype(BF16), preferred_element_type=F32)
                v_new = u_ref[hh, pl.ds(r0 + c2 * CHUNK, CHUNK), :] - res1[:CHUNK]
                vb = v_new.astype(BF16)
                rhs = jnp.concatenate([vb, zeros_half] if c2 == 0 else [zeros_half, vb], axis=0)
                res2 = jnp.dot(akt, rhs, preferred_element_type=F32)
                oacc_ref[hh, pl.ds(r0 + c2 * CHUNK, CHUNK), :] = (
                    res1[CHUNK:] + res2[c2 * CHUNK:(c2 + 1) * CHUNK])
                state = state * cd_rows[8 * c2 + 7:8 * c2 + 8] + res2[SUPER:]
            new_states.append(state)
        return tuple(new_states)

    init = tuple(jnp.zeros((DN_HEAD_DIM, DN_HEAD_DIM), F32) for _ in range(hb))
    lax.fori_loop(0, n_super, step, init)

    for hh in range(hb):
        sl = slice(hh * DN_HEAD_DIM, (hh + 1) * DN_HEAD_DIM)
        o = oacc_ref[hh]
        ms = jnp.mean(o * o, axis=-1, keepdims=True)
        o_ref[:, sl] = (o * lax.rsqrt(ms + GATED_NORM_EPS) * ng_ref[...] * _silu(gz_ref[:, sl])).astype(o_ref.dtype)


def _gdn(proj, conv_w, bdc, bdr, alc, dtc, alr, dtr, norm_g, layer, *, batch, seq, nh, hb):
    n = proj.shape[0]
    width = nh * DN_HEAD_DIM
    bw = hb * DN_HEAD_DIM
    groups = nh // hb
    rows_r = bdr.shape[0]

    def col(off):
        return pl.BlockSpec((seq, bw), lambda b, g: (b, off * groups + g))

    def cw(off):
        return pl.BlockSpec((None, CONV_K, bw), lambda b, g: (layer, 0, off * groups + g))

    lane_vec = pl.BlockSpec((None, 1, LANES), lambda b, g: (layer, 0, 0))
    row_vec = pl.BlockSpec((None, rows_r, 1), lambda b, g: (layer, 0, 0))
    return pl.pallas_call(
        functools.partial(_gdn_kernel, hb=hb, nh=nh),
        grid=(batch, groups),
        in_specs=[
            col(0), col(1), col(2), col(3),
            cw(0), cw(1), cw(2),
            pl.BlockSpec((seq, LANES), lambda b, g: (b, 0)),
            pl.BlockSpec((rows_r, seq), lambda b, g: (0, b)),
            lane_vec, lane_vec, row_vec, row_vec,
            pl.BlockSpec((None, 1, DN_HEAD_DIM), lambda b, g: (layer, 0, 0)),
        ],
        out_specs=pl.BlockSpec((seq, bw), lambda b, g: (b, g)),
        out_shape=jax.ShapeDtypeStruct((n, width), BF16),
        scratch_shapes=[
            pltpu.VMEM((seq, bw), F32), pltpu.VMEM((seq, bw), F32), pltpu.VMEM((seq, bw), F32),
            pltpu.VMEM((seq, LANES), F32), pltpu.VMEM((seq, LANES), F32),
            pltpu.VMEM((seq, LANES), F32), pltpu.VMEM((seq, LANES), F32),
            pltpu.VMEM((rows_r, seq), F32),
            pltpu.VMEM((hb, seq, DN_HEAD_DIM), F32),
            pltpu.VMEM((hb, 2 * seq, DN_HEAD_DIM), BF16),
            pltpu.VMEM((hb, 2 * seq, DN_HEAD_DIM), BF16),
            pltpu.VMEM((hb, 16 * (seq // SUPER), DN_HEAD_DIM), F32),
            pltpu.VMEM((hb, seq, DN_HEAD_DIM), F32),
        ],
        compiler_params=_params(("parallel", "parallel")),
        name="gdn",
    )(proj, proj, proj, proj, conv_w, conv_w, conv_w, bdc, bdr, alc, dtc, alr, dtr, norm_g)


def _outproj_ln_kernel(x_ref, od_ref, on_ref, w1_ref, w2_ref, g_ref, b_ref, y_ref, *, alpha):
    mix = (jnp.dot(od_ref[...], w1_ref[...], preferred_element_type=F32)
           + jnp.dot(on_ref[...], w2_ref[...], preferred_element_type=F32))
    y_ref[...] = _layer_norm(alpha * x_ref[...] + mix, g_ref[...], b_ref[...])


def _outproj_ln(x, o_diff, o_dn, w_out, g, b, layer, *, alpha, tm):
    n, d = x.shape
    k1, k2 = o_diff.shape[1], o_dn.shape[1]
    assert k1 == k2
    vec = pl.BlockSpec((None, 1, d), lambda i: (layer, 0, 0))
    return pl.pallas_call(
        functools.partial(_outproj_ln_kernel, alpha=alpha),
        grid=(n // tm,),
        in_specs=[
            pl.BlockSpec((tm, d), lambda i: (i, 0)),
            pl.BlockSpec((tm, k1), lambda i: (i, 0)),
            pl.BlockSpec((tm, k2), lambda i: (i, 0)),
            pl.BlockSpec((None, k1, d), lambda i: (layer, 0, 0)),
            pl.BlockSpec((None, k2, d), lambda i: (layer, 1, 0)),
            vec, vec,
        ],
        out_specs=pl.BlockSpec((tm, d), lambda i: (i, 0)),
        out_shape=jax.ShapeDtypeStruct((n, d), F32),
        compiler_params=_params(("parallel",)),
        name="outproj_ln",
    )(x, o_diff, o_dn, w_out, w_out, g, b)


def _tile(n, pref):
    t = min(n, pref)
    assert n % t == 0, (n, pref)
    return t


def kernel(x, positions, ffn1_w_in, ffn1_w_out, ln1_g, ln1_b, w_in, conv_w, a_log, dt_bias, lam_q1, lam_k1, lam_q2, lam_k2, diff_norm_g, delta_norm_g, w_out, ln2_g, ln2_b, ffn2_w_in, ffn2_w_out, ln3_g, ln3_b):
    batch, seq, d = x.shape
    depth = ffn1_w_in.shape[0]
    d_ff = ffn1_w_out.shape[1]
    n = batch * seq
    diff_width = d // 2
    dn_width = d - diff_width
    n_diff_heads = diff_width // (2 * DIFF_HEAD_DIM)
    nh = dn_width // DN_HEAD_DIM
    qk_cols = 2 * (2 * n_diff_heads * DIFF_HEAD_DIM)
    v_cols = n_diff_heads * 2 * DIFF_HEAD_DIM
    main_cols = qk_cols + v_cols + 4 * dn_width
    assert w_in.shape[2] == main_cols + 2 * nh and 2 * nh <= LANES
    assert seq % SUPER == 0
    alpha = (2 * depth) ** 0.25

    tm = _tile(n, 512)
    tf = _tile(d_ff, 512)
    tn = _tile(v_cols, 512)
    tq = _tile(seq, 256)
    hb = 2 if nh % 2 == 0 else 1

    ffn1_in, ffn1_out = ffn1_w_in.astype(BF16), ffn1_w_out.astype(BF16)
    ffn2_in, ffn2_out = ffn2_w_in.astype(BF16), ffn2_w_out.astype(BF16)
    w_main = w_in[:, :, :main_cols].astype(BF16)
    w_gate = w_in[:, :, main_cols:].astype(BF16)
    w_bd = jnp.pad(w_gate, ((0, 0), (0, 0), (0, LANES - 2 * nh)))
    rows_r = -(-2 * nh // 8) * 8
    w_bd_t = jnp.pad(jnp.swapaxes(w_gate, 1, 2), ((0, 0), (0, rows_r - 2 * nh), (0, 0)))
    w_out_b = w_out.astype(BF16)
    pad_lane = lambda v: jnp.pad(v.astype(F32), ((0, 0), (nh, LANES - 2 * nh))).reshape(depth, 1, LANES)
    pad_row = lambda v: jnp.pad(v.astype(F32), ((0, 0), (nh, rows_r - 2 * nh))).reshape(depth, rows_r, 1)
    alc, dtc, alr, dtr = pad_lane(a_log), pad_lane(dt_bias), pad_row(a_log), pad_row(dt_bias)
    lam_vecs = jnp.stack([lam_q1, lam_k1, lam_q2, lam_k2], axis=1).astype(F32)
    vec3 = lambda v: v.reshape(depth, 1, v.shape[-1])
    ln1_g, ln1_b, ln2_g, ln2_b, ln3_g, ln3_b = map(vec3, (ln1_g, ln1_b, ln2_g, ln2_b, ln3_g, ln3_b))
    diff_g, delta_g = vec3(diff_norm_g), vec3(delta_norm_g)

    rope_c, rope_sa, rope_sb = _rope_tables(positions, tm)

    h = x.reshape(n, d)
    for l in range(depth):
        lam_init = 0.8 - 0.6 * math.exp(-0.3 * l)
        h, hb16 = _ffn_ln(h, ffn1_in, ffn1_out, ln1_g, ln1_b, l, alpha=alpha, tm=tm, tf=tf, emit_bf16=True)
        qk = _proj(hb16, w_main, l, 0, qk_cols, BF16, tm=tm, tn=tn,
                   rope=(rope_c, rope_sa, rope_sb, qk_cols // 2))
        v = _proj(hb16, w_main, l, qk_cols, v_cols, BF16, tm=tm, tn=tn)
        dn = _proj(hb16, w_main, l, qk_cols + v_cols, 4 * dn_width, F32, tm=tm, tn=tn)
        bdc, bdr = _proj_gates(hb16, w_bd, w_bd_t, l, tm=tm)
        o_diff = _attn(qk, v, lam_vecs, diff_g, l, batch=batch, seq=seq, n_heads=n_diff_heads, tq=tq,
                       lam_init=lam_init)
        o_dn = _gdn(dn, conv_w, bdc, bdr, alc, dtc, alr, dtr, delta_g, l, batch=batch, seq=seq, nh=nh, hb=hb)
        h = _outproj_ln(h, o_diff, o_dn, w_out_b, ln2_g, ln2_b, l, alpha=alpha, tm=tm)
        (h,) = _ffn_ln(h, ffn2_in, ffn2_out, ln3_g, ln3_b, l, alpha=alpha, tm=tm, tf=tf, emit_bf16=False)
    return h.reshape(batch, seq, d)
```

```python
import functools
import math

import jax
import jax.numpy as jnp
from jax import lax
from jax.experimental import pallas as pl
from jax.experimental.pallas import tpu as pltpu

F32 = jnp.float32
BF16 = jnp.bfloat16

LANES = 128
SUBLANES = 8
DIFF_HEAD_DIM = 64
DN_HEAD_DIM = 128
CONV_K = 4
CHUNK = 64
SUPER = 2 * CHUNK
GDN_CHAIN_GROUP = 16
ROPE_THETA = 500000.0
ROPE_DIM = DIFF_HEAD_DIM // 4
ROPE_HALF = ROPE_DIM // 2
LN_EPS = 1e-5
SUBLN_EPS = 1e-5
GATED_NORM_EPS = 1e-6
L2_EPS = 1e-6
VMEM_LIMIT_BYTES = 56 * 1024 * 1024

_NT = (((1,), (1,)), ((), ()))


def _params(sem):
    return pltpu.CompilerParams(dimension_semantics=sem, vmem_limit_bytes=VMEM_LIMIT_BYTES)


def _layer_norm(y, g, b):
    mu = jnp.mean(y, axis=-1, keepdims=True)
    yc = y - mu
    var = jnp.mean(yc * yc, axis=-1, keepdims=True)
    return yc * lax.rsqrt(var + LN_EPS) * g + b


def _silu(x):
    return x * jax.nn.sigmoid(x)


def _rope_table_kernel(pos_ref, inv_ref, c_ref, sa_ref, sb_ref):
    ang = pos_ref[...].astype(F32) * inv_ref[...]
    lane = lax.broadcasted_iota(jnp.int32, ang.shape, 1) % DIFF_HEAD_DIM
    cos, sin = jnp.cos(ang), jnp.sin(ang)
    c_ref[...] = jnp.where(lane < ROPE_DIM, cos, 1.0)
    sa_ref[...] = jnp.where(lane < ROPE_HALF, -sin, 0.0)
    sb_ref[...] = jnp.where((lane >= ROPE_HALF) & (lane < ROPE_DIM), sin, 0.0)


def _rope_tables(positions, tm):
    n = positions.size
    pos = positions.reshape(n, 1)
    inv_freq = ROPE_THETA ** (-jnp.arange(0, ROPE_DIM, 2, dtype=F32) / ROPE_DIM)
    lane = jnp.arange(LANES) % DIFF_HEAD_DIM
    inv = jnp.where(lane < ROPE_DIM, inv_freq[lane % ROPE_HALF], 0.0).reshape(1, LANES).astype(F32)
    out = jax.ShapeDtypeStruct((n, LANES), F32)
    spec = pl.BlockSpec((tm, LANES), lambda i: (i, 0))
    return pl.pallas_call(
        _rope_table_kernel,
        grid=(n // tm,),
        in_specs=[pl.BlockSpec((tm, 1), lambda i: (i, 0)), pl.BlockSpec((1, LANES), lambda i: (0, 0))],
        out_specs=[spec, spec, spec],
        out_shape=[out, out, out],
        compiler_params=_params(("parallel",)),
        name="rope_tables",
    )(pos, inv)


def _ffn_ln_kernel(x_ref, wg_ref, wu_ref, wo_ref, g_ref, b_ref, *rest, alpha, emit_bf16):
    if emit_bf16:
        y_ref, yb_ref, xb_ref, acc_ref = rest
    else:
        y_ref, xb_ref, acc_ref = rest
    j = pl.program_id(1)

    @pl.when(j == 0)
    def _init():
        xb_ref[...] = x_ref[...].astype(BF16)
        acc_ref[...] = jnp.zeros_like(acc_ref)

    xb = xb_ref[...]
    hg = jnp.dot(xb, wg_ref[...], preferred_element_type=F32)
    hu = jnp.dot(xb, wu_ref[...], preferred_element_type=F32)
    a = (_silu(hg) * hu).astype(BF16)
    acc_ref[...] += jnp.dot(a, wo_ref[...], preferred_element_type=F32)

    @pl.when(j == pl.num_programs(1) - 1)
    def _fin():
        y = _layer_norm(alpha * x_ref[...] + 0.5 * acc_ref[...], g_ref[...], b_ref[...])
        y_ref[...] = y
        if emit_bf16:
            yb_ref[...] = y.astype(BF16)


def _ffn_ln(x, w_in, w_out, g, b, layer, *, alpha, tm, tf, emit_bf16):
    n, d = x.shape
    f = w_out.shape[1]
    nf = f // tf
    row = pl.BlockSpec((tm, d), lambda i, j: (i, 0))
    vec = pl.BlockSpec((None, 1, d), lambda i, j: (layer, 0, 0))
    out_shape = [jax.ShapeDtypeStruct((n, d), F32)]
    out_specs = [row]
    if emit_bf16:
        out_shape.append(jax.ShapeDtypeStruct((n, d), BF16))
        out_specs.append(row)
    return pl.pallas_call(
        functools.partial(_ffn_ln_kernel, alpha=alpha, emit_bf16=emit_bf16),
        grid=(n // tm, nf),
        in_specs=[
            row,
            pl.BlockSpec((None, d, tf), lambda i, j: (layer, 0, j)),
            pl.BlockSpec((None, d, tf), lambda i, j: (layer, 0, j + nf)),
            pl.BlockSpec((None, tf, d), lambda i, j: (layer, j, 0)),
            vec, vec,
        ],
        out_specs=out_specs,
        out_shape=out_shape,
        scratch_shapes=[pltpu.VMEM((tm, d), BF16), pltpu.VMEM((tm, d), F32)],
        compiler_params=_params(("parallel", "arbitrary")),
        name="ffn_ln",
    )(x, w_in, w_in, w_out, g, b)


def _proj_qk_kernel(xb_ref, w_ref, c_ref, sa_ref, sb_ref, o_ref, *, n_q_tiles):
    acc = jnp.dot(xb_ref[...], w_ref[...], preferred_element_type=F32)
    scale = jnp.where(pl.program_id(1) < n_q_tiles, DIFF_HEAD_DIM ** -0.5, 1.0)
    c, sa, sb = c_ref[...] * scale, sa_ref[...] * scale, sb_ref[...] * scale
    for t in range(acc.shape[1] // LANES):
        seg = acc[:, t * LANES:(t + 1) * LANES]
        rot = seg * c + pltpu.roll(seg, LANES - ROPE_HALF, 1) * sa + pltpu.roll(seg, ROPE_HALF, 1) * sb
        o_ref[:, t * LANES:(t + 1) * LANES] = rot.astype(o_ref.dtype)


def _proj_plain_kernel(xb_ref, w_ref, o_ref):
    o_ref[...] = jnp.dot(xb_ref[...], w_ref[...], preferred_element_type=F32).astype(o_ref.dtype)


def _proj_gate_kernel(xb_ref, w_ref, wt_ref, oc_ref, or_ref):
    xb = xb_ref[...]
    oc_ref[...] = jnp.dot(xb, w_ref[...], preferred_element_type=F32)
    or_ref[...] = lax.dot_general(wt_ref[...], xb, _NT, preferred_element_type=F32)


def _proj(xb, w, layer, col0, ncols, out_dtype, *, tm, tn, rope=None):
    n, d = xb.shape
    xspec = pl.BlockSpec((tm, d), lambda i, j: (i, 0))
    wspec = pl.BlockSpec((None, d, tn), lambda i, j: (layer, 0, col0 // tn + j))
    ospec = pl.BlockSpec((tm, tn), lambda i, j: (i, j))
    if rope is None:
        body, in_specs, args = _proj_plain_kernel, [xspec, wspec], (xb, w)
    else:
        c, sa, sb, q_cols = rope
        tspec = pl.BlockSpec((tm, LANES), lambda i, j: (i, 0))
        body = functools.partial(_proj_qk_kernel, n_q_tiles=q_cols // tn)
        in_specs, args = [xspec, wspec, tspec, tspec, tspec], (xb, w, c, sa, sb)
    return pl.pallas_call(
        body,
        grid=(n // tm, ncols // tn),
        in_specs=in_specs,
        out_specs=ospec,
        out_shape=jax.ShapeDtypeStruct((n, ncols), out_dtype),
        compiler_params=_params(("parallel", "arbitrary")),
        name="proj",
    )(*args)


def _proj_gates(xb, w_bd, w_bd_t, layer, *, tm):
    n, d = xb.shape
    rows = w_bd_t.shape[1]
    return pl.pallas_call(
        _proj_gate_kernel,
        grid=(n // tm,),
        in_specs=[
            pl.BlockSpec((tm, d), lambda i: (i, 0)),
            pl.BlockSpec((None, d, LANES), lambda i: (layer, 0, 0)),
            pl.BlockSpec((None, rows, d), lambda i: (layer, 0, 0)),
        ],
        out_specs=[pl.BlockSpec((tm, LANES), lambda i: (i, 0)), pl.BlockSpec((rows, tm), lambda i: (0, i))],
        out_shape=[jax.ShapeDtypeStruct((n, LANES), F32), jax.ShapeDtypeStruct((rows, n), F32)],
        compiler_params=_params(("parallel",)),
        name="proj_gates",
    )(xb, w_bd, w_bd_t)


def _attn_kernel(lam_ref, q_ref, k_ref, v_ref, g_ref, o_ref, *, tq, lam_init):
    s_len = q_ref.shape[0]
    lq = lam_ref[...]
    lam = (jnp.exp(jnp.sum(lq[0:1] * lq[1:2], axis=-1, keepdims=True))
           - jnp.exp(jnp.sum(lq[2:3] * lq[3:4], axis=-1, keepdims=True)) + lam_init)
    lane = lax.broadcasted_iota(jnp.int32, (1, LANES), 1)
    map0 = lane < DIFF_HEAD_DIM
    gain = g_ref[...] * (1.0 - lam_init)
    for i in range(s_len // tq):
        skv = (i + 1) * tq
        q = q_ref[i * tq:(i + 1) * tq, :]
        zero = jnp.zeros_like(q)
        k = k_ref[0:skv, :]
        s0 = lax.dot_general(jnp.where(map0, q, zero), k, _NT, preferred_element_type=F32)
        s1 = lax.dot_general(jnp.where(map0, zero, q), k, _NT, preferred_element_type=F32)
        row = lax.broadcasted_iota(jnp.int32, (tq, skv), 0) + i * tq
        col = lax.broadcasted_iota(jnp.int32, (tq, skv), 1)
        keep = col <= row
        s0 = jnp.where(keep, s0, -jnp.inf)
        s1 = jnp.where(keep, s1, -jnp.inf)
        e0 = jnp.exp(s0 - jnp.max(s0, axis=-1, keepdims=True))
        e1 = jnp.exp(s1 - jnp.max(s1, axis=-1, keepdims=True))
        r0 = 1.0 / jnp.sum(e0, axis=-1, keepdims=True)
        r1 = lam / jnp.sum(e1, axis=-1, keepdims=True)
        p = (e0 * r0 - e1 * r1).astype(BF16)
        o = jnp.dot(p, v_ref[0:skv, :], preferred_element_type=F32)
        ms = jnp.mean(o * o, axis=-1, keepdims=True)
        o_ref[i * tq:(i + 1) * tq, :] = (o * lax.rsqrt(ms + SUBLN_EPS) * gain).astype(o_ref.dtype)


def _attn(qk, v, lam_vecs, norm_g, layer, *, batch, seq, n_heads, tq, lam_init):
    n = qk.shape[0]
    hd = 2 * DIFF_HEAD_DIM
    return pl.pallas_call(
        functools.partial(_attn_kernel, tq=tq, lam_init=lam_init),
        grid=(batch, n_heads),
        in_specs=[
            pl.BlockSpec((None, 4, DIFF_HEAD_DIM), lambda b, h: (layer, 0, 0)),
            pl.BlockSpec((seq, hd), lambda b, h: (b, h)),
            pl.BlockSpec((seq, hd), lambda b, h: (b, n_heads + h)),
            pl.BlockSpec((seq, hd), lambda b, h: (b, h)),
            pl.BlockSpec((None, 1, hd), lambda b, h: (layer, 0, 0)),
        ],
        out_specs=pl.BlockSpec((seq, hd), lambda b, h: (b, h)),
        out_shape=jax.ShapeDtypeStruct((n, n_heads * hd), BF16),
        compiler_params=_params(("parallel", "parallel")),
        name="diff_attn",
    )(lam_vecs, qk, qk, v, norm_g)


def _chunk_scan(x, pos, axis, reverse=False):
    size = x.shape[axis]
    s = 1
    while s < CHUNK:
        if reverse:
            x = x + jnp.where(pos < CHUNK - s, pltpu.roll(x, size - s, axis), 0.0)
        else:
            x = x + jnp.where(pos >= s, pltpu.roll(x, s, axis), 0.0)
        s *= 2
    return x


def _gdn_kernel(gq_ref, gk_ref, gv_ref, gz_ref, cw_ref, bdc_ref, bdr_ref,
                alc_ref, dtc_ref, alr_ref, dtr_ref, ng_ref, o_ref,
                hist_ref, state_ref, qn_ref, kn_ref, vc_ref, u_ref, wq_ref, akt_ref, oacc_ref, *, nh, group):
    rows_blk, width = gq_ref.shape
    n_super = rows_blk // SUPER

    @pl.when(pl.program_id(1) == 0)
    def _reset():
        hist_ref[...] = jnp.zeros_like(hist_ref)
        state_ref[...] = jnp.zeros_like(state_ref)

    rows8 = lax.broadcasted_iota(jnp.int32, (SUBLANES, 1), 0)

    def conv_silu(u_ref_, idx, dst_ref):
        w = cw_ref[:, idx * width:(idx + 1) * width]
        hist = hist_ref[idx]
        head = u_ref_[0:SUBLANES, :]
        acc = u_ref_[SUBLANES:rows_blk, :] * w[CONV_K - 1:CONV_K]
        top = head * w[CONV_K - 1:CONV_K]
        for j in range(CONV_K - 1):
            sh = CONV_K - 1 - j
            acc = acc + u_ref_[SUBLANES - sh:rows_blk - sh, :] * w[j:j + 1]
            top = top + jnp.where(rows8 < sh, pltpu.roll(hist, sh, 0), pltpu.roll(head, sh, 0)) * w[j:j + 1]
        hist_ref[idx] = u_ref_[rows_blk - SUBLANES:rows_blk, :]
        dst_ref[0:SUBLANES, :] = _silu(top)
        dst_ref[SUBLANES:rows_blk, :] = _silu(acc)

    conv_silu(gq_ref, 0, qn_ref)
    conv_silu(gk_ref, 1, kn_ref)
    conv_silu(gv_ref, 2, vc_ref)
    for h in range(nh):
        sl = slice(h * DN_HEAD_DIM, (h + 1) * DN_HEAD_DIM)
        qh, kh = qn_ref[:, sl], kn_ref[:, sl]
        qn_ref[:, sl] = qh * (lax.rsqrt(jnp.sum(qh * qh, axis=-1, keepdims=True) + L2_EPS) * DN_HEAD_DIM ** -0.5)
        kn_ref[:, sl] = kh * lax.rsqrt(jnp.sum(kh * kh, axis=-1, keepdims=True) + L2_EPS)

    rows = lax.broadcasted_iota(jnp.int32, (rows_blk, 1), 0)
    xg = bdc_ref[...]
    beta_all = jax.nn.sigmoid(xg)
    g_raw = -jnp.exp(alc_ref[...]) * jax.nn.softplus(xg + dtc_ref[...])
    g_cum = _chunk_scan(g_raw, rows % CHUNK, 0)
    eg_all = jnp.exp(g_cum)
    ekd_all = jnp.exp(_chunk_scan(g_raw, rows % CHUNK, 0, reverse=True) - g_raw)
    cols = lax.broadcasted_iota(jnp.int32, (1, rows_blk), 1)
    gr_raw = -jnp.exp(alr_ref[...]) * jax.nn.softplus(bdr_ref[...] + dtr_ref[...])
    g_row_all = _chunk_scan(gr_raw, cols % CHUNK, 1)

    ii = lax.broadcasted_iota(jnp.int32, (SUPER, SUPER), 0)
    jj = lax.broadcasted_iota(jnp.int32, (SUPER, SUPER), 1)
    same = (ii // CHUNK) == (jj // CHUNK)
    incl = same & (ii >= jj)
    strict = same & (ii > jj)
    eye = (ii == jj).astype(F32)

    def lower_left_mask(s):
        return ((ii // (2 * s)) == (jj // (2 * s))) & ((ii // s) % 2 == 1) & ((jj // s) % 2 == 0)

    lower_left = {}
    s = 1
    while s < CHUNK:
        lower_left[s] = lower_left_mask(s)
        s *= 2

    chunk_decay = {}
    chains = [(m, h) for m in range(n_super) for h in range(nh)]
    for g0 in range(0, len(chains), group):
        grp = chains[g0:g0 + group]
        lmats, ts, rhss = [], [], []
        for m, h in grp:
            rs = slice(m * SUPER, (m + 1) * SUPER)
            sl = slice(h * DN_HEAD_DIM, (h + 1) * DN_HEAD_DIM)
            gl = nh + h
            qn, kn, v = qn_ref[rs, sl], kn_ref[rs, sl], vc_ref[rs, sl]
            beta, eg, ekd = beta_all[rs, h:h + 1], eg_all[rs, gl:gl + 1], ekd_all[rs, gl:gl + 1]
            decay = jnp.where(
                incl, jnp.exp(jnp.where(incl, g_cum[rs, gl:gl + 1] - g_row_all[gl:gl + 1, rs], 0.0)), 0.0)
            kb = kn * beta
            knb = kn.astype(BF16)
            kk = lax.dot_general(kb.astype(BF16), knb, _NT, preferred_element_type=F32)
            qk = lax.dot_general(qn.astype(BF16), knb, _NT, preferred_element_type=F32)
            lmat = jnp.where(strict, kk * decay, 0.0)
            lmats.append(lmat)
            ts.append(eye - jnp.where(lower_left[1], lmat, 0.0))
            rhss.append(jnp.concatenate([v * beta, kb * eg], axis=1).astype(BF16))
            qd = (qn * eg).astype(BF16)
            base = 2 * SUPER * m
            for c2 in range(2):
                wq_ref[base + c2 * SUPER + CHUNK:base + (c2 + 1) * SUPER, sl] = qd[c2 * CHUNK:(c2 + 1) * CHUNK]
                last = (c2 + 1) * CHUNK - 1
                chunk_decay[m, c2, h] = eg[last:last + 1]
            akt_ref[base:base + SUPER, sl] = (qk * decay).astype(BF16)
            akt_ref[base + SUPER:base + 2 * SUPER, sl] = (kn * ekd).T.astype(BF16)
        s = 2
        while s < CHUNK:
            ys = [jnp.dot(jnp.where(lower_left[s], lmat, 0.0).astype(BF16), t.astype(BF16),
                          preferred_element_type=F32) for lmat, t in zip(lmats, ts)]
            ts = [t - jnp.dot(t.astype(BF16), y.astype(BF16), preferred_element_type=F32)
                  for t, y in zip(ts, ys)]
            s *= 2
        for (m, h), t, rhs in zip(grp, ts, rhss):
            rs = slice(m * SUPER, (m + 1) * SUPER)
            sl = slice(h * DN_HEAD_DIM, (h + 1) * DN_HEAD_DIM)
            sol = jnp.dot(t.astype(BF16), rhs, preferred_element_type=F32)
            u_ref[rs, sl] = sol[:, :DN_HEAD_DIM]
            w = sol[:, DN_HEAD_DIM:].astype(BF16)
            base = 2 * SUPER * m
            for c2 in range(2):
                wq_ref[base + c2 * SUPER:base + c2 * SUPER + CHUNK, sl] = w[c2 * CHUNK:(c2 + 1) * CHUNK]

    zeros_half = jnp.zeros((CHUNK, DN_HEAD_DIM), BF16)
    states = [state_ref[h] for h in range(nh)]
    for m in range(n_super):
        base = 2 * SUPER * m
        for c2 in range(2):
            r0 = m * SUPER + c2 * CHUNK
            for h in range(nh):
                sl = slice(h * DN_HEAD_DIM, (h + 1) * DN_HEAD_DIM)
                wq = wq_ref[base + c2 * SUPER:base + (c2 + 1) * SUPER, sl]
                res1 = jnp.dot(wq, states[h].astype(BF16), preferred_element_type=F32)
                vb = (u_ref[r0:r0 + CHUNK, sl] - res1[:CHUNK]).astype(BF16)
                rhs = jnp.concatenate([vb, zeros_half] if c2 == 0 else [zeros_half, vb], axis=0)
                res2 = jnp.dot(akt_ref[base:base + 2 * SUPER, sl], rhs, preferred_element_type=F32)
                oacc_ref[r0:r0 + CHUNK, sl] = res1[CHUNK:] + res2[c2 * CHUNK:(c2 + 1) * CHUNK]
                states[h] = states[h] * chunk_decay[m, c2, h] + res2[SUPER:]
    for h in range(nh):
        state_ref[h] = states[h]

    for h in range(nh):
        sl = slice(h * DN_HEAD_DIM, (h + 1) * DN_HEAD_DIM)
        o = oacc_ref[:, sl]
        ms = jnp.mean(o * o, axis=-1, keepdims=True)
        o_ref[:, sl] = (o * lax.rsqrt(ms + GATED_NORM_EPS) * ng_ref[...] * _silu(gz_ref[:, sl])).astype(o_ref.dtype)


def _gdn(proj, conv_w, bdc, bdr, alc, dtc, alr, dtr, norm_g, layer, *, batch, seq, nh, rows_blk):
    n = proj.shape[0]
    width = nh * DN_HEAD_DIM
    nblk = seq // rows_blk
    rows_r = bdr.shape[0]

    def col(off):
        return pl.BlockSpec((rows_blk, width), lambda b, t: (b * nblk + t, off))

    lane_vec = pl.BlockSpec((None, 1, LANES), lambda b, t: (layer, 0, 0))
    row_vec = pl.BlockSpec((None, rows_r, 1), lambda b, t: (layer, 0, 0))
    return pl.pallas_call(
        functools.partial(_gdn_kernel, nh=nh, group=GDN_CHAIN_GROUP),
        grid=(batch, nblk),
        in_specs=[
            col(0), col(1), col(2), col(3),
            pl.BlockSpec((None, CONV_K, 3 * width), lambda b, t: (layer, 0, 0)),
            pl.BlockSpec((rows_blk, LANES), lambda b, t: (b * nblk + t, 0)),
            pl.BlockSpec((rows_r, rows_blk), lambda b, t: (0, b * nblk + t)),
            lane_vec, lane_vec, row_vec, row_vec,
            pl.BlockSpec((None, 1, DN_HEAD_DIM), lambda b, t: (layer, 0, 0)),
        ],
        out_specs=pl.BlockSpec((rows_blk, width), lambda b, t: (b * nblk + t, 0)),
        out_shape=jax.ShapeDtypeStruct((n, width), BF16),
        scratch_shapes=[
            pltpu.VMEM((3, SUBLANES, width), F32),
            pltpu.VMEM((nh, DN_HEAD_DIM, DN_HEAD_DIM), F32),
            pltpu.VMEM((rows_blk, width), F32),
            pltpu.VMEM((rows_blk, width), F32),
            pltpu.VMEM((rows_blk, width), F32),
            pltpu.VMEM((rows_blk, width), F32),
            pltpu.VMEM((2 * rows_blk, width), BF16),
            pltpu.VMEM((2 * rows_blk, width), BF16),
            pltpu.VMEM((rows_blk, width), F32),
        ],
        compiler_params=_params(("parallel", "arbitrary")),
        name="gdn",
    )(proj, proj, proj, proj, conv_w, bdc, bdr, alc, dtc, alr, dtr, norm_g)


def _outproj_ln_kernel(x_ref, od_ref, on_ref, w1_ref, w2_ref, g_ref, b_ref, y_ref, *, alpha):
    mix = (jnp.dot(od_ref[...], w1_ref[...], preferred_element_type=F32)
           + jnp.dot(on_ref[...], w2_ref[...], preferred_element_type=F32))
    y_ref[...] = _layer_norm(alpha * x_ref[...] + mix, g_ref[...], b_ref[...])


def _outproj_ln(x, o_diff, o_dn, w_out, g, b, layer, *, alpha, tm):
    n, d = x.shape
    k1, k2 = o_diff.shape[1], o_dn.shape[1]
    assert k1 == k2
    vec = pl.BlockSpec((None, 1, d), lambda i: (layer, 0, 0))
    return pl.pallas_call(
        functools.partial(_outproj_ln_kernel, alpha=alpha),
        grid=(n // tm,),
        in_specs=[
            pl.BlockSpec((tm, d), lambda i: (i, 0)),
            pl.BlockSpec((tm, k1), lambda i: (i, 0)),
            pl.BlockSpec((tm, k2), lambda i: (i, 0)),
            pl.BlockSpec((None, k1, d), lambda i: (layer, 0, 0)),
            pl.BlockSpec((None, k2, d), lambda i: (layer, 1, 0)),
            vec, vec,
        ],
        out_specs=pl.BlockSpec((tm, d), lambda i: (i, 0)),
        out_shape=jax.ShapeDtypeStruct((n, d), F32),
        compiler_params=_params(("parallel",)),
        name="outproj_ln",
    )(x, o_diff, o_dn, w_out, w_out, g, b)


def _tile(n, pref):
    t = min(n, pref)
    assert n % t == 0, (n, pref)
    return t


def kernel(x, positions, ffn1_w_in, ffn1_w_out, ln1_g, ln1_b, w_in, conv_w, a_log, dt_bias, lam_q1, lam_k1, lam_q2, lam_k2, diff_norm_g, delta_norm_g, w_out, ln2_g, ln2_b, ffn2_w_in, ffn2_w_out, ln3_g, ln3_b):
    batch, seq, d = x.shape
    depth = ffn1_w_in.shape[0]
    d_ff = ffn1_w_out.shape[1]
    n = batch * seq
    diff_width = d // 2
    dn_width = d - diff_width
    n_diff_heads = diff_width // (2 * DIFF_HEAD_DIM)
    nh = dn_width // DN_HEAD_DIM
    qk_cols = 2 * (2 * n_diff_heads * DIFF_HEAD_DIM)
    v_cols = n_diff_heads * 2 * DIFF_HEAD_DIM
    main_cols = qk_cols + v_cols + 4 * dn_width
    assert w_in.shape[2] == main_cols + 2 * nh and 2 * nh <= LANES
    assert seq % SUPER == 0
    alpha = (2 * depth) ** 0.25

    tm = _tile(n, 512)
    tf = _tile(d_ff, 512)
    tn = _tile(v_cols, 512)
    tq = _tile(seq, 256)
    gdn_rows = _tile(seq, 512)

    ffn1_in, ffn1_out = ffn1_w_in.astype(BF16), ffn1_w_out.astype(BF16)
    ffn2_in, ffn2_out = ffn2_w_in.astype(BF16), ffn2_w_out.astype(BF16)
    w_main = w_in[:, :, :main_cols].astype(BF16)
    w_gate = w_in[:, :, main_cols:].astype(BF16)
    w_bd = jnp.pad(w_gate, ((0, 0), (0, 0), (0, LANES - 2 * nh)))
    rows_r = -(-2 * nh // SUBLANES) * SUBLANES
    w_bd_t = jnp.pad(jnp.swapaxes(w_gate, 1, 2), ((0, 0), (0, rows_r - 2 * nh), (0, 0)))
    w_out_b = w_out.astype(BF16)
    pad_lane = lambda v: jnp.pad(v.astype(F32), ((0, 0), (nh, LANES - 2 * nh))).reshape(depth, 1, LANES)
    pad_row = lambda v: jnp.pad(v.astype(F32), ((0, 0), (nh, rows_r - 2 * nh))).reshape(depth, rows_r, 1)
    alc, dtc, alr, dtr = pad_lane(a_log), pad_lane(dt_bias), pad_row(a_log), pad_row(dt_bias)
    lam_vecs = jnp.stack([lam_q1, lam_k1, lam_q2, lam_k2], axis=1).astype(F32)
    vec3 = lambda v: v.reshape(depth, 1, v.shape[-1])
    ln1_g, ln1_b, ln2_g, ln2_b, ln3_g, ln3_b = map(vec3, (ln1_g, ln1_b, ln2_g, ln2_b, ln3_g, ln3_b))
    diff_g, delta_g = vec3(diff_norm_g), vec3(delta_norm_g)

    rope_c, rope_sa, rope_sb = _rope_tables(positions, tm)

    h = x.reshape(n, d)
    for l in range(depth):
        lam_init = 0.8 - 0.6 * math.exp(-0.3 * l)
        h, hb16 = _ffn_ln(h, ffn1_in, ffn1_out, ln1_g, ln1_b, l, alpha=alpha, tm=tm, tf=tf, emit_bf16=True)
        qk = _proj(hb16, w_main, l, 0, qk_cols, BF16, tm=tm, tn=tn,
                   rope=(rope_c, rope_sa, rope_sb, qk_cols // 2))
        v = _proj(hb16, w_main, l, qk_cols, v_cols, BF16, tm=tm, tn=tn)
        dn = _proj(hb16, w_main, l, qk_cols + v_cols, 4 * dn_width, F32, tm=tm, tn=tn)
        bdc, bdr = _proj_gates(hb16, w_bd, w_bd_t, l, tm=tm)
        o_diff = _attn(qk, v, lam_vecs, diff_g, l, batch=batch, seq=seq, n_heads=n_diff_heads, tq=tq,
                       lam_init=lam_init)
        o_dn = _gdn(dn, conv_w, bdc, bdr, alc, dtc, alr, dtr, delta_g, l, batch=batch, seq=seq, nh=nh,
                    rows_blk=gdn_rows)
        h = _outproj_ln(h, o_diff, o_dn, w_out_b, ln2_g, ln2_b, l, alpha=alpha, tm=tm)
        (h,) = _ffn_ln(h, ffn2_in, ffn2_out, ln3_g, ln3_b, l, alpha=alpha, tm=tm, tf=tf, emit_bf16=False)
    return h.reshape(batch, seq, d)
```

```python
import functools
import math

import jax
import jax.numpy as jnp
from jax import lax
from jax.experimental import pallas as pl
from jax.experimental.pallas import tpu as pltpu

F32 = jnp.float32
BF16 = jnp.bfloat16

LANES = 128
SUBLANES = 8
DIFF_HEAD_DIM = 64
DN_HEAD_DIM = 128
CONV_K = 4
CHUNK = 64
SUPER = 2 * CHUNK
GDN_CHAIN_GROUP = 16
ROPE_THETA = 500000.0
ROPE_DIM = DIFF_HEAD_DIM // 4
ROPE_HALF = ROPE_DIM // 2
LN_EPS = 1e-5
SUBLN_EPS = 1e-5
GATED_NORM_EPS = 1e-6
L2_EPS = 1e-6
VMEM_LIMIT_BYTES = 56 * 1024 * 1024

_NT = (((1,), (1,)), ((), ()))


def _params(sem):
    return pltpu.CompilerParams(dimension_semantics=sem, vmem_limit_bytes=VMEM_LIMIT_BYTES)


def _layer_norm(y, g, b):
    mu = jnp.mean(y, axis=-1, keepdims=True)
    yc = y - mu
    var = jnp.mean(yc * yc, axis=-1, keepdims=True)
    return yc * lax.rsqrt(var + LN_EPS) * g + b


def _silu(x):
    return x * jax.nn.sigmoid(x)


def _rope_table_kernel(pos_ref, inv_ref, c_ref, sa_ref, sb_ref):
    ang = pos_ref[...].astype(F32) * inv_ref[...]
    lane = lax.broadcasted_iota(jnp.int32, ang.shape, 1) % DIFF_HEAD_DIM
    cos, sin = jnp.cos(ang), jnp.sin(ang)
    c_ref[...] = jnp.where(lane < ROPE_DIM, cos, 1.0)
    sa_ref[...] = jnp.where(lane < ROPE_HALF, -sin, 0.0)
    sb_ref[...] = jnp.where((lane >= ROPE_HALF) & (lane < ROPE_DIM), sin, 0.0)


def _rope_tables(positions, tm):
    n = positions.size
    pos = positions.reshape(n, 1)
    inv_freq = ROPE_THETA ** (-jnp.arange(0, ROPE_DIM, 2, dtype=F32) / ROPE_DIM)
    lane = jnp.arange(LANES) % DIFF_HEAD_DIM
    inv = jnp.where(lane < ROPE_DIM, inv_freq[lane % ROPE_HALF], 0.0).reshape(1, LANES).astype(F32)
    out = jax.ShapeDtypeStruct((n, LANES), F32)
    spec = pl.BlockSpec((tm, LANES), lambda i: (i, 0))
    return pl.pallas_call(
        _rope_table_kernel,
        grid=(n // tm,),
        in_specs=[pl.BlockSpec((tm, 1), lambda i: (i, 0)), pl.BlockSpec((1, LANES), lambda i: (0, 0))],
        out_specs=[spec, spec, spec],
        out_shape=[out, out, out],
        compiler_params=_params(("parallel",)),
        name="rope_tables",
    )(pos, inv)


def _ffn_ln_kernel(x_ref, wg_ref, wu_ref, wo_ref, g_ref, b_ref, *rest, alpha, emit_bf16):
    if emit_bf16:
        y_ref, yb_ref, xb_ref, acc_ref = rest
    else:
        y_ref, xb_ref, acc_ref = rest
    j = pl.program_id(1)

    @pl.when(j == 0)
    def _init():
        xb_ref[...] = x_ref[...].astype(BF16)
        acc_ref[...] = jnp.zeros_like(acc_ref)

    xb = xb_ref[...]
    hg = jnp.dot(xb, wg_ref[...], preferred_element_type=F32)
    hu = jnp.dot(xb, wu_ref[...], preferred_element_type=F32)
    a = (_silu(hg) * hu).astype(BF16)
    acc_ref[...] += jnp.dot(a, wo_ref[...], preferred_element_type=F32)

    @pl.when(j == pl.num_programs(1) - 1)
    def _fin():
        y = _layer_norm(alpha * x_ref[...] + 0.5 * acc_ref[...], g_ref[...], b_ref[...])
        y_ref[...] = y
        if emit_bf16:
            yb_ref[...] = y.astype(BF16)


def _ffn_ln(x, w_in, w_out, g, b, layer, *, alpha, tm, tf, emit_bf16):
    n, d = x.shape
    f = w_out.shape[1]
    nf = f // tf
    row = pl.BlockSpec((tm, d), lambda i, j: (i, 0))
    vec = pl.BlockSpec((None, 1, d), lambda i, j: (layer, 0, 0))
    out_shape = [jax.ShapeDtypeStruct((n, d), F32)]
    out_specs = [row]
    if emit_bf16:
        out_shape.append(jax.ShapeDtypeStruct((n, d), BF16))
        out_specs.append(row)
    return pl.pallas_call(
        functools.partial(_ffn_ln_kernel, alpha=alpha, emit_bf16=emit_bf16),
        grid=(n // tm, nf),
        in_specs=[
            row,
            pl.BlockSpec((None, d, tf), lambda i, j: (layer, 0, j)),
            pl.BlockSpec((None, d, tf), lambda i, j: (layer, 0, j + nf)),
            pl.BlockSpec((None, tf, d), lambda i, j: (layer, j, 0)),
            vec, vec,
        ],
        out_specs=out_specs,
        out_shape=out_shape,
        scratch_shapes=[pltpu.VMEM((tm, d), BF16), pltpu.VMEM((tm, d), F32)],
        compiler_params=_params(("parallel", "arbitrary")),
        name="ffn_ln",
    )(x, w_in, w_in, w_out, g, b)


def _proj_kernel(xb_ref, w_ref, wbd_ref, c_ref, sa_ref, sb_ref,
                 qk_ref, v_ref, dn_ref, bdc_ref, bdr_ref, *, n_q, n_qk, n_v):
    j = pl.program_id(1)
    xb = xb_ref[...]
    acc = jnp.dot(xb, w_ref[...], preferred_element_type=F32)

    @pl.when(j == 0)
    def _gates():
        gates = jnp.dot(xb, wbd_ref[...], preferred_element_type=F32)
        bdc_ref[...] = gates
        bdr_ref[...] = gates.T[0:bdr_ref.shape[0], :]

    @pl.when(j < n_qk)
    def _qk():
        scale = jnp.where(j < n_q, DIFF_HEAD_DIM ** -0.5, 1.0)
        c, sa, sb = c_ref[...] * scale, sa_ref[...] * scale, sb_ref[...] * scale
        for t in range(acc.shape[1] // LANES):
            seg = acc[:, t * LANES:(t + 1) * LANES]
            rot = seg * c + pltpu.roll(seg, LANES - ROPE_HALF, 1) * sa + pltpu.roll(seg, ROPE_HALF, 1) * sb
            qk_ref[:, t * LANES:(t + 1) * LANES] = rot.astype(qk_ref.dtype)

    @pl.when((j >= n_qk) & (j < n_qk + n_v))
    def _v():
        v_ref[...] = acc.astype(v_ref.dtype)

    @pl.when(j >= n_qk + n_v)
    def _dn():
        dn_ref[...] = acc


def _proj(xb, w_main, w_bd, rope, layer, *, rows, qk_cols, v_cols, dn_cols, tm, tn):
    n, d = xb.shape
    n_qk, n_v, n_dn = qk_cols // tn, v_cols // tn, dn_cols // tn
    c, sa, sb = rope
    tspec = pl.BlockSpec((tm, LANES), lambda i, j: (i, 0))
    return pl.pallas_call(
        functools.partial(_proj_kernel, n_q=n_qk // 2, n_qk=n_qk, n_v=n_v),
        grid=(n // tm, n_qk + n_v + n_dn),
        in_specs=[
            pl.BlockSpec((tm, d), lambda i, j: (i, 0)),
            pl.BlockSpec((None, d, tn), lambda i, j: (layer, 0, j)),
            pl.BlockSpec((None, d, LANES), lambda i, j: (layer, 0, 0)),
            tspec, tspec, tspec,
        ],
        out_specs=[
            pl.BlockSpec((tm, tn), lambda i, j: (i, jnp.minimum(j, n_qk - 1))),
            pl.BlockSpec((tm, tn), lambda i, j: (i, jnp.clip(j - n_qk, 0, n_v - 1))),
            pl.BlockSpec((tm, tn), lambda i, j: (i, jnp.clip(j - n_qk - n_v, 0, n_dn - 1))),
            pl.BlockSpec((tm, LANES), lambda i, j: (i, 0)),
            pl.BlockSpec((rows, tm), lambda i, j: (0, i)),
        ],
        out_shape=[
            jax.ShapeDtypeStruct((n, qk_cols), BF16),
            jax.ShapeDtypeStruct((n, v_cols), BF16),
            jax.ShapeDtypeStruct((n, dn_cols), F32),
            jax.ShapeDtypeStruct((n, LANES), F32),
            jax.ShapeDtypeStruct((rows, n), F32),
        ],
        compiler_params=_params(("parallel", "arbitrary")),
        name="proj",
    )(xb, w_main, w_bd, c, sa, sb)


def _attn_kernel(lam_ref, q_ref, k_ref, v_ref, g_ref, o_ref, *, tq, lam_init):
    s_len = q_ref.shape[0]
    lq = lam_ref[...]
    lam = (jnp.exp(jnp.sum(lq[0:1] * lq[1:2], axis=-1, keepdims=True))
           - jnp.exp(jnp.sum(lq[2:3] * lq[3:4], axis=-1, keepdims=True)) + lam_init)
    lane = lax.broadcasted_iota(jnp.int32, (1, LANES), 1)
    map0 = lane < DIFF_HEAD_DIM
    gain = g_ref[...] * (1.0 - lam_init)
    n_blk = s_len // tq

    def scores(i):
        skv = (i + 1) * tq
        q = q_ref[i * tq:(i + 1) * tq, :]
        zero = jnp.zeros_like(q)
        k = k_ref[0:skv, :]
        s0 = lax.dot_general(jnp.where(map0, q, zero), k, _NT, preferred_element_type=F32)
        s1 = lax.dot_general(jnp.where(map0, zero, q), k, _NT, preferred_element_type=F32)
        row = lax.broadcasted_iota(jnp.int32, (tq, skv), 0) + i * tq
        col = lax.broadcasted_iota(jnp.int32, (tq, skv), 1)
        keep = col <= row
        return jnp.where(keep, s0, -jnp.inf), jnp.where(keep, s1, -jnp.inf)

    def probs(s0, s1):
        e0 = jnp.exp(s0 - jnp.max(s0, axis=-1, keepdims=True))
        e1 = jnp.exp(s1 - jnp.max(s1, axis=-1, keepdims=True))
        r0 = 1.0 / jnp.sum(e0, axis=-1, keepdims=True)
        r1 = lam / jnp.sum(e1, axis=-1, keepdims=True)
        return (e0 * r0 - e1 * r1).astype(BF16)

    def emit(i, p):
        o = jnp.dot(p, v_ref[0:(i + 1) * tq, :], preferred_element_type=F32)
        ms = jnp.mean(o * o, axis=-1, keepdims=True)
        o_ref[i * tq:(i + 1) * tq, :] = (o * lax.rsqrt(ms + SUBLN_EPS) * gain).astype(o_ref.dtype)

    s_next = scores(0)
    for i in range(n_blk):
        s_cur = s_next
        if i + 1 < n_blk:
            s_next = scores(i + 1)
        emit(i, probs(*s_cur))


def _attn(qk, v, lam_vecs, norm_g, layer, *, batch, seq, n_heads, tq, lam_init):
    n = qk.shape[0]
    hd = 2 * DIFF_HEAD_DIM
    return pl.pallas_call(
        functools.partial(_attn_kernel, tq=tq, lam_init=lam_init),
        grid=(batch, n_heads),
        in_specs=[
            pl.BlockSpec((None, 4, DIFF_HEAD_DIM), lambda b, h: (layer, 0, 0)),
            pl.BlockSpec((seq, hd), lambda b, h: (b, h)),
            pl.BlockSpec((seq, hd), lambda b, h: (b, n_heads + h)),
            pl.BlockSpec((seq, hd), lambda b, h: (b, h)),
            pl.BlockSpec((None, 1, hd), lambda b, h: (layer, 0, 0)),
        ],
        out_specs=pl.BlockSpec((seq, hd), lambda b, h: (b, h)),
        out_shape=jax.ShapeDtypeStruct((n, n_heads * hd), BF16),
        compiler_params=_params(("parallel", "parallel")),
        name="diff_attn",
    )(lam_vecs, qk, qk, v, norm_g)


def _chunk_scan(x, pos, axis, reverse=False):
    size = x.shape[axis]
    s = 1
    while s < CHUNK:
        if reverse:
            x = x + jnp.where(pos < CHUNK - s, pltpu.roll(x, size - s, axis), 0.0)
        else:
            x = x + jnp.where(pos >= s, pltpu.roll(x, s, axis), 0.0)
        s *= 2
    return x


def _gdn_kernel(gq_ref, gk_ref, gv_ref, gz_ref, cw_ref, bdc_ref, bdr_ref,
                alc_ref, dtc_ref, alr_ref, dtr_ref, ng_ref, o_ref,
                hist_ref, state_ref, qn_ref, kn_ref, vc_ref, u_ref, wq_ref, akt_ref, oacc_ref, *, nh, group):
    rows_blk, width = gq_ref.shape
    n_super = rows_blk // SUPER

    @pl.when(pl.program_id(1) == 0)
    def _reset():
        hist_ref[...] = jnp.zeros_like(hist_ref)
        state_ref[...] = jnp.zeros_like(state_ref)

    rows8 = lax.broadcasted_iota(jnp.int32, (SUBLANES, 1), 0)

    def conv_silu(u_ref_, idx, dst_ref):
        w = cw_ref[:, idx * width:(idx + 1) * width]
        hist = hist_ref[idx]
        head = u_ref_[0:SUBLANES, :]
        acc = u_ref_[SUBLANES:rows_blk, :] * w[CONV_K - 1:CONV_K]
        top = head * w[CONV_K - 1:CONV_K]
        for j in range(CONV_K - 1):
            sh = CONV_K - 1 - j
            acc = acc + u_ref_[SUBLANES - sh:rows_blk - sh, :] * w[j:j + 1]
            top = top + jnp.where(rows8 < sh, pltpu.roll(hist, sh, 0), pltpu.roll(head, sh, 0)) * w[j:j + 1]
        hist_ref[idx] = u_ref_[rows_blk - SUBLANES:rows_blk, :]
        dst_ref[0:SUBLANES, :] = _silu(top)
        dst_ref[SUBLANES:rows_blk, :] = _silu(acc)

    conv_silu(gq_ref, 0, qn_ref)
    conv_silu(gk_ref, 1, kn_ref)
    conv_silu(gv_ref, 2, vc_ref)
    for h in range(nh):
        sl = slice(h * DN_HEAD_DIM, (h + 1) * DN_HEAD_DIM)
        qh, kh = qn_ref[:, sl], kn_ref[:, sl]
        qn_ref[:, sl] = qh * (lax.rsqrt(jnp.sum(qh * qh, axis=-1, keepdims=True) + L2_EPS) * DN_HEAD_DIM ** -0.5)
        kn_ref[:, sl] = kh * lax.rsqrt(jnp.sum(kh * kh, axis=-1, keepdims=True) + L2_EPS)

    rows = lax.broadcasted_iota(jnp.int32, (rows_blk, 1), 0)
    xg = bdc_ref[...]
    beta_all = jax.nn.sigmoid(xg)
    g_raw = -jnp.exp(alc_ref[...]) * jax.nn.softplus(xg + dtc_ref[...])
    g_cum = _chunk_scan(g_raw, rows % CHUNK, 0)
    eg_all = jnp.exp(g_cum)
    ekd_all = jnp.exp(_chunk_scan(g_raw, rows % CHUNK, 0, reverse=True) - g_raw)
    cols = lax.broadcasted_iota(jnp.int32, (1, rows_blk), 1)
    gr_raw = -jnp.exp(alr_ref[...]) * jax.nn.softplus(bdr_ref[...] + dtr_ref[...])
    g_row_all = _chunk_scan(gr_raw, cols % CHUNK, 1)

    ii = lax.broadcasted_iota(jnp.int32, (SUPER, SUPER), 0)
    jj = lax.broadcasted_iota(jnp.int32, (SUPER, SUPER), 1)
    same = (ii // CHUNK) == (jj // CHUNK)
    incl = same & (ii >= jj)
    strict = same & (ii > jj)
    eye = (ii == jj).astype(F32)

    def lower_left_mask(s):
        return ((ii // (2 * s)) == (jj // (2 * s))) & ((ii // s) % 2 == 1) & ((jj // s) % 2 == 0)

    lower_left = {}
    s = 1
    while s < CHUNK:
        lower_left[s] = lower_left_mask(s)
        s *= 2

    chunk_decay = {}
    chains = [(m, h) for m in range(n_super) for h in range(nh)]
    for g0 in range(0, len(chains), group):
        grp = chains[g0:g0 + group]
        lmats, ts, rhss = [], [], []
        for m, h in grp:
            rs = slice(m * SUPER, (m + 1) * SUPER)
            sl = slice(h * DN_HEAD_DIM, (h + 1) * DN_HEAD_DIM)
            gl = nh + h
            qn, kn, v = qn_ref[rs, sl], kn_ref[rs, sl], vc_ref[rs, sl]
            beta, eg, ekd = beta_all[rs, h:h + 1], eg_all[rs, gl:gl + 1], ekd_all[rs, gl:gl + 1]
            decay = jnp.where(
                incl, jnp.exp(jnp.where(incl, g_cum[rs, gl:gl + 1] - g_row_all[gl:gl + 1, rs], 0.0)), 0.0)
            kb = kn * beta
            knb = kn.astype(BF16)
            kk = lax.dot_general(kb.astype(BF16), knb, _NT, preferred_element_type=F32)
            qk = lax.dot_general(qn.astype(BF16), knb, _NT, preferred_element_type=F32)
            lmat = jnp.where(strict, kk * decay, 0.0)
            lmats.append(lmat)
            ts.append(eye - jnp.where(lower_left[1], lmat, 0.0))
            rhss.append(jnp.concatenate([v * beta, kb * eg], axis=1).astype(BF16))
            qd = (qn * eg).astype(BF16)
            base = 2 * SUPER * m
            for c2 in range(2):
                wq_ref[base + c2 * SUPER + CHUNK:base + (c2 + 1) * SUPER, sl] = qd[c2 * CHUNK:(c2 + 1) * CHUNK]
                last = (c2 + 1) * CHUNK - 1
                chunk_decay[m, c2, h] = eg[last:last + 1]
            akt_ref[base:base + SUPER, sl] = (qk * decay).astype(BF16)
            akt_ref[base + SUPER:base + 2 * SUPER, sl] = (kn * ekd).T.astype(BF16)
        s = 2
        while s < CHUNK:
            ys = [jnp.dot(jnp.where(lower_left[s], lmat, 0.0).astype(BF16), t.astype(BF16),
                          preferred_element_type=F32) for lmat, t in zip(lmats, ts)]
            ts = [t - jnp.dot(t.astype(BF16), y.astype(BF16), preferred_element_type=F32)
                  for t, y in zip(ts, ys)]
            s *= 2
        for (m, h), t, rhs in zip(grp, ts, rhss):
            rs = slice(m * SUPER, (m + 1) * SUPER)
            sl = slice(h * DN_HEAD_DIM, (h + 1) * DN_HEAD_DIM)
            sol = jnp.dot(t.astype(BF16), rhs, preferred_element_type=F32)
            u_ref[rs, sl] = sol[:, :DN_HEAD_DIM]
            w = sol[:, DN_HEAD_DIM:].astype(BF16)
            base = 2 * SUPER * m
            for c2 in range(2):
                wq_ref[base + c2 * SUPER:base + c2 * SUPER + CHUNK, sl] = w[c2 * CHUNK:(c2 + 1) * CHUNK]

    zeros_half = jnp.zeros((CHUNK, DN_HEAD_DIM), BF16)
    states = [state_ref[h] for h in range(nh)]
    for m in range(n_super):
        base = 2 * SUPER * m
        for c2 in range(2):
            r0 = m * SUPER + c2 * CHUNK
            for h in range(nh):
                sl = slice(h * DN_HEAD_DIM, (h + 1) * DN_HEAD_DIM)
                wq = wq_ref[base + c2 * SUPER:base + (c2 + 1) * SUPER, sl]
                res1 = jnp.dot(wq, states[h].astype(BF16), preferred_element_type=F32)
                vb = (u_ref[r0:r0 + CHUNK, sl] - res1[:CHUNK]).astype(BF16)
                rhs = jnp.concatenate([vb, zeros_half] if c2 == 0 else [zeros_half, vb], axis=0)
                res2 = jnp.dot(akt_ref[base:base + 2 * SUPER, sl], rhs, preferred_element_type=F32)
                oacc_ref[r0:r0 + CHUNK, sl] = res1[CHUNK:] + res2[c2 * CHUNK:(c2 + 1) * CHUNK]
                states[h] = states[h] * chunk_decay[m, c2, h] + res2[SUPER:]
    for h in range(nh):
        state_ref[h] = states[h]

    for h in range(nh):
        sl = slice(h * DN_HEAD_DIM, (h + 1) * DN_HEAD_DIM)
        o = oacc_ref[:, sl]
        ms = jnp.mean(o * o, axis=-1, keepdims=True)
        o_ref[:, sl] = (o * lax.rsqrt(ms + GATED_NORM_EPS) * ng_ref[...] * _silu(gz_ref[:, sl])).astype(o_ref.dtype)


def _gdn(proj, conv_w, bdc, bdr, alc, dtc, alr, dtr, norm_g, layer, *, batch, seq, nh, rows_blk):
    n = proj.shape[0]
    width = nh * DN_HEAD_DIM
    nblk = seq // rows_blk
    rows_r = bdr.shape[0]

    def col(off):
        return pl.BlockSpec((rows_blk, width), lambda b, t: (b * nblk + t, off))

    lane_vec = pl.BlockSpec((None, 1, LANES), lambda b, t: (layer, 0, 0))
    row_vec = pl.BlockSpec((None, rows_r, 1), lambda b, t: (layer, 0, 0))
    return pl.pallas_call(
        functools.partial(_gdn_kernel, nh=nh, group=GDN_CHAIN_GROUP),
        grid=(batch, nblk),
        in_specs=[
            col(0), col(1), col(2), col(3),
            pl.BlockSpec((None, CONV_K, 3 * width), lambda b, t: (layer, 0, 0)),
            pl.BlockSpec((rows_blk, LANES), lambda b, t: (b * nblk + t, 0)),
            pl.BlockSpec((rows_r, rows_blk), lambda b, t: (0, b * nblk + t)),
            lane_vec, lane_vec, row_vec, row_vec,
            pl.BlockSpec((None, 1, DN_HEAD_DIM), lambda b, t: (layer, 0, 0)),
        ],
        out_specs=pl.BlockSpec((rows_blk, width), lambda b, t: (b * nblk + t, 0)),
        out_shape=jax.ShapeDtypeStruct((n, width), BF16),
        scratch_shapes=[
            pltpu.VMEM((3, SUBLANES, width), F32),
            pltpu.VMEM((nh, DN_HEAD_DIM, DN_HEAD_DIM), F32),
            pltpu.VMEM((rows_blk, width), F32),
            pltpu.VMEM((rows_blk, width), F32),
            pltpu.VMEM((rows_blk, width), F32),
            pltpu.VMEM((rows_blk, width), F32),
            pltpu.VMEM((2 * rows_blk, width), BF16),
            pltpu.VMEM((2 * rows_blk, width), BF16),
            pltpu.VMEM((rows_blk, width), F32),
        ],
        compiler_params=_params(("parallel", "arbitrary")),
        name="gdn",
    )(proj, proj, proj, proj, conv_w, bdc, bdr, alc, dtc, alr, dtr, norm_g)


def _outproj_ln_kernel(x_ref, od_ref, on_ref, w1_ref, w2_ref, g_ref, b_ref, y_ref, *, alpha):
    mix = (jnp.dot(od_ref[...], w1_ref[...], preferred_element_type=F32)
           + jnp.dot(on_ref[...], w2_ref[...], preferred_element_type=F32))
    y_ref[...] = _layer_norm(alpha * x_ref[...] + mix, g_ref[...], b_ref[...])


def _outproj_ln(x, o_diff, o_dn, w_out, g, b, layer, *, alpha, tm):
    n, d = x.shape
    k1, k2 = o_diff.shape[1], o_dn.shape[1]
    assert k1 == k2
    vec = pl.BlockSpec((None, 1, d), lambda i: (layer, 0, 0))
    return pl.pallas_call(
        functools.partial(_outproj_ln_kernel, alpha=alpha),
        grid=(n // tm,),
        in_specs=[
            pl.BlockSpec((tm, d), lambda i: (i, 0)),
            pl.BlockSpec((tm, k1), lambda i: (i, 0)),
            pl.BlockSpec((tm, k2), lambda i: (i, 0)),
            pl.BlockSpec((None, k1, d), lambda i: (layer, 0, 0)),
            pl.BlockSpec((None, k2, d), lambda i: (layer, 1, 0)),
            vec, vec,
        ],
        out_specs=pl.BlockSpec((tm, d), lambda i: (i, 0)),
        out_shape=jax.ShapeDtypeStruct((n, d), F32),
        compiler_params=_params(("parallel",)),
        name="outproj_ln",
    )(x, o_diff, o_dn, w_out, w_out, g, b)


def _tile(n, pref):
    t = min(n, pref)
    assert n % t == 0, (n, pref)
    return t


def kernel(x, positions, ffn1_w_in, ffn1_w_out, ln1_g, ln1_b, w_in, conv_w, a_log, dt_bias, lam_q1, lam_k1, lam_q2, lam_k2, diff_norm_g, delta_norm_g, w_out, ln2_g, ln2_b, ffn2_w_in, ffn2_w_out, ln3_g, ln3_b):
    batch, seq, d = x.shape
    depth = ffn1_w_in.shape[0]
    d_ff = ffn1_w_out.shape[1]
    n = batch * seq
    diff_width = d // 2
    dn_width = d - diff_width
    n_diff_heads = diff_width // (2 * DIFF_HEAD_DIM)
    nh = dn_width // DN_HEAD_DIM
    qk_cols = 2 * (2 * n_diff_heads * DIFF_HEAD_DIM)
    v_cols = n_diff_heads * 2 * DIFF_HEAD_DIM
    main_cols = qk_cols + v_cols + 4 * dn_width
    assert w_in.shape[2] == main_cols + 2 * nh and 2 * nh <= LANES
    assert seq % SUPER == 0
    alpha = (2 * depth) ** 0.25

    tm = _tile(n, 512)
    tm_proj = _tile(n, 1024)
    tf = _tile(d_ff, 512)
    tn = _tile(v_cols, 512)
    tq = _tile(seq, 256)
    gdn_rows = _tile(seq, 512)

    ffn1_in, ffn1_out = ffn1_w_in.astype(BF16), ffn1_w_out.astype(BF16)
    ffn2_in, ffn2_out = ffn2_w_in.astype(BF16), ffn2_w_out.astype(BF16)
    w_main = w_in[:, :, :main_cols].astype(BF16)
    w_gate = w_in[:, :, main_cols:].astype(BF16)
    w_bd = jnp.pad(w_gate, ((0, 0), (0, 0), (0, LANES - 2 * nh)))
    rows_r = -(-2 * nh // SUBLANES) * SUBLANES
    w_out_b = w_out.astype(BF16)
    pad_lane = lambda v: jnp.pad(v.astype(F32), ((0, 0), (nh, LANES - 2 * nh))).reshape(depth, 1, LANES)
    pad_row = lambda v: jnp.pad(v.astype(F32), ((0, 0), (nh, rows_r - 2 * nh))).reshape(depth, rows_r, 1)
    alc, dtc, alr, dtr = pad_lane(a_log), pad_lane(dt_bias), pad_row(a_log), pad_row(dt_bias)
    lam_vecs = jnp.stack([lam_q1, lam_k1, lam_q2, lam_k2], axis=1).astype(F32)
    vec3 = lambda v: v.reshape(depth, 1, v.shape[-1])
    ln1_g, ln1_b, ln2_g, ln2_b, ln3_g, ln3_b = map(vec3, (ln1_g, ln1_b, ln2_g, ln2_b, ln3_g, ln3_b))
    diff_g, delta_g = vec3(diff_norm_g), vec3(delta_norm_g)

    rope_c, rope_sa, rope_sb = _rope_tables(positions, tm)

    h = x.reshape(n, d)
    for l in range(depth):
        lam_init = 0.8 - 0.6 * math.exp(-0.3 * l)
        h, hb16 = _ffn_ln(h, ffn1_in, ffn1_out, ln1_g, ln1_b, l, alpha=alpha, tm=tm, tf=tf, emit_bf16=True)
        qk, v, dn, bdc, bdr = _proj(hb16, w_main, w_bd, (rope_c, rope_sa, rope_sb), l, rows=rows_r,
                                    qk_cols=qk_cols, v_cols=v_cols, dn_cols=4 * dn_width, tm=tm_proj, tn=tn)
        o_diff = _attn(qk, v, lam_vecs, diff_g, l, batch=batch, seq=seq, n_heads=n_diff_heads, tq=tq,
                       lam_init=lam_init)
        o_dn = _gdn(dn, conv_w, bdc, bdr, alc, dtc, alr, dtr, delta_g, l, batch=batch, seq=seq, nh=nh,
                    rows_blk=gdn_rows)
        h = _outproj_ln(h, o_diff, o_dn, w_out_b, ln2_g, ln2_b, l, alpha=alpha, tm=tm)
        (h,) = _ffn_ln(h, ffn2_in, ffn2_out, ln3_g, ln3_b, l, alpha=alpha, tm=tm, tf=tf, emit_bf16=False)
    return h.reshape(batch, seq, d)
```

```python
import functools
import math

import jax
import jax.numpy as jnp
from jax import lax
from jax.experimental import pallas as pl
from jax.experimental.pallas import tpu as pltpu

F32 = jnp.float32
BF16 = jnp.bfloat16

LANES = 128
SUBLANES = 8
DIFF_HEAD_DIM = 64
DN_HEAD_DIM = 128
CONV_K = 4
CHUNK = 64
SUPER = 2 * CHUNK
GDN_CHAIN_GROUP = 16
ROPE_THETA = 500000.0
ROPE_DIM = DIFF_HEAD_DIM // 4
ROPE_HALF = ROPE_DIM // 2
LN_EPS = 1e-5
SUBLN_EPS = 1e-5
GATED_NORM_EPS = 1e-6
L2_EPS = 1e-6
VMEM_LIMIT_BYTES = 56 * 1024 * 1024

_NT = (((1,), (1,)), ((), ()))


def _params(sem):
    return pltpu.CompilerParams(dimension_semantics=sem, vmem_limit_bytes=VMEM_LIMIT_BYTES)


def _layer_norm(y, g, b):
    mu = jnp.mean(y, axis=-1, keepdims=True)
    yc = y - mu
    var = jnp.mean(yc * yc, axis=-1, keepdims=True)
    return yc * lax.rsqrt(var + LN_EPS) * g + b


def _silu(x):
    return x * jax.nn.sigmoid(x)


def _rope_table_kernel(pos_ref, inv_ref, c_ref, sa_ref, sb_ref):
    ang = pos_ref[...].astype(F32) * inv_ref[...]
    lane = lax.broadcasted_iota(jnp.int32, ang.shape, 1) % DIFF_HEAD_DIM
    cos, sin = jnp.cos(ang), jnp.sin(ang)
    c_ref[...] = jnp.where(lane < ROPE_DIM, cos, 1.0)
    sa_ref[...] = jnp.where(lane < ROPE_HALF, -sin, 0.0)
    sb_ref[...] = jnp.where((lane >= ROPE_HALF) & (lane < ROPE_DIM), sin, 0.0)


def _rope_tables(positions, tm):
    n = positions.size
    pos = positions.reshape(n, 1)
    inv_freq = ROPE_THETA ** (-jnp.arange(0, ROPE_DIM, 2, dtype=F32) / ROPE_DIM)
    lane = jnp.arange(LANES) % DIFF_HEAD_DIM
    inv = jnp.where(lane < ROPE_DIM, inv_freq[lane % ROPE_HALF], 0.0).reshape(1, LANES).astype(F32)
    out = jax.ShapeDtypeStruct((n, LANES), F32)
    spec = pl.BlockSpec((tm, LANES), lambda i: (i, 0))
    return pl.pallas_call(
        _rope_table_kernel,
        grid=(n // tm,),
        in_specs=[pl.BlockSpec((tm, 1), lambda i: (i, 0)), pl.BlockSpec((1, LANES), lambda i: (0, 0))],
        out_specs=[spec, spec, spec],
        out_shape=[out, out, out],
        compiler_params=_params(("parallel",)),
        name="rope_tables",
    )(pos, inv)


def _ffn_ln_kernel(x_ref, wg_ref, wu_ref, wo_ref, g_ref, b_ref, *rest, alpha, emit_bf16, n_row, ln_rows):
    if emit_bf16:
        y_ref, yb_ref, xb_ref, acc_ref = rest
    else:
        y_ref, xb_ref, acc_ref = rest
    i, j = pl.program_id(0), pl.program_id(1)
    tm = xb_ref.shape[0]
    slot = i % 2

    @pl.when((i == 0) & (j == 0))
    def _first():
        acc_ref[1] = jnp.zeros(acc_ref.shape[1:], F32)

    def ln_slice():
        r0 = pl.multiple_of(jnp.minimum(j * ln_rows, tm - ln_rows), SUBLANES)
        y = _layer_norm(acc_ref[1 - slot, pl.ds(r0, ln_rows), :], g_ref[...], b_ref[...])
        y_ref[pl.ds(r0, ln_rows), :] = y
        if emit_bf16:
            yb_ref[pl.ds(r0, ln_rows), :] = y.astype(BF16)

    @pl.when((i < n_row) & (j == 0))
    def _init():
        x = x_ref[...]
        xb_ref[...] = x.astype(BF16)
        acc_ref[slot] = alpha * x

    @pl.when(i < n_row)
    def _main():
        ln_slice()
        xb = xb_ref[...]
        hg = jnp.dot(xb, wg_ref[...], preferred_element_type=F32)
        hu = jnp.dot(xb, wu_ref[...], preferred_element_type=F32)
        a = (_silu(hg) * hu).astype(BF16)
        acc_ref[slot] += jnp.dot(a, wo_ref[...], preferred_element_type=F32)

    @pl.when(i == n_row)
    def _tail():
        ln_slice()


def _ffn_ln(x, w_in, w_out_half, g, b, layer, *, alpha, tm, tf, emit_bf16):
    n, d = x.shape
    f = w_out_half.shape[1]
    nf = f // tf
    n_row = n // tm
    ln_rows = -(-tm // (nf * SUBLANES)) * SUBLANES
    assert ln_rows <= tm
    last = n_row - 1
    col = lambda i, j: jnp.where(i < n_row, j, nf - 1)
    prev = pl.BlockSpec((tm, d), lambda i, j: (jnp.maximum(i - 1, 0), 0))
    vec = pl.BlockSpec((None, 1, d), lambda i, j: (layer, 0, 0))
    out_shape = [jax.ShapeDtypeStruct((n, d), F32)]
    out_specs = [prev]
    if emit_bf16:
        out_shape.append(jax.ShapeDtypeStruct((n, d), BF16))
        out_specs.append(prev)
    return pl.pallas_call(
        functools.partial(_ffn_ln_kernel, alpha=alpha, emit_bf16=emit_bf16, n_row=n_row, ln_rows=ln_rows),
        grid=(n_row + 1, nf),
        in_specs=[
            pl.BlockSpec((tm, d), lambda i, j: (jnp.minimum(i, last), 0)),
            pl.BlockSpec((None, d, tf), lambda i, j: (layer, 0, col(i, j))),
            pl.BlockSpec((None, d, tf), lambda i, j: (layer, 0, col(i, j) + nf)),
            pl.BlockSpec((None, tf, d), lambda i, j: (layer, col(i, j), 0)),
            vec, vec,
        ],
        out_specs=out_specs,
        out_shape=out_shape,
        scratch_shapes=[pltpu.VMEM((tm, d), BF16), pltpu.VMEM((2, tm, d), F32)],
        compiler_params=_params(("arbitrary", "arbitrary")),
        name="ffn_ln",
    )(x, w_in, w_in, w_out_half, g, b)


def _proj_kernel(xb_ref, w_ref, wbd_ref, c_ref, sa_ref, sb_ref,
                 qk_ref, v_ref, dn_ref, bdc_ref, bdr_ref, *, n_q, n_qk, n_v):
    j = pl.program_id(1)
    xb = xb_ref[...]
    acc = jnp.dot(xb, w_ref[...], preferred_element_type=F32)

    @pl.when(j == 0)
    def _gates():
        gates = jnp.dot(xb, wbd_ref[...], preferred_element_type=F32)
        bdc_ref[...] = gates
        bdr_ref[...] = gates.T[0:bdr_ref.shape[0], :]

    @pl.when(j < n_qk)
    def _qk():
        scale = jnp.where(j < n_q, DIFF_HEAD_DIM ** -0.5, 1.0)
        c, sa, sb = c_ref[...] * scale, sa_ref[...] * scale, sb_ref[...] * scale
        for t in range(acc.shape[1] // LANES):
            seg = acc[:, t * LANES:(t + 1) * LANES]
            rot = seg * c + pltpu.roll(seg, LANES - ROPE_HALF, 1) * sa + pltpu.roll(seg, ROPE_HALF, 1) * sb
            qk_ref[:, t * LANES:(t + 1) * LANES] = rot.astype(qk_ref.dtype)

    @pl.when((j >= n_qk) & (j < n_qk + n_v))
    def _v():
        v_ref[...] = acc.astype(v_ref.dtype)

    @pl.when(j >= n_qk + n_v)
    def _dn():
        dn_ref[...] = acc


def _proj(xb, w_main, w_bd, rope, layer, *, rows, qk_cols, v_cols, dn_cols, tm, tn):
    n, d = xb.shape
    n_qk, n_v, n_dn = qk_cols // tn, v_cols // tn, dn_cols // tn
    c, sa, sb = rope
    tspec = pl.BlockSpec((tm, LANES), lambda i, j: (i, 0))
    return pl.pallas_call(
        functools.partial(_proj_kernel, n_q=n_qk // 2, n_qk=n_qk, n_v=n_v),
        grid=(n // tm, n_qk + n_v + n_dn),
        in_specs=[
            pl.BlockSpec((tm, d), lambda i, j: (i, 0)),
            pl.BlockSpec((None, d, tn), lambda i, j: (layer, 0, j)),
            pl.BlockSpec((None, d, LANES), lambda i, j: (layer, 0, 0)),
            tspec, tspec, tspec,
        ],
        out_specs=[
            pl.BlockSpec((tm, tn), lambda i, j: (i, jnp.minimum(j, n_qk - 1))),
            pl.BlockSpec((tm, tn), lambda i, j: (i, jnp.clip(j - n_qk, 0, n_v - 1))),
            pl.BlockSpec((tm, tn), lambda i, j: (i, jnp.clip(j - n_qk - n_v, 0, n_dn - 1))),
            pl.BlockSpec((tm, LANES), lambda i, j: (i, 0)),
            pl.BlockSpec((rows, tm), lambda i, j: (0, i)),
        ],
        out_shape=[
            jax.ShapeDtypeStruct((n, qk_cols), BF16),
            jax.ShapeDtypeStruct((n, v_cols), BF16),
            jax.ShapeDtypeStruct((n, dn_cols), F32),
            jax.ShapeDtypeStruct((n, LANES), F32),
            jax.ShapeDtypeStruct((rows, n), F32),
        ],
        compiler_params=_params(("parallel", "arbitrary")),
        name="proj",
    )(xb, w_main, w_bd, c, sa, sb)


def _attn_kernel(lam_ref, q_ref, k_ref, v_ref, g_ref, o_ref, *, tq, lam_init):
    s_len = q_ref.shape[0]
    lq = lam_ref[...]
    lam = (jnp.exp(jnp.sum(lq[0:1] * lq[1:2], axis=-1, keepdims=True))
           - jnp.exp(jnp.sum(lq[2:3] * lq[3:4], axis=-1, keepdims=True)) + lam_init)
    lane = lax.broadcasted_iota(jnp.int32, (1, LANES), 1)
    map0 = lane < DIFF_HEAD_DIM
    gain = g_ref[...] * (1.0 - lam_init)
    n_blk = s_len // tq

    def scores(i):
        skv = (i + 1) * tq
        q = q_ref[i * tq:(i + 1) * tq, :]
        zero = jnp.zeros_like(q)
        k = k_ref[0:skv, :]
        s0 = lax.dot_general(jnp.where(map0, q, zero), k, _NT, preferred_element_type=F32)
        s1 = lax.dot_general(jnp.where(map0, zero, q), k, _NT, preferred_element_type=F32)
        row = lax.broadcasted_iota(jnp.int32, (tq, skv), 0) + i * tq
        col = lax.broadcasted_iota(jnp.int32, (tq, skv), 1)
        keep = col <= row
        return jnp.where(keep, s0, -jnp.inf), jnp.where(keep, s1, -jnp.inf)

    def probs(s0, s1):
        e0 = jnp.exp(s0 - jnp.max(s0, axis=-1, keepdims=True))
        e1 = jnp.exp(s1 - jnp.max(s1, axis=-1, keepdims=True))
        r0 = 1.0 / jnp.sum(e0, axis=-1, keepdims=True)
        r1 = lam / jnp.sum(e1, axis=-1, keepdims=True)
        return (e0 * r0 - e1 * r1).astype(BF16)

    def emit(i, p):
        o = jnp.dot(p, v_ref[0:(i + 1) * tq, :], preferred_element_type=F32)
        ms = jnp.mean(o * o, axis=-1, keepdims=True)
        o_ref[i * tq:(i + 1) * tq, :] = (o * lax.rsqrt(ms + SUBLN_EPS) * gain).astype(o_ref.dtype)

    s_next = scores(0)
    for i in range(n_blk):
        s_cur = s_next
        if i + 1 < n_blk:
            s_next = scores(i + 1)
        emit(i, probs(*s_cur))


def _attn(qk, v, lam_vecs, norm_g, layer, *, batch, seq, n_heads, tq, lam_init):
    n = qk.shape[0]
    hd = 2 * DIFF_HEAD_DIM
    return pl.pallas_call(
        functools.partial(_attn_kernel, tq=tq, lam_init=lam_init),
        grid=(batch, n_heads),
        in_specs=[
            pl.BlockSpec((None, 4, DIFF_HEAD_DIM), lambda b, h: (layer, 0, 0)),
            pl.BlockSpec((seq, hd), lambda b, h: (b, h)),
            pl.BlockSpec((seq, hd), lambda b, h: (b, n_heads + h)),
            pl.BlockSpec((seq, hd), lambda b, h: (b, h)),
            pl.BlockSpec((None, 1, hd), lambda b, h: (layer, 0, 0)),
        ],
        out_specs=pl.BlockSpec((seq, hd), lambda b, h: (b, h)),
        out_shape=jax.ShapeDtypeStruct((n, n_heads * hd), BF16),
        compiler_params=_params(("parallel", "parallel")),
        name="diff_attn",
    )(lam_vecs, qk, qk, v, norm_g)


def _chunk_scan(x, pos, axis, reverse=False):
    size = x.shape[axis]
    s = 1
    while s < CHUNK:
        if reverse:
            x = x + jnp.where(pos < CHUNK - s, pltpu.roll(x, size - s, axis), 0.0)
        else:
            x = x + jnp.where(pos >= s, pltpu.roll(x, s, axis), 0.0)
        s *= 2
    return x


def _gdn_kernel(gq_ref, gk_ref, gv_ref, gz_ref, cw_ref, bdc_ref, bdr_ref,
                alc_ref, dtc_ref, alr_ref, dtr_ref, ng_ref, o_ref,
                hist_ref, state_ref, qn_ref, kn_ref, vc_ref, u_ref, wq_ref, akt_ref, oacc_ref, *, nh, group):
    rows_blk, width = gq_ref.shape
    n_super = rows_blk // SUPER

    @pl.when(pl.program_id(1) == 0)
    def _reset():
        hist_ref[...] = jnp.zeros_like(hist_ref)
        state_ref[...] = jnp.zeros_like(state_ref)

    rows8 = lax.broadcasted_iota(jnp.int32, (SUBLANES, 1), 0)

    def conv_silu(u_ref_, idx, dst_ref):
        w = cw_ref[:, idx * width:(idx + 1) * width]
        hist = hist_ref[idx]
        head = u_ref_[0:SUBLANES, :]
        acc = u_ref_[SUBLANES:rows_blk, :] * w[CONV_K - 1:CONV_K]
        top = head * w[CONV_K - 1:CONV_K]
        for j in range(CONV_K - 1):
            sh = CONV_K - 1 - j
            acc = acc + u_ref_[SUBLANES - sh:rows_blk - sh, :] * w[j:j + 1]
            top = top + jnp.where(rows8 < sh, pltpu.roll(hist, sh, 0), pltpu.roll(head, sh, 0)) * w[j:j + 1]
        hist_ref[idx] = u_ref_[rows_blk - SUBLANES:rows_blk, :]
        dst_ref[0:SUBLANES, :] = _silu(top)
        dst_ref[SUBLANES:rows_blk, :] = _silu(acc)

    conv_silu(gq_ref, 0, qn_ref)
    conv_silu(gk_ref, 1, kn_ref)
    conv_silu(gv_ref, 2, vc_ref)
    for h in range(nh):
        sl = slice(h * DN_HEAD_DIM, (h + 1) * DN_HEAD_DIM)
        qh, kh = qn_ref[:, sl], kn_ref[:, sl]
        qn_ref[:, sl] = qh * (lax.rsqrt(jnp.sum(qh * qh, axis=-1, keepdims=True) + L2_EPS) * DN_HEAD_DIM ** -0.5)
        kn_ref[:, sl] = kh * lax.rsqrt(jnp.sum(kh * kh, axis=-1, keepdims=True) + L2_EPS)

    rows = lax.broadcasted_iota(jnp.int32, (rows_blk, 1), 0)
    xg = bdc_ref[...]
    beta_all = jax.nn.sigmoid(xg)
    g_raw = -jnp.exp(alc_ref[...]) * jax.nn.softplus(xg + dtc_ref[...])
    g_cum = _chunk_scan(g_raw, rows % CHUNK, 0)
    eg_all = jnp.exp(g_cum)
    ekd_all = jnp.exp(_chunk_scan(g_raw, rows % CHUNK, 0, reverse=True) - g_raw)
    cols = lax.broadcasted_iota(jnp.int32, (1, rows_blk), 1)
    gr_raw = -jnp.exp(alr_ref[...]) * jax.nn.softplus(bdr_ref[...] + dtr_ref[...])
    g_row_all = _chunk_scan(gr_raw, cols % CHUNK, 1)

    ii = lax.broadcasted_iota(jnp.int32, (SUPER, SUPER), 0)
    jj = lax.broadcasted_iota(jnp.int32, (SUPER, SUPER), 1)
    same = (ii // CHUNK) == (jj // CHUNK)
    incl = same & (ii >= jj)
    strict = same & (ii > jj)
    eye = (ii == jj).astype(F32)

    def lower_left_mask(s):
        return ((ii // (2 * s)) == (jj // (2 * s))) & ((ii // s) % 2 == 1) & ((jj // s) % 2 == 0)

    lower_left = {}
    s = 1
    while s < CHUNK:
        lower_left[s] = lower_left_mask(s)
        s *= 2

    chunk_decay = {}
    chains = [(m, h) for m in range(n_super) for h in range(nh)]
    for g0 in range(0, len(chains), group):
        grp = chains[g0:g0 + group]
        lmats, ts, rhss = [], [], []
        for m, h in grp:
            rs = slice(m * SUPER, (m + 1) * SUPER)
            sl = slice(h * DN_HEAD_DIM, (h + 1) * DN_HEAD_DIM)
            gl = nh + h
            qn, kn, v = qn_ref[rs, sl], kn_ref[rs, sl], vc_ref[rs, sl]
            beta, eg, ekd = beta_all[rs, h:h + 1], eg_all[rs, gl:gl + 1], ekd_all[rs, gl:gl + 1]
            decay = jnp.where(
                incl, jnp.exp(jnp.where(incl, g_cum[rs, gl:gl + 1] - g_row_all[gl:gl + 1, rs], 0.0)), 0.0)
            kb = kn * beta
            knb = kn.astype(BF16)
            kk = lax.dot_general(kb.astype(BF16), knb, _NT, preferred_element_type=F32)
            qk = lax.dot_general(qn.astype(BF16), knb, _NT, preferred_element_type=F32)
            lmat = jnp.where(strict, kk * decay, 0.0)
            lmats.append(lmat)
            ts.append(eye - jnp.where(lower_left[1], lmat, 0.0))
            rhss.append(jnp.concatenate([v * beta, kb * eg], axis=1).astype(BF16))
            qd = (qn * eg).astype(BF16)
            base = 2 * SUPER * m
            for c2 in range(2):
                wq_ref[base + c2 * SUPER + CHUNK:base + (c2 + 1) * SUPER, sl] = qd[c2 * CHUNK:(c2 + 1) * CHUNK]
                last = (c2 + 1) * CHUNK - 1
                chunk_decay[m, c2, h] = eg[last:last + 1]
            akt_ref[base:base + SUPER, sl] = (qk * decay).astype(BF16)
            akt_ref[base + SUPER:base + 2 * SUPER, sl] = (kn * ekd).T.astype(BF16)
        s = 2
        while s < CHUNK:
            ys = [jnp.dot(jnp.where(lower_left[s], lmat, 0.0).astype(BF16), t.astype(BF16),
                          preferred_element_type=F32) for lmat, t in zip(lmats, ts)]
            ts = [t - jnp.dot(t.astype(BF16), y.astype(BF16), preferred_element_type=F32)
                  for t, y in zip(ts, ys)]
            s *= 2
        for (m, h), t, rhs in zip(grp, ts, rhss):
            rs = slice(m * SUPER, (m + 1) * SUPER)
            sl = slice(h * DN_HEAD_DIM, (h + 1) * DN_HEAD_DIM)
            sol = jnp.dot(t.astype(BF16), rhs, preferred_element_type=F32)
            u_ref[rs, sl] = sol[:, :DN_HEAD_DIM]
            w = sol[:, DN_HEAD_DIM:].astype(BF16)
            base = 2 * SUPER * m
            for c2 in range(2):
                wq_ref[base + c2 * SUPER:base + c2 * SUPER + CHUNK, sl] = w[c2 * CHUNK:(c2 + 1) * CHUNK]

    zeros_half = jnp.zeros((CHUNK, DN_HEAD_DIM), BF16)
    states = [state_ref[h] for h in range(nh)]
    for m in range(n_super):
        base = 2 * SUPER * m
        for c2 in range(2):
            r0 = m * SUPER + c2 * CHUNK
            for h in range(nh):
                sl = slice(h * DN_HEAD_DIM, (h + 1) * DN_HEAD_DIM)
                wq = wq_ref[base + c2 * SUPER:base + (c2 + 1) * SUPER, sl]
                res1 = jnp.dot(wq, states[h].astype(BF16), preferred_element_type=F32)
                vb = (u_ref[r0:r0 + CHUNK, sl] - res1[:CHUNK]).astype(BF16)
                rhs = jnp.concatenate([vb, zeros_half] if c2 == 0 else [zeros_half, vb], axis=0)
                res2 = jnp.dot(akt_ref[base:base + 2 * SUPER, sl], rhs, preferred_element_type=F32)
                oacc_ref[r0:r0 + CHUNK, sl] = res1[CHUNK:] + res2[c2 * CHUNK:(c2 + 1) * CHUNK]
                states[h] = states[h] * chunk_decay[m, c2, h] + res2[SUPER:]
    for h in range(nh):
        state_ref[h] = states[h]

    for h in range(nh):
        sl = slice(h * DN_HEAD_DIM, (h + 1) * DN_HEAD_DIM)
        o = oacc_ref[:, sl]
        ms = jnp.mean(o * o, axis=-1, keepdims=True)
        o_ref[:, sl] = (o * lax.rsqrt(ms + GATED_NORM_EPS) * ng_ref[...] * _silu(gz_ref[:, sl])).astype(o_ref.dtype)


def _gdn(proj, conv_w, bdc, bdr, alc, dtc, alr, dtr, norm_g, layer, *, batch, seq, nh, rows_blk):
    n = proj.shape[0]
    width = nh * DN_HEAD_DIM
    nblk = seq // rows_blk
    rows_r = bdr.shape[0]

    def col(off):
        return pl.BlockSpec((rows_blk, width), lambda b, t: (b * nblk + t, off))

    lane_vec = pl.BlockSpec((None, 1, LANES), lambda b, t: (layer, 0, 0))
    row_vec = pl.BlockSpec((None, rows_r, 1), lambda b, t: (layer, 0, 0))
    return pl.pallas_call(
        functools.partial(_gdn_kernel, nh=nh, group=GDN_CHAIN_GROUP),
        grid=(batch, nblk),
        in_specs=[
            col(0), col(1), col(2), col(3),
            pl.BlockSpec((None, CONV_K, 3 * width), lambda b, t: (layer, 0, 0)),
            pl.BlockSpec((rows_blk, LANES), lambda b, t: (b * nblk + t, 0)),
            pl.BlockSpec((rows_r, rows_blk), lambda b, t: (0, b * nblk + t)),
            lane_vec, lane_vec, row_vec, row_vec,
            pl.BlockSpec((None, 1, DN_HEAD_DIM), lambda b, t: (layer, 0, 0)),
        ],
        out_specs=pl.BlockSpec((rows_blk, width), lambda b, t: (b * nblk + t, 0)),
        out_shape=jax.ShapeDtypeStruct((n, width), BF16),
        scratch_shapes=[
            pltpu.VMEM((3, SUBLANES, width), F32),
            pltpu.VMEM((nh, DN_HEAD_DIM, DN_HEAD_DIM), F32),
            pltpu.VMEM((rows_blk, width), F32),
            pltpu.VMEM((rows_blk, width), F32),
            pltpu.VMEM((rows_blk, width), F32),
            pltpu.VMEM((rows_blk, width), F32),
            pltpu.VMEM((2 * rows_blk, width), BF16),
            pltpu.VMEM((2 * rows_blk, width), BF16),
            pltpu.VMEM((rows_blk, width), F32),
        ],
        compiler_params=_params(("parallel", "arbitrary")),
        name="gdn",
    )(proj, proj, proj, proj, conv_w, bdc, bdr, alc, dtc, alr, dtr, norm_g)


def _outproj_ln_kernel(x_ref, od_ref, on_ref, w1_ref, w2_ref, g_ref, b_ref, y_ref, *, alpha):
    mix = (jnp.dot(od_ref[...], w1_ref[...], preferred_element_type=F32)
           + jnp.dot(on_ref[...], w2_ref[...], preferred_element_type=F32))
    y_ref[...] = _layer_norm(alpha * x_ref[...] + mix, g_ref[...], b_ref[...])


def _outproj_ln(x, o_diff, o_dn, w_out, g, b, layer, *, alpha, tm):
    n, d = x.shape
    k1, k2 = o_diff.shape[1], o_dn.shape[1]
    assert k1 == k2
    vec = pl.BlockSpec((None, 1, d), lambda i: (layer, 0, 0))
    return pl.pallas_call(
        functools.partial(_outproj_ln_kernel, alpha=alpha),
        grid=(n // tm,),
        in_specs=[
            pl.BlockSpec((tm, d), lambda i: (i, 0)),
            pl.BlockSpec((tm, k1), lambda i: (i, 0)),
            pl.BlockSpec((tm, k2), lambda i: (i, 0)),
            pl.BlockSpec((None, k1, d), lambda i: (layer, 0, 0)),
            pl.BlockSpec((None, k2, d), lambda i: (layer, 1, 0)),
            vec, vec,
        ],
        out_specs=pl.BlockSpec((tm, d), lambda i: (i, 0)),
        out_shape=jax.ShapeDtypeStruct((n, d), F32),
        compiler_params=_params(("parallel",)),
        name="outproj_ln",
    )(x, o_diff, o_dn, w_out, w_out, g, b)


def _tile(n, pref):
    t = min(n, pref)
    assert n % t == 0, (n, pref)
    return t


def kernel(x, positions, ffn1_w_in, ffn1_w_out, ln1_g, ln1_b, w_in, conv_w, a_log, dt_bias, lam_q1, lam_k1, lam_q2, lam_k2, diff_norm_g, delta_norm_g, w_out, ln2_g, ln2_b, ffn2_w_in, ffn2_w_out, ln3_g, ln3_b):
    batch, seq, d = x.shape
    depth = ffn1_w_in.shape[0]
    d_ff = ffn1_w_out.shape[1]
    n = batch * seq
    diff_width = d // 2
    dn_width = d - diff_width
    n_diff_heads = diff_width // (2 * DIFF_HEAD_DIM)
    nh = dn_width // DN_HEAD_DIM
    qk_cols = 2 * (2 * n_diff_heads * DIFF_HEAD_DIM)
    v_cols = n_diff_heads * 2 * DIFF_HEAD_DIM
    main_cols = qk_cols + v_cols + 4 * dn_width
    assert w_in.shape[2] == main_cols + 2 * nh and 2 * nh <= LANES
    assert seq % SUPER == 0
    alpha = (2 * depth) ** 0.25

    tm = _tile(n, 512)
    tm_proj = _tile(n, 1024)
    tf = _tile(d_ff, 512)
    tn = _tile(v_cols, 512)
    tq = _tile(seq, 256)
    gdn_rows = _tile(seq, 512)

    ffn1_in, ffn1_out = ffn1_w_in.astype(BF16), (0.5 * ffn1_w_out).astype(BF16)
    ffn2_in, ffn2_out = ffn2_w_in.astype(BF16), (0.5 * ffn2_w_out).astype(BF16)
    w_main = w_in[:, :, :main_cols].astype(BF16)
    w_gate = w_in[:, :, main_cols:].astype(BF16)
    w_bd = jnp.pad(w_gate, ((0, 0), (0, 0), (0, LANES - 2 * nh)))
    rows_r = -(-2 * nh // SUBLANES) * SUBLANES
    w_out_b = w_out.astype(BF16)
    pad_lane = lambda v: jnp.pad(v.astype(F32), ((0, 0), (nh, LANES - 2 * nh))).reshape(depth, 1, LANES)
    pad_row = lambda v: jnp.pad(v.astype(F32), ((0, 0), (nh, rows_r - 2 * nh))).reshape(depth, rows_r, 1)
    alc, dtc, alr, dtr = pad_lane(a_log), pad_lane(dt_bias), pad_row(a_log), pad_row(dt_bias)
    lam_vecs = jnp.stack([lam_q1, lam_k1, lam_q2, lam_k2], axis=1).astype(F32)
    vec3 = lambda v: v.reshape(depth, 1, v.shape[-1])
    ln1_g, ln1_b, ln2_g, ln2_b, ln3_g, ln3_b = map(vec3, (ln1_g, ln1_b, ln2_g, ln2_b, ln3_g, ln3_b))
    diff_g, delta_g = vec3(diff_norm_g), vec3(delta_norm_g)

    rope_c, rope_sa, rope_sb = _rope_tables(positions, tm)

    h = x.reshape(n, d)
    for l in range(depth):
        lam_init = 0.8 - 0.6 * math.exp(-0.3 * l)
        h, hb16 = _ffn_ln(h, ffn1_in, ffn1_out, ln1_g, ln1_b, l, alpha=alpha, tm=tm, tf=tf, emit_bf16=True)
        qk, v, dn, bdc, bdr = _proj(hb16, w_main, w_bd, (rope_c, rope_sa, rope_sb), l, rows=rows_r,
                                    qk_cols=qk_cols, v_cols=v_cols, dn_cols=4 * dn_width, tm=tm_proj, tn=tn)
        o_diff = _attn(qk, v, lam_vecs, diff_g, l, batch=batch, seq=seq, n_heads=n_diff_heads, tq=tq,
                       lam_init=lam_init)
        o_dn = _gdn(dn, conv_w, bdc, bdr, alc, dtc, alr, dtr, delta_g, l, batch=batch, seq=seq, nh=nh,
                    rows_blk=gdn_rows)
        h = _outproj_ln(h, o_diff, o_dn, w_out_b, ln2_g, ln2_b, l, alpha=alpha, tm=tm)
        (h,) = _ffn_ln(h, ffn2_in, ffn2_out, ln3_g, ln3_b, l, alpha=alpha, tm=tm, tf=tf, emit_bf16=False)
    return h.reshape(batch, seq, d)
```

```python
import functools
import math

import jax
import jax.numpy as jnp
from jax import lax
from jax.experimental import pallas as pl
from jax.experimental.pallas import tpu as pltpu

F32 = jnp.float32
BF16 = jnp.bfloat16

LANES = 128
SUBLANES = 8
DIFF_HEAD_DIM = 64
DN_HEAD_DIM = 128
CONV_K = 4
CHUNK = 64
SUPER = 2 * CHUNK
GDN_CHAIN_GROUP = 16
ROPE_THETA = 500000.0
ROPE_DIM = DIFF_HEAD_DIM // 4
ROPE_HALF = ROPE_DIM // 2
LN_EPS = 1e-5
SUBLN_EPS = 1e-5
GATED_NORM_EPS = 1e-6
L2_EPS = 1e-6
VMEM_LIMIT_BYTES = 56 * 1024 * 1024

_NT = (((1,), (1,)), ((), ()))


def _params(sem):
    return pltpu.CompilerParams(dimension_semantics=sem, vmem_limit_bytes=VMEM_LIMIT_BYTES)


def _layer_norm(y, g, b):
    mu = jnp.mean(y, axis=-1, keepdims=True)
    yc = y - mu
    var = jnp.mean(yc * yc, axis=-1, keepdims=True)
    return yc * lax.rsqrt(var + LN_EPS) * g + b


def _silu(x):
    return x * jax.nn.sigmoid(x)


def _rope_table_kernel(pos_ref, inv_ref, c_ref, sa_ref, sb_ref):
    ang = pos_ref[...].astype(F32) * inv_ref[...]
    lane = lax.broadcasted_iota(jnp.int32, ang.shape, 1) % DIFF_HEAD_DIM
    cos, sin = jnp.cos(ang), jnp.sin(ang)
    c_ref[...] = jnp.where(lane < ROPE_DIM, cos, 1.0)
    sa_ref[...] = jnp.where(lane < ROPE_HALF, -sin, 0.0)
    sb_ref[...] = jnp.where((lane >= ROPE_HALF) & (lane < ROPE_DIM), sin, 0.0)


def _rope_tables(positions, tm):
    n = positions.size
    pos = positions.reshape(n, 1)
    inv_freq = ROPE_THETA ** (-jnp.arange(0, ROPE_DIM, 2, dtype=F32) / ROPE_DIM)
    lane = jnp.arange(LANES) % DIFF_HEAD_DIM
    inv = jnp.where(lane < ROPE_DIM, inv_freq[lane % ROPE_HALF], 0.0).reshape(1, LANES).astype(F32)
    out = jax.ShapeDtypeStruct((n, LANES), F32)
    spec = pl.BlockSpec((tm, LANES), lambda i: (i, 0))
    return pl.pallas_call(
        _rope_table_kernel,
        grid=(n // tm,),
        in_specs=[pl.BlockSpec((tm, 1), lambda i: (i, 0)), pl.BlockSpec((1, LANES), lambda i: (0, 0))],
        out_specs=[spec, spec, spec],
        out_shape=[out, out, out],
        compiler_params=_params(("parallel",)),
        name="rope_tables",
    )(pos, inv)


def _ffn_ln_kernel(x_ref, wg_ref, wu_ref, wo_ref, g_ref, b_ref, *rest, alpha, emit_bf16, n_row, ln_rows):
    if emit_bf16:
        y_ref, yb_ref, xb_ref, acc_ref = rest
    else:
        y_ref, xb_ref, acc_ref = rest
    i, j = pl.program_id(0), pl.program_id(1)
    tm = xb_ref.shape[0]
    slot = i % 2

    @pl.when((i == 0) & (j == 0))
    def _first():
        acc_ref[1] = jnp.zeros(acc_ref.shape[1:], F32)

    def ln_slice():
        r0 = pl.multiple_of(jnp.minimum(j * ln_rows, tm - ln_rows), SUBLANES)
        y = _layer_norm(acc_ref[1 - slot, pl.ds(r0, ln_rows), :], g_ref[...], b_ref[...])
        y_ref[pl.ds(r0, ln_rows), :] = y
        if emit_bf16:
            yb_ref[pl.ds(r0, ln_rows), :] = y.astype(BF16)

    @pl.when((i < n_row) & (j == 0))
    def _init():
        x = x_ref[...]
        xb_ref[...] = x.astype(BF16)
        acc_ref[slot] = alpha * x

    @pl.when(i < n_row)
    def _main():
        ln_slice()
        xb = xb_ref[...]
        hg = jnp.dot(xb, wg_ref[...], preferred_element_type=F32)
        hu = jnp.dot(xb, wu_ref[...], preferred_element_type=F32)
        a = (_silu(hg) * hu).astype(BF16)
        acc_ref[slot] += jnp.dot(a, wo_ref[...], preferred_element_type=F32)

    @pl.when(i == n_row)
    def _tail():
        ln_slice()


def _ffn_ln(x, w_in, w_out_half, g, b, layer, *, alpha, tm, tf, emit_bf16):
    n, d = x.shape
    f = w_out_half.shape[1]
    nf = f // tf
    n_row = n // tm
    ln_rows = -(-tm // (nf * SUBLANES)) * SUBLANES
    assert ln_rows <= tm
    last = n_row - 1
    col = lambda i, j: jnp.where(i < n_row, j, nf - 1)
    prev = pl.BlockSpec((tm, d), lambda i, j: (jnp.maximum(i - 1, 0), 0))
    vec = pl.BlockSpec((None, 1, d), lambda i, j: (layer, 0, 0))
    out_shape = [jax.ShapeDtypeStruct((n, d), F32)]
    out_specs = [prev]
    if emit_bf16:
        out_shape.append(jax.ShapeDtypeStruct((n, d), BF16))
        out_specs.append(prev)
    return pl.pallas_call(
        functools.partial(_ffn_ln_kernel, alpha=alpha, emit_bf16=emit_bf16, n_row=n_row, ln_rows=ln_rows),
        grid=(n_row + 1, nf),
        in_specs=[
            pl.BlockSpec((tm, d), lambda i, j: (jnp.minimum(i, last), 0)),
            pl.BlockSpec((None, d, tf), lambda i, j: (layer, 0, col(i, j))),
            pl.BlockSpec((None, d, tf), lambda i, j: (layer, 0, col(i, j) + nf)),
            pl.BlockSpec((None, tf, d), lambda i, j: (layer, col(i, j), 0)),
            vec, vec,
        ],
        out_specs=out_specs,
        out_shape=out_shape,
        scratch_shapes=[pltpu.VMEM((tm, d), BF16), pltpu.VMEM((2, tm, d), F32)],
        compiler_params=_params(("arbitrary", "arbitrary")),
        name="ffn_ln",
    )(x, w_in, w_in, w_out_half, g, b)


def _proj_kernel(xb_ref, w_ref, wbd_ref, c_ref, sa_ref, sb_ref,
                 qk_ref, v_ref, dn_ref, bdc_ref, bdr_ref, *, n_q, n_qk, n_v):
    j = pl.program_id(1)
    tm, tn = qk_ref.shape
    half = tm // 2

    @pl.when(j == 0)
    def _gates():
        gates = jnp.dot(xb_ref[...], wbd_ref[...], preferred_element_type=F32)
        bdc_ref[...] = gates
        bdr_ref[...] = gates.T[0:bdr_ref.shape[0], :]

    def tile(store):
        for r in range(2):
            rs = slice(r * half, (r + 1) * half)
            store(jnp.dot(xb_ref[rs, :], w_ref[...], preferred_element_type=F32), rs)

    def rope_store(acc, rs):
        scale = jnp.where(j < n_q, DIFF_HEAD_DIM ** -0.5, 1.0)
        c, sa, sb = c_ref[rs, :] * scale, sa_ref[rs, :] * scale, sb_ref[rs, :] * scale
        for t in range(tn // LANES):
            seg = acc[:, t * LANES:(t + 1) * LANES]
            rot = seg * c + pltpu.roll(seg, LANES - ROPE_HALF, 1) * sa + pltpu.roll(seg, ROPE_HALF, 1) * sb
            qk_ref[rs, t * LANES:(t + 1) * LANES] = rot.astype(qk_ref.dtype)

    def v_store(acc, rs):
        v_ref[rs, :] = acc.astype(v_ref.dtype)

    def dn_store(acc, rs):
        dn_ref[rs, :] = acc

    pl.when(j < n_qk)(lambda: tile(rope_store))
    pl.when((j >= n_qk) & (j < n_qk + n_v))(lambda: tile(v_store))
    pl.when(j >= n_qk + n_v)(lambda: tile(dn_store))


def _proj(xb, w_main, w_bd, rope, layer, *, rows, qk_cols, v_cols, dn_cols, tm, tn):
    n, d = xb.shape
    n_qk, n_v, n_dn = qk_cols // tn, v_cols // tn, dn_cols // tn
    c, sa, sb = rope
    tspec = pl.BlockSpec((tm, LANES), lambda i, j: (i, 0))
    return pl.pallas_call(
        functools.partial(_proj_kernel, n_q=n_qk // 2, n_qk=n_qk, n_v=n_v),
        grid=(n // tm, n_qk + n_v + n_dn),
        in_specs=[
            pl.BlockSpec((tm, d), lambda i, j: (i, 0)),
            pl.BlockSpec((None, d, tn), lambda i, j: (layer, 0, j)),
            pl.BlockSpec((None, d, LANES), lambda i, j: (layer, 0, 0)),
            tspec, tspec, tspec,
        ],
        out_specs=[
            pl.BlockSpec((tm, tn), lambda i, j: (i, jnp.minimum(j, n_qk - 1))),
            pl.BlockSpec((tm, tn), lambda i, j: (i, jnp.clip(j - n_qk, 0, n_v - 1))),
            pl.BlockSpec((tm, tn), lambda i, j: (i, jnp.clip(j - n_qk - n_v, 0, n_dn - 1))),
            pl.BlockSpec((tm, LANES), lambda i, j: (i, 0)),
            pl.BlockSpec((rows, tm), lambda i, j: (0, i)),
        ],
        out_shape=[
            jax.ShapeDtypeStruct((n, qk_cols), BF16),
            jax.ShapeDtypeStruct((n, v_cols), BF16),
            jax.ShapeDtypeStruct((n, dn_cols), F32),
            jax.ShapeDtypeStruct((n, LANES), F32),
            jax.ShapeDtypeStruct((rows, n), F32),
        ],
        compiler_params=_params(("parallel", "arbitrary")),
        name="proj",
    )(xb, w_main, w_bd, c, sa, sb)


def _attn_kernel(lam_ref, q_ref, k_ref, v_ref, g_ref, o_ref, *, tq, lam_init):
    s_len = q_ref.shape[0]
    lq = lam_ref[...]
    lam = (jnp.exp(jnp.sum(lq[0:1] * lq[1:2], axis=-1, keepdims=True))
           - jnp.exp(jnp.sum(lq[2:3] * lq[3:4], axis=-1, keepdims=True)) + lam_init)
    lane = lax.broadcasted_iota(jnp.int32, (1, LANES), 1)
    map0 = lane < DIFF_HEAD_DIM
    gain = g_ref[...] * (1.0 - lam_init)
    n_blk = s_len // tq

    def scores(i):
        skv = (i + 1) * tq
        q = q_ref[i * tq:(i + 1) * tq, :]
        zero = jnp.zeros_like(q)
        k = k_ref[0:skv, :]
        s0 = lax.dot_general(jnp.where(map0, q, zero), k, _NT, preferred_element_type=F32)
        s1 = lax.dot_general(jnp.where(map0, zero, q), k, _NT, preferred_element_type=F32)
        row = lax.broadcasted_iota(jnp.int32, (tq, skv), 0) + i * tq
        col = lax.broadcasted_iota(jnp.int32, (tq, skv), 1)
        keep = col <= row
        return jnp.where(keep, s0, -jnp.inf), jnp.where(keep, s1, -jnp.inf)

    def probs(s0, s1):
        e0 = jnp.exp(s0 - jnp.max(s0, axis=-1, keepdims=True))
        e1 = jnp.exp(s1 - jnp.max(s1, axis=-1, keepdims=True))
        r0 = 1.0 / jnp.sum(e0, axis=-1, keepdims=True)
        r1 = lam / jnp.sum(e1, axis=-1, keepdims=True)
        return (e0 * r0 - e1 * r1).astype(BF16)

    def emit(i, p):
        o = jnp.dot(p, v_ref[0:(i + 1) * tq, :], preferred_element_type=F32)
        ms = jnp.mean(o * o, axis=-1, keepdims=True)
        o_ref[i * tq:(i + 1) * tq, :] = (o * lax.rsqrt(ms + SUBLN_EPS) * gain).astype(o_ref.dtype)

    s_next = scores(0)
    for i in range(n_blk):
        s_cur = s_next
        if i + 1 < n_blk:
            s_next = scores(i + 1)
        emit(i, probs(*s_cur))


def _attn(qk, v, lam_vecs, norm_g, layer, *, batch, seq, n_heads, tq, lam_init):
    n = qk.shape[0]
    hd = 2 * DIFF_HEAD_DIM
    return pl.pallas_call(
        functools.partial(_attn_kernel, tq=tq, lam_init=lam_init),
        grid=(batch, n_heads),
        in_specs=[
            pl.BlockSpec((None, 4, DIFF_HEAD_DIM), lambda b, h: (layer, 0, 0)),
            pl.BlockSpec((seq, hd), lambda b, h: (b, h)),
            pl.BlockSpec((seq, hd), lambda b, h: (b, n_heads + h)),
            pl.BlockSpec((seq, hd), lambda b, h: (b, h)),
            pl.BlockSpec((None, 1, hd), lambda b, h: (layer, 0, 0)),
        ],
        out_specs=pl.BlockSpec((seq, hd), lambda b, h: (b, h)),
        out_shape=jax.ShapeDtypeStruct((n, n_heads * hd), BF16),
        compiler_params=_params(("parallel", "parallel")),
        name="diff_attn",
    )(lam_vecs, qk, qk, v, norm_g)


def _chunk_scan(x, pos, axis, reverse=False):
    size = x.shape[axis]
    s = 1
    while s < CHUNK:
        if reverse:
            x = x + jnp.where(pos < CHUNK - s, pltpu.roll(x, size - s, axis), 0.0)
        else:
            x = x + jnp.where(pos >= s, pltpu.roll(x, s, axis), 0.0)
        s *= 2
    return x


def _gdn_kernel(gq_ref, gk_ref, gv_ref, gz_ref, cw_ref, bdc_ref, bdr_ref,
                alc_ref, dtc_ref, alr_ref, dtr_ref, ng_ref, o_ref,
                hist_ref, state_ref, qn_ref, kn_ref, vc_ref, u_ref, wq_ref, akt_ref, oacc_ref, *, nh, group):
    rows_blk, width = gq_ref.shape
    n_super = rows_blk // SUPER

    @pl.when(pl.program_id(1) == 0)
    def _reset():
        hist_ref[...] = jnp.zeros_like(hist_ref)
        state_ref[...] = jnp.zeros_like(state_ref)

    rows8 = lax.broadcasted_iota(jnp.int32, (SUBLANES, 1), 0)

    def conv_silu(u_ref_, idx, dst_ref):
        w = cw_ref[:, idx * width:(idx + 1) * width]
        hist = hist_ref[idx]
        head = u_ref_[0:SUBLANES, :]
        acc = u_ref_[SUBLANES:rows_blk, :] * w[CONV_K - 1:CONV_K]
        top = head * w[CONV_K - 1:CONV_K]
        for j in range(CONV_K - 1):
            sh = CONV_K - 1 - j
            acc = acc + u_ref_[SUBLANES - sh:rows_blk - sh, :] * w[j:j + 1]
            top = top + jnp.where(rows8 < sh, pltpu.roll(hist, sh, 0), pltpu.roll(head, sh, 0)) * w[j:j + 1]
        hist_ref[idx] = u_ref_[rows_blk - SUBLANES:rows_blk, :]
        dst_ref[0:SUBLANES, :] = _silu(top)
        dst_ref[SUBLANES:rows_blk, :] = _silu(acc)

    conv_silu(gq_ref, 0, qn_ref)
    conv_silu(gk_ref, 1, kn_ref)
    conv_silu(gv_ref, 2, vc_ref)
    for h in range(nh):
        sl = slice(h * DN_HEAD_DIM, (h + 1) * DN_HEAD_DIM)
        qh, kh = qn_ref[:, sl], kn_ref[:, sl]
        qn_ref[:, sl] = qh * (lax.rsqrt(jnp.sum(qh * qh, axis=-1, keepdims=True) + L2_EPS) * DN_HEAD_DIM ** -0.5)
        kn_ref[:, sl] = kh * lax.rsqrt(jnp.sum(kh * kh, axis=-1, keepdims=True) + L2_EPS)

    rows = lax.broadcasted_iota(jnp.int32, (rows_blk, 1), 0)
    xg = bdc_ref[...]
    beta_all = jax.nn.sigmoid(xg)
    g_raw = -jnp.exp(alc_ref[...]) * jax.nn.softplus(xg + dtc_ref[...])
    g_cum = _chunk_scan(g_raw, rows % CHUNK, 0)
    eg_all = jnp.exp(g_cum)
    ekd_all = jnp.exp(_chunk_scan(g_raw, rows % CHUNK, 0, reverse=True) - g_raw)
    cols = lax.broadcasted_iota(jnp.int32, (1, rows_blk), 1)
    gr_raw = -jnp.exp(alr_ref[...]) * jax.nn.softplus(bdr_ref[...] + dtr_ref[...])
    g_row_all = _chunk_scan(gr_raw, cols % CHUNK, 1)

    ii = lax.broadcasted_iota(jnp.int32, (SUPER, SUPER), 0)
    jj = lax.broadcasted_iota(jnp.int32, (SUPER, SUPER), 1)
    same = (ii // CHUNK) == (jj // CHUNK)
    incl = same & (ii >= jj)
    strict = same & (ii > jj)
    eye = (ii == jj).astype(F32)

    def lower_left_mask(s):
        return ((ii // (2 * s)) == (jj // (2 * s))) & ((ii // s) % 2 == 1) & ((jj // s) % 2 == 0)

    lower_left = {}
    s = 1
    while s < CHUNK:
        lower_left[s] = lower_left_mask(s)
        s *= 2

    chunk_decay = {}
    chains = [(m, h) for m in range(n_super) for h in range(nh)]
    for g0 in range(0, len(chains), group):
        grp = chains[g0:g0 + group]
        lmats, ts, rhss = [], [], []
        for m, h in grp:
            rs = slice(m * SUPER, (m + 1) * SUPER)
            sl = slice(h * DN_HEAD_DIM, (h + 1) * DN_HEAD_DIM)
            gl = nh + h
            qn, kn, v = qn_ref[rs, sl], kn_ref[rs, sl], vc_ref[rs, sl]
            beta, eg, ekd = beta_all[rs, h:h + 1], eg_all[rs, gl:gl + 1], ekd_all[rs, gl:gl + 1]
            decay = jnp.where(
                incl, jnp.exp(jnp.where(incl, g_cum[rs, gl:gl + 1] - g_row_all[gl:gl + 1, rs], 0.0)), 0.0)
            kb = kn * beta
            knb = kn.astype(BF16)
            kk = lax.dot_general(kb.astype(BF16), knb, _NT, preferred_element_type=F32)
            qk = lax.dot_general(qn.astype(BF16), knb, _NT, preferred_element_type=F32)
            lmat = jnp.where(strict, kk * decay, 0.0)
            lmats.append(lmat)
            ts.append(eye - jnp.where(lower_left[1], lmat, 0.0))
            rhss.append(jnp.concatenate([v * beta, kb * eg], axis=1).astype(BF16))
            qd = (qn * eg).astype(BF16)
            base = 2 * SUPER * m
            for c2 in range(2):
                wq_ref[base + c2 * SUPER + CHUNK:base + (c2 + 1) * SUPER, sl] = qd[c2 * CHUNK:(c2 + 1) * CHUNK]
                last = (c2 + 1) * CHUNK - 1
                chunk_decay[m, c2, h] = eg[last:last + 1]
            akt_ref[base:base + SUPER, sl] = (qk * decay).astype(BF16)
            akt_ref[base + SUPER:base + 2 * SUPER, sl] = (kn * ekd).T.astype(BF16)
        s = 2
        while s < CHUNK:
            ys = [jnp.dot(jnp.where(lower_left[s], lmat, 0.0).astype(BF16), t.astype(BF16),
                          preferred_element_type=F32) for lmat, t in zip(lmats, ts)]
            ts = [t - jnp.dot(t.astype(BF16), y.astype(BF16), preferred_element_type=F32)
                  for t, y in zip(ts, ys)]
            s *= 2
        for (m, h), t, rhs in zip(grp, ts, rhss):
            rs = slice(m * SUPER, (m + 1) * SUPER)
            sl = slice(h * DN_HEAD_DIM, (h + 1) * DN_HEAD_DIM)
            sol = jnp.dot(t.astype(BF16), rhs, preferred_element_type=F32)
            u_ref[rs, sl] = sol[:, :DN_HEAD_DIM]
            w = sol[:, DN_HEAD_DIM:].astype(BF16)
            base = 2 * SUPER * m
            for c2 in range(2):
                wq_ref[base + c2 * SUPER:base + c2 * SUPER + CHUNK, sl] = w[c2 * CHUNK:(c2 + 1) * CHUNK]

    zeros_half = jnp.zeros((CHUNK, DN_HEAD_DIM), BF16)
    states = [state_ref[h] for h in range(nh)]
    for m in range(n_super):
        base = 2 * SUPER * m
        for c2 in range(2):
            r0 = m * SUPER + c2 * CHUNK
            for h in range(nh):
                sl = slice(h * DN_HEAD_DIM, (h + 1) * DN_HEAD_DIM)
                wq = wq_ref[base + c2 * SUPER:base + (c2 + 1) * SUPER, sl]
                res1 = jnp.dot(wq, states[h].astype(BF16), preferred_element_type=F32)
                vb = (u_ref[r0:r0 + CHUNK, sl] - res1[:CHUNK]).astype(BF16)
                rhs = jnp.concatenate([vb, zeros_half] if c2 == 0 else [zeros_half, vb], axis=0)
                res2 = jnp.dot(akt_ref[base:base + 2 * SUPER, sl], rhs, preferred_element_type=F32)
                oacc_ref[r0:r0 + CHUNK, sl] = res1[CHUNK:] + res2[c2 * CHUNK:(c2 + 1) * CHUNK]
                states[h] = states[h] * chunk_decay[m, c2, h] + res2[SUPER:]
    for h in range(nh):
        state_ref[h] = states[h]

    for h in range(nh):
        sl = slice(h * DN_HEAD_DIM, (h + 1) * DN_HEAD_DIM)
        o = oacc_ref[:, sl]
        ms = jnp.mean(o * o, axis=-1, keepdims=True)
        o_ref[:, sl] = (o * lax.rsqrt(ms + GATED_NORM_EPS) * ng_ref[...] * _silu(gz_ref[:, sl])).astype(o_ref.dtype)


def _gdn(proj, conv_w, bdc, bdr, alc, dtc, alr, dtr, norm_g, layer, *, batch, seq, nh, rows_blk):
    n = proj.shape[0]
    width = nh * DN_HEAD_DIM
    nblk = seq // rows_blk
    rows_r = bdr.shape[0]

    def col(off):
        return pl.BlockSpec((rows_blk, width), lambda b, t: (b * nblk + t, off))

    lane_vec = pl.BlockSpec((None, 1, LANES), lambda b, t: (layer, 0, 0))
    row_vec = pl.BlockSpec((None, rows_r, 1), lambda b, t: (layer, 0, 0))
    return pl.pallas_call(
        functools.partial(_gdn_kernel, nh=nh, group=GDN_CHAIN_GROUP),
        grid=(batch, nblk),
        in_specs=[
            col(0), col(1), col(2), col(3),
            pl.BlockSpec((None, CONV_K, 3 * width), lambda b, t: (layer, 0, 0)),
            pl.BlockSpec((rows_blk, LANES), lambda b, t: (b * nblk + t, 0)),
            pl.BlockSpec((rows_r, rows_blk), lambda b, t: (0, b * nblk + t)),
            lane_vec, lane_vec, row_vec, row_vec,
            pl.BlockSpec((None, 1, DN_HEAD_DIM), lambda b, t: (layer, 0, 0)),
        ],
        out_specs=pl.BlockSpec((rows_blk, width), lambda b, t: (b * nblk + t, 0)),
        out_shape=jax.ShapeDtypeStruct((n, width), BF16),
        scratch_shapes=[
            pltpu.VMEM((3, SUBLANES, width), F32),
            pltpu.VMEM((nh, DN_HEAD_DIM, DN_HEAD_DIM), F32),
            pltpu.VMEM((rows_blk, width), F32),
            pltpu.VMEM((rows_blk, width), F32),
            pltpu.VMEM((rows_blk, width), F32),
            pltpu.VMEM((rows_blk, width), F32),
            pltpu.VMEM((2 * rows_blk, width), BF16),
            pltpu.VMEM((2 * rows_blk, width), BF16),
            pltpu.VMEM((rows_blk, width), F32),
        ],
        compiler_params=_params(("parallel", "arbitrary")),
        name="gdn",
    )(proj, proj, proj, proj, conv_w, bdc, bdr, alc, dtc, alr, dtr, norm_g)


def _outproj_ln_kernel(x_ref, od_ref, on_ref, w1_ref, w2_ref, g_ref, b_ref, y_ref, *, alpha):
    mix = (jnp.dot(od_ref[...], w1_ref[...], preferred_element_type=F32)
           + jnp.dot(on_ref[...], w2_ref[...], preferred_element_type=F32))
    y_ref[...] = _layer_norm(alpha * x_ref[...] + mix, g_ref[...], b_ref[...])


def _outproj_ln(x, o_diff, o_dn, w_out, g, b, layer, *, alpha, tm):
    n, d = x.shape
    k1, k2 = o_diff.shape[1], o_dn.shape[1]
    assert k1 == k2
    vec = pl.BlockSpec((None, 1, d), lambda i: (layer, 0, 0))
    return pl.pallas_call(
        functools.partial(_outproj_ln_kernel, alpha=alpha),
        grid=(n // tm,),
        in_specs=[
            pl.BlockSpec((tm, d), lambda i: (i, 0)),
            pl.BlockSpec((tm, k1), lambda i: (i, 0)),
            pl.BlockSpec((tm, k2), lambda i: (i, 0)),
            pl.BlockSpec((None, k1, d), lambda i: (layer, 0, 0)),
            pl.BlockSpec((None, k2, d), lambda i: (layer, 1, 0)),
            vec, vec,
        ],
        out_specs=pl.BlockSpec((tm, d), lambda i: (i, 0)),
        out_shape=jax.ShapeDtypeStruct((n, d), F32),
        compiler_params=_params(("parallel",)),
        name="outproj_ln",
    )(x, o_diff, o_dn, w_out, w_out, g, b)


def _tile(n, pref):
    t = min(n, pref)
    assert n % t == 0, (n, pref)
    return t


def kernel(x, positions, ffn1_w_in, ffn1_w_out, ln1_g, ln1_b, w_in, conv_w, a_log, dt_bias, lam_q1, lam_k1, lam_q2, lam_k2, diff_norm_g, delta_norm_g, w_out, ln2_g, ln2_b, ffn2_w_in, ffn2_w_out, ln3_g, ln3_b):
    batch, seq, d = x.shape
    depth = ffn1_w_in.shape[0]
    d_ff = ffn1_w_out.shape[1]
    n = batch * seq
    diff_width = d // 2
    dn_width = d - diff_width
    n_diff_heads = diff_width // (2 * DIFF_HEAD_DIM)
    nh = dn_width // DN_HEAD_DIM
    qk_cols = 2 * (2 * n_diff_heads * DIFF_HEAD_DIM)
    v_cols = n_diff_heads * 2 * DIFF_HEAD_DIM
    main_cols = qk_cols + v_cols + 4 * dn_width
    assert w_in.shape[2] == main_cols + 2 * nh and 2 * nh <= LANES
    assert seq % SUPER == 0
    alpha = (2 * depth) ** 0.25

    tm = _tile(n, 512)
    tm_proj = _tile(n, 1024)
    tf = _tile(d_ff, 512)
    tn = _tile(v_cols, 1024)
    tq = _tile(seq, 256)
    gdn_rows = _tile(seq, 512)

    ffn1_in, ffn1_out = ffn1_w_in.astype(BF16), (0.5 * ffn1_w_out).astype(BF16)
    ffn2_in, ffn2_out = ffn2_w_in.astype(BF16), (0.5 * ffn2_w_out).astype(BF16)
    w_main = w_in.astype(BF16)
    w_gate = w_main[:, :, main_cols:]
    w_bd = jnp.pad(w_gate, ((0, 0), (0, 0), (0, LANES - 2 * nh)))
    rows_r = -(-2 * nh // SUBLANES) * SUBLANES
    w_out_b = w_out.astype(BF16)
    pad_lane = lambda v: jnp.pad(v.astype(F32), ((0, 0), (nh, LANES - 2 * nh))).reshape(depth, 1, LANES)
    pad_row = lambda v: jnp.pad(v.astype(F32), ((0, 0), (nh, rows_r - 2 * nh))).reshape(depth, rows_r, 1)
    alc, dtc, alr, dtr = pad_lane(a_log), pad_lane(dt_bias), pad_row(a_log), pad_row(dt_bias)
    lam_vecs = jnp.stack([lam_q1, lam_k1, lam_q2, lam_k2], axis=1).astype(F32)
    vec3 = lambda v: v.reshape(depth, 1, v.shape[-1])
    ln1_g, ln1_b, ln2_g, ln2_b, ln3_g, ln3_b = map(vec3, (ln1_g, ln1_b, ln2_g, ln2_b, ln3_g, ln3_b))
    diff_g, delta_g = vec3(diff_norm_g), vec3(delta_norm_g)

    rope_c, rope_sa, rope_sb = _rope_tables(positions, tm)

    h = x.reshape(n, d)
    for l in range(depth):
        lam_init = 0.8 - 0.6 * math.exp(-0.3 * l)
        h, hb16 = _ffn_ln(h, ffn1_in, ffn1_out, ln1_g, ln1_b, l, alpha=alpha, tm=tm, tf=tf, emit_bf16=True)
        qk, v, dn, bdc, bdr = _proj(hb16, w_main, w_bd, (rope_c, rope_sa, rope_sb), l, rows=rows_r,
                                    qk_cols=qk_cols, v_cols=v_cols, dn_cols=4 * dn_width, tm=tm_proj, tn=tn)
        o_diff = _attn(qk, v, lam_vecs, diff_g, l, batch=batch, seq=seq, n_heads=n_diff_heads, tq=tq,
                       lam_init=lam_init)
        o_dn = _gdn(dn, conv_w, bdc, bdr, alc, dtc, alr, dtr, delta_g, l, batch=batch, seq=seq, nh=nh,
                    rows_blk=gdn_rows)
        h = _outproj_ln(h, o_diff, o_dn, w_out_b, ln2_g, ln2_b, l, alpha=alpha, tm=tm)
        (h,) = _ffn_ln(h, ffn2_in, ffn2_out, ln3_g, ln3_b, l, alpha=alpha, tm=tm, tf=tf, emit_bf16=False)
    return h.reshape(batch, seq, d)
```

```python
import functools
import math

import jax
import jax.numpy as jnp
from jax import lax
from jax.experimental import pallas as pl
from jax.experimental.pallas import tpu as pltpu

F32 = jnp.float32
BF16 = jnp.bfloat16

LANES = 128
SUBLANES = 8
DIFF_HEAD_DIM = 64
DN_HEAD_DIM = 128
CONV_K = 4
CHUNK = 64
SUPER = 2 * CHUNK
GDN_CHAIN_GROUP = 16
ROPE_THETA = 500000.0
ROPE_DIM = DIFF_HEAD_DIM // 4
ROPE_HALF = ROPE_DIM // 2
LN_EPS = 1e-5
SUBLN_EPS = 1e-5
GATED_NORM_EPS = 1e-6
L2_EPS = 1e-6
VMEM_LIMIT_BYTES = 56 * 1024 * 1024

_NT = (((1,), (1,)), ((), ()))


def _params(sem):
    return pltpu.CompilerParams(dimension_semantics=sem, vmem_limit_bytes=VMEM_LIMIT_BYTES)


def _layer_norm(y, g, b):
    mu = jnp.mean(y, axis=-1, keepdims=True)
    yc = y - mu
    var = jnp.mean(yc * yc, axis=-1, keepdims=True)
    return yc * lax.rsqrt(var + LN_EPS) * g + b


def _silu(x):
    return x * jax.nn.sigmoid(x)


def _rope_table_kernel(pos_ref, inv_ref, c_ref, sa_ref, sb_ref):
    ang = pos_ref[...].astype(F32) * inv_ref[...]
    lane = lax.broadcasted_iota(jnp.int32, ang.shape, 1) % DIFF_HEAD_DIM
    cos, sin = jnp.cos(ang), jnp.sin(ang)
    c_ref[...] = jnp.where(lane < ROPE_DIM, cos, 1.0)
    sa_ref[...] = jnp.where(lane < ROPE_HALF, -sin, 0.0)
    sb_ref[...] = jnp.where((lane >= ROPE_HALF) & (lane < ROPE_DIM), sin, 0.0)


def _rope_tables(positions, tm):
    n = positions.size
    pos = positions.reshape(n, 1)
    inv_freq = ROPE_THETA ** (-jnp.arange(0, ROPE_DIM, 2, dtype=F32) / ROPE_DIM)
    lane = jnp.arange(LANES) % DIFF_HEAD_DIM
    inv = jnp.where(lane < ROPE_DIM, inv_freq[lane % ROPE_HALF], 0.0).reshape(1, LANES).astype(F32)
    out = jax.ShapeDtypeStruct((n, LANES), F32)
    spec = pl.BlockSpec((tm, LANES), lambda i: (i, 0))
    return pl.pallas_call(
        _rope_table_kernel,
        grid=(n // tm,),
        in_specs=[pl.BlockSpec((tm, 1), lambda i: (i, 0)), pl.BlockSpec((1, LANES), lambda i: (0, 0))],
        out_specs=[spec, spec, spec],
        out_shape=[out, out, out],
        compiler_params=_params(("parallel",)),
        name="rope_tables",
    )(pos, inv)


def _ffn_ln_kernel(x_ref, wg_ref, wu_ref, wo_ref, g_ref, b_ref, *rest, alpha, emit_bf16, n_row, ln_rows):
    if emit_bf16:
        y_ref, yb_ref, xb_ref, acc_ref = rest
    else:
        y_ref, xb_ref, acc_ref = rest
    i, j = pl.program_id(0), pl.program_id(1)
    tm = xb_ref.shape[0]
    slot = i % 2

    @pl.when((i == 0) & (j == 0))
    def _first():
        acc_ref[1] = jnp.zeros(acc_ref.shape[1:], F32)

    def ln_slice():
        r0 = pl.multiple_of(jnp.minimum(j * ln_rows, tm - ln_rows), SUBLANES)
        y = _layer_norm(acc_ref[1 - slot, pl.ds(r0, ln_rows), :], g_ref[...], b_ref[...])
        y_ref[pl.ds(r0, ln_rows), :] = y
        if emit_bf16:
            yb_ref[pl.ds(r0, ln_rows), :] = y.astype(BF16)

    @pl.when((i < n_row) & (j == 0))
    def _init():
        x = x_ref[...]
        xb_ref[...] = x.astype(BF16)
        acc_ref[slot] = alpha * x

    @pl.when(i < n_row)
    def _main():
        ln_slice()
        xb = xb_ref[...]
        hg = jnp.dot(xb, wg_ref[...], preferred_element_type=F32)
        hu = jnp.dot(xb, wu_ref[...], preferred_element_type=F32)
        a = (_silu(hg) * hu).astype(BF16)
        acc_ref[slot] += jnp.dot(a, wo_ref[...], preferred_element_type=F32)

    @pl.when(i == n_row)
    def _tail():
        ln_slice()


def _ffn_ln(x, w_in, w_out_half, g, b, layer, *, alpha, tm, tf, emit_bf16):
    n, d = x.shape
    f = w_out_half.shape[1]
    nf = f // tf
    n_row = n // tm
    ln_rows = -(-tm // (nf * SUBLANES)) * SUBLANES
    assert ln_rows <= tm
    last = n_row - 1
    col = lambda i, j: jnp.where(i < n_row, j, nf - 1)
    prev = pl.BlockSpec((tm, d), lambda i, j: (jnp.maximum(i - 1, 0), 0))
    vec = pl.BlockSpec((None, 1, d), lambda i, j: (layer, 0, 0))
    out_shape = [jax.ShapeDtypeStruct((n, d), F32)]
    out_specs = [prev]
    if emit_bf16:
        out_shape.append(jax.ShapeDtypeStruct((n, d), BF16))
        out_specs.append(prev)
    return pl.pallas_call(
        functools.partial(_ffn_ln_kernel, alpha=alpha, emit_bf16=emit_bf16, n_row=n_row, ln_rows=ln_rows),
        grid=(n_row + 1, nf),
        in_specs=[
            pl.BlockSpec((tm, d), lambda i, j: (jnp.minimum(i, last), 0)),
            pl.BlockSpec((None, d, tf), lambda i, j: (layer, 0, col(i, j))),
            pl.BlockSpec((None, d, tf), lambda i, j: (layer, 0, col(i, j) + nf)),
            pl.BlockSpec((None, tf, d), lambda i, j: (layer, col(i, j), 0)),
            vec, vec,
        ],
        out_specs=out_specs,
        out_shape=out_shape,
        scratch_shapes=[pltpu.VMEM((tm, d), BF16), pltpu.VMEM((2, tm, d), F32)],
        compiler_params=_params(("arbitrary", "arbitrary")),
        name="ffn_ln",
    )(x, w_in, w_in, w_out_half, g, b)


def _proj_kernel(xb_ref, w_ref, wbd_ref, c_ref, sa_ref, sb_ref,
                 qk_ref, v_ref, dn_ref, bdc_ref, bdr_ref, *, n_q, n_qk, n_v):
    j = pl.program_id(1)
    tm, tn = qk_ref.shape
    half = tm // 2

    @pl.when(j == 0)
    def _gates():
        gates = jnp.dot(xb_ref[...], wbd_ref[...], preferred_element_type=F32)
        bdc_ref[...] = gates
        bdr_ref[...] = gates.T[0:bdr_ref.shape[0], :]

    def tile(store):
        for r in range(2):
            rs = slice(r * half, (r + 1) * half)
            store(jnp.dot(xb_ref[rs, :], w_ref[...], preferred_element_type=F32), rs)

    def rope_store(acc, rs):
        scale = jnp.where(j < n_q, DIFF_HEAD_DIM ** -0.5, 1.0)
        c, sa, sb = c_ref[rs, :] * scale, sa_ref[rs, :] * scale, sb_ref[rs, :] * scale
        for t in range(tn // LANES):
            seg = acc[:, t * LANES:(t + 1) * LANES]
            rot = seg * c + pltpu.roll(seg, LANES - ROPE_HALF, 1) * sa + pltpu.roll(seg, ROPE_HALF, 1) * sb
            qk_ref[rs, t * LANES:(t + 1) * LANES] = rot.astype(qk_ref.dtype)

    def v_store(acc, rs):
        v_ref[rs, :] = acc.astype(v_ref.dtype)

    def dn_store(acc, rs):
        dn_ref[rs, :] = acc

    pl.when(j < n_qk)(lambda: tile(rope_store))
    pl.when((j >= n_qk) & (j < n_qk + n_v))(lambda: tile(v_store))
    pl.when(j >= n_qk + n_v)(lambda: tile(dn_store))


def _proj(xb, w_main, w_bd, rope, layer, *, rows, qk_cols, v_cols, dn_cols, tm, tn):
    n, d = xb.shape
    n_qk, n_v, n_dn = qk_cols // tn, v_cols // tn, dn_cols // tn
    c, sa, sb = rope
    tspec = pl.BlockSpec((tm, LANES), lambda i, j: (i, 0))
    return pl.pallas_call(
        functools.partial(_proj_kernel, n_q=n_qk // 2, n_qk=n_qk, n_v=n_v),
        grid=(n // tm, n_qk + n_v + n_dn),
        in_specs=[
            pl.BlockSpec((tm, d), lambda i, j: (i, 0)),
            pl.BlockSpec((None, d, tn), lambda i, j: (layer, 0, j)),
            pl.BlockSpec((None, d, LANES), lambda i, j: (layer, 0, 0)),
            tspec, tspec, tspec,
        ],
        out_specs=[
            pl.BlockSpec((tm, tn), lambda i, j: (i, jnp.minimum(j, n_qk - 1))),
            pl.BlockSpec((tm, tn), lambda i, j: (i, jnp.clip(j - n_qk, 0, n_v - 1))),
            pl.BlockSpec((tm, tn), lambda i, j: (i, jnp.clip(j - n_qk - n_v, 0, n_dn - 1))),
            pl.BlockSpec((tm, LANES), lambda i, j: (i, 0)),
            pl.BlockSpec((rows, tm), lambda i, j: (0, i)),
        ],
        out_shape=[
            jax.ShapeDtypeStruct((n, qk_cols), BF16),
            jax.ShapeDtypeStruct((n, v_cols), BF16),
            jax.ShapeDtypeStruct((n, dn_cols), F32),
            jax.ShapeDtypeStruct((n, LANES), F32),
            jax.ShapeDtypeStruct((rows, n), F32),
        ],
        compiler_params=_params(("parallel", "arbitrary")),
        name="proj",
    )(xb, w_main, w_bd, c, sa, sb)


def _attn_kernel(lam_ref, q_ref, k_ref, v_ref, g_ref, o_ref, *, tq, lam_init):
    s_len = q_ref.shape[0]
    lq = lam_ref[...]
    lam = (jnp.exp(jnp.sum(lq[0:1] * lq[1:2], axis=-1, keepdims=True))
           - jnp.exp(jnp.sum(lq[2:3] * lq[3:4], axis=-1, keepdims=True)) + lam_init)
    lane = lax.broadcasted_iota(jnp.int32, (1, LANES), 1)
    map0 = lane < DIFF_HEAD_DIM
    gain = g_ref[...] * (1.0 - lam_init)
    n_blk = s_len // tq

    def scores(i):
        skv = (i + 1) * tq
        q = q_ref[i * tq:(i + 1) * tq, :]
        zero = jnp.zeros_like(q)
        k = k_ref[0:skv, :]
        s0 = lax.dot_general(jnp.where(map0, q, zero), k, _NT, preferred_element_type=F32)
        s1 = lax.dot_general(jnp.where(map0, zero, q), k, _NT, preferred_element_type=F32)
        row = lax.broadcasted_iota(jnp.int32, (tq, skv), 0) + i * tq
        col = lax.broadcasted_iota(jnp.int32, (tq, skv), 1)
        keep = col <= row
        return jnp.where(keep, s0, -jnp.inf), jnp.where(keep, s1, -jnp.inf)

    def probs(s0, s1):
        e0 = jnp.exp(s0 - jnp.max(s0, axis=-1, keepdims=True))
        e1 = jnp.exp(s1 - jnp.max(s1, axis=-1, keepdims=True))
        r0 = 1.0 / jnp.sum(e0, axis=-1, keepdims=True)
        r1 = lam / jnp.sum(e1, axis=-1, keepdims=True)
        return (e0 * r0 - e1 * r1).astype(BF16)

    def emit(i, p):
        o = jnp.dot(p, v_ref[0:(i + 1) * tq, :], preferred_element_type=F32)
        ms = jnp.mean(o * o, axis=-1, keepdims=True)
        o_ref[i * tq:(i + 1) * tq, :] = (o * lax.rsqrt(ms + SUBLN_EPS) * gain).astype(o_ref.dtype)

    s_next = scores(0)
    for i in range(n_blk):
        s_cur = s_next
        if i + 1 < n_blk:
            s_next = scores(i + 1)
        emit(i, probs(*s_cur))


def _attn(qk, v, lam_vecs, norm_g, layer, *, batch, seq, n_heads, tq, lam_init):
    n = qk.shape[0]
    hd = 2 * DIFF_HEAD_DIM
    return pl.pallas_call(
        functools.partial(_attn_kernel, tq=tq, lam_init=lam_init),
        grid=(batch, n_heads),
        in_specs=[
            pl.BlockSpec((None, 4, DIFF_HEAD_DIM), lambda b, h: (layer, 0, 0)),
            pl.BlockSpec((seq, hd), lambda b, h: (b, h)),
            pl.BlockSpec((seq, hd), lambda b, h: (b, n_heads + h)),
            pl.BlockSpec((seq, hd), lambda b, h: (b, h)),
            pl.BlockSpec((None, 1, hd), lambda b, h: (layer, 0, 0)),
        ],
        out_specs=pl.BlockSpec((seq, hd), lambda b, h: (b, h)),
        out_shape=jax.ShapeDtypeStruct((n, n_heads * hd), BF16),
        compiler_params=_params(("parallel", "parallel")),
        name="diff_attn",
    )(lam_vecs, qk, qk, v, norm_g)


def _chunk_scan(x, pos, axis, reverse=False):
    size = x.shape[axis]
    s = 1
    while s < CHUNK:
        if reverse:
            x = x + jnp.where(pos < CHUNK - s, pltpu.roll(x, size - s, axis), 0.0)
        else:
            x = x + jnp.where(pos >= s, pltpu.roll(x, s, axis), 0.0)
        s *= 2
    return x


def _interleave(lead, fill):
    done = 0
    for idx, item in enumerate(lead):
        item()
        upto = (idx + 1) * len(fill) // len(lead)
        for f in fill[done:upto]:
            f()
        done = upto


def _gdn_kernel(gq_ref, gk_ref, gv_ref, gz_ref, cw_ref, bdc_ref, bdr_ref,
                alc_ref, dtc_ref, alr_ref, dtr_ref, ng_ref, o_ref,
                hist_ref, state_ref, qn_ref, kn_ref, vc_ref, u_ref, wq_ref, akt_ref, cd_ref, oacc_ref,
                *, nh, group, nblk):
    rows_blk, width = gq_ref.shape
    n_super = rows_blk // SUPER
    n_chunk = rows_blk // CHUNK
    s_idx = pl.program_id(0)
    wslot = s_idx % 2
    rslot = 1 - wslot

    @pl.when(s_idx == 0)
    def _first():
        hist_ref[...] = jnp.zeros_like(hist_ref)
        state_ref[...] = jnp.zeros_like(state_ref)
        u_ref[1] = jnp.zeros(u_ref.shape[1:], F32)
        wq_ref[1] = jnp.zeros(wq_ref.shape[1:], BF16)
        akt_ref[1] = jnp.zeros(akt_ref.shape[1:], BF16)
        cd_ref[1] = jnp.zeros(cd_ref.shape[1:], F32)

    rows8 = lax.broadcasted_iota(jnp.int32, (SUBLANES, 1), 0)
    seq_start = (s_idx % nblk) == 0

    def conv_silu(u_ref_, idx, dst_ref):
        w = cw_ref[:, idx * width:(idx + 1) * width]
        hist = jnp.where(seq_start, 0.0, hist_ref[idx])
        head = u_ref_[0:SUBLANES, :]
        acc = u_ref_[SUBLANES:rows_blk, :] * w[CONV_K - 1:CONV_K]
        top = head * w[CONV_K - 1:CONV_K]
        for j in range(CONV_K - 1):
            sh = CONV_K - 1 - j
            acc = acc + u_ref_[SUBLANES - sh:rows_blk - sh, :] * w[j:j + 1]
            top = top + jnp.where(rows8 < sh, pltpu.roll(hist, sh, 0), pltpu.roll(head, sh, 0)) * w[j:j + 1]
        hist_ref[idx] = u_ref_[rows_blk - SUBLANES:rows_blk, :]
        dst_ref[0:SUBLANES, :] = _silu(top)
        dst_ref[SUBLANES:rows_blk, :] = _silu(acc)

    def l2_norm_heads():
        for h in range(nh):
            sl = slice(h * DN_HEAD_DIM, (h + 1) * DN_HEAD_DIM)
            qh, kh = qn_ref[:, sl], kn_ref[:, sl]
            qn_ref[:, sl] = qh * (lax.rsqrt(jnp.sum(qh * qh, axis=-1, keepdims=True) + L2_EPS) * DN_HEAD_DIM ** -0.5)
            kn_ref[:, sl] = kh * lax.rsqrt(jnp.sum(kh * kh, axis=-1, keepdims=True) + L2_EPS)

    gates = {}

    def gate_scans():
        rows = lax.broadcasted_iota(jnp.int32, (rows_blk, 1), 0)
        xg = bdc_ref[...]
        gates["beta"] = jax.nn.sigmoid(xg)
        g_raw = -jnp.exp(alc_ref[...]) * jax.nn.softplus(xg + dtc_ref[...])
        g_cum = _chunk_scan(g_raw, rows % CHUNK, 0)
        gates["g"] = g_cum
        eg = jnp.exp(g_cum)
        gates["eg"] = eg
        gates["ekd"] = jnp.exp(_chunk_scan(g_raw, rows % CHUNK, 0, reverse=True) - g_raw)
        cols = lax.broadcasted_iota(jnp.int32, (1, rows_blk), 1)
        gr_raw = -jnp.exp(alr_ref[...]) * jax.nn.softplus(bdr_ref[...] + dtr_ref[...])
        gates["g_row"] = _chunk_scan(gr_raw, cols % CHUNK, 1)
        for ci in range(n_chunk):
            cd_ref[wslot, ci * SUBLANES:(ci + 1) * SUBLANES, :] = eg[(ci + 1) * CHUNK - SUBLANES:(ci + 1) * CHUNK]

    ii = lax.broadcasted_iota(jnp.int32, (SUPER, SUPER), 0)
    jj = lax.broadcasted_iota(jnp.int32, (SUPER, SUPER), 1)
    same = (ii // CHUNK) == (jj // CHUNK)
    incl = same & (ii >= jj)
    strict = same & (ii > jj)
    eye = (ii == jj).astype(F32)

    def lower_left_mask(s):
        return ((ii // (2 * s)) == (jj // (2 * s))) & ((ii // s) % 2 == 1) & ((jj // s) % 2 == 0)

    lower_left = {}
    s = 1
    while s < CHUNK:
        lower_left[s] = lower_left_mask(s)
        s *= 2

    def wy_items(grp):
        st = {}

        def setup():
            st["lmat"], st["t"], st["rhs"] = [], [], []
            for m, h in grp:
                rs = slice(m * SUPER, (m + 1) * SUPER)
                sl = slice(h * DN_HEAD_DIM, (h + 1) * DN_HEAD_DIM)
                gl = nh + h
                qn, kn, v = qn_ref[rs, sl], kn_ref[rs, sl], vc_ref[rs, sl]
                beta, eg, ekd = gates["beta"][rs, h:h + 1], gates["eg"][rs, gl:gl + 1], gates["ekd"][rs, gl:gl + 1]
                decay = jnp.where(
                    incl, jnp.exp(jnp.where(incl, gates["g"][rs, gl:gl + 1] - gates["g_row"][gl:gl + 1, rs], 0.0)),
                    0.0)
                kb = kn * beta
                knb = kn.astype(BF16)
                kk = lax.dot_general(kb.astype(BF16), knb, _NT, preferred_element_type=F32)
                qk = lax.dot_general(qn.astype(BF16), knb, _NT, preferred_element_type=F32)
                lmat = jnp.where(strict, kk * decay, 0.0)
                st["lmat"].append(lmat)
                st["t"].append(eye - jnp.where(lower_left[1], lmat, 0.0))
                st["rhs"].append(jnp.concatenate([v * beta, kb * eg], axis=1).astype(BF16))
                qd = (qn * eg).astype(BF16)
                base = 2 * SUPER * m
                for c2 in range(2):
                    wq_ref[wslot, base + c2 * SUPER + CHUNK:base + (c2 + 1) * SUPER, sl] = (
                        qd[c2 * CHUNK:(c2 + 1) * CHUNK])
                akt_ref[wslot, base:base + SUPER, sl] = (qk * decay).astype(BF16)
                akt_ref[wslot, base + SUPER:base + 2 * SUPER, sl] = (kn * ekd).T.astype(BF16)

        def level(s):
            def run():
                ys = [jnp.dot(jnp.where(lower_left[s], lmat, 0.0).astype(BF16), t.astype(BF16),
                              preferred_element_type=F32) for lmat, t in zip(st["lmat"], st["t"])]
                st["t"] = [t - jnp.dot(t.astype(BF16), y.astype(BF16), preferred_element_type=F32)
                           for t, y in zip(st["t"], ys)]
            return run

        def solve():
            for (m, h), t, rhs in zip(grp, st["t"], st["rhs"]):
                rs = slice(m * SUPER, (m + 1) * SUPER)
                sl = slice(h * DN_HEAD_DIM, (h + 1) * DN_HEAD_DIM)
                sol = jnp.dot(t.astype(BF16), rhs, preferred_element_type=F32)
                u_ref[wslot, rs, sl] = sol[:, :DN_HEAD_DIM]
                w = sol[:, DN_HEAD_DIM:].astype(BF16)
                base = 2 * SUPER * m
                for c2 in range(2):
                    wq_ref[wslot, base + c2 * SUPER:base + c2 * SUPER + CHUNK, sl] = w[c2 * CHUNK:(c2 + 1) * CHUNK]

        levels = []
        s = 2
        while s < CHUNK:
            levels.append(level(s))
            s *= 2
        return [setup] + levels + [solve]

    prep_items = [
        lambda: conv_silu(gq_ref, 0, qn_ref),
        lambda: conv_silu(gk_ref, 1, kn_ref),
        lambda: conv_silu(gv_ref, 2, vc_ref),
        l2_norm_heads,
        gate_scans,
    ]
    chains = [(m, h) for m in range(n_super) for h in range(nh)]
    for g0 in range(0, len(chains), group):
        prep_items += wy_items(chains[g0:g0 + group])

    zeros_half = jnp.zeros((CHUNK, DN_HEAD_DIM), BF16)
    keep_state = jnp.where(((s_idx + nblk - 1) % nblk) == 0, 0.0, 1.0)
    states = [state_ref[h] * keep_state for h in range(nh)]

    def chunk_item(ci):
        def run():
            m, c2 = divmod(ci, 2)
            base = 2 * SUPER * m
            r0 = ci * CHUNK
            cd_row = cd_ref[rslot, ci * SUBLANES + SUBLANES - 1:(ci + 1) * SUBLANES, :]
            for h in range(nh):
                sl = slice(h * DN_HEAD_DIM, (h + 1) * DN_HEAD_DIM)
                wq = wq_ref[rslot, base + c2 * SUPER:base + (c2 + 1) * SUPER, sl]
                res1 = jnp.dot(wq, states[h].astype(BF16), preferred_element_type=F32)
                vb = (u_ref[rslot, r0:r0 + CHUNK, sl] - res1[:CHUNK]).astype(BF16)
                rhs = jnp.concatenate([vb, zeros_half] if c2 == 0 else [zeros_half, vb], axis=0)
                res2 = jnp.dot(akt_ref[rslot, base:base + 2 * SUPER, sl], rhs, preferred_element_type=F32)
                oacc_ref[r0:r0 + CHUNK, sl] = res1[CHUNK:] + res2[c2 * CHUNK:(c2 + 1) * CHUNK]
                states[h] = states[h] * cd_row[:, nh + h:nh + h + 1] + res2[SUPER:]
        return run

    _interleave([chunk_item(ci) for ci in range(n_chunk)], prep_items)

    for h in range(nh):
        state_ref[h] = states[h]

    for h in range(nh):
        sl = slice(h * DN_HEAD_DIM, (h + 1) * DN_HEAD_DIM)
        o = oacc_ref[:, sl]
        ms = jnp.mean(o * o, axis=-1, keepdims=True)
        o_ref[:, sl] = (o * lax.rsqrt(ms + GATED_NORM_EPS) * ng_ref[...] * _silu(gz_ref[:, sl])).astype(o_ref.dtype)


def _gdn(proj, conv_w, bdc, bdr, alc, dtc, alr, dtr, norm_g, layer, *, batch, seq, nh, rows_blk):
    n = proj.shape[0]
    width = nh * DN_HEAD_DIM
    nblk = seq // rows_blk
    total = batch * nblk
    rows_r = bdr.shape[0]
    nxt = lambda s: jnp.minimum(s, total - 1)
    cur = lambda s: jnp.maximum(s - 1, 0)

    lane_vec = pl.BlockSpec((None, 1, LANES), lambda s: (layer, 0, 0))
    row_vec = pl.BlockSpec((None, rows_r, 1), lambda s: (layer, 0, 0))
    return pl.pallas_call(
        functools.partial(_gdn_kernel, nh=nh, group=GDN_CHAIN_GROUP, nblk=nblk),
        grid=(total + 1,),
        in_specs=[
            pl.BlockSpec((rows_blk, width), lambda s: (nxt(s), 0)),
            pl.BlockSpec((rows_blk, width), lambda s: (nxt(s), 1)),
            pl.BlockSpec((rows_blk, width), lambda s: (nxt(s), 2)),
            pl.BlockSpec((rows_blk, width), lambda s: (cur(s), 3)),
            pl.BlockSpec((None, CONV_K, 3 * width), lambda s: (layer, 0, 0)),
            pl.BlockSpec((rows_blk, LANES), lambda s: (nxt(s), 0)),
            pl.BlockSpec((rows_r, rows_blk), lambda s: (0, nxt(s))),
            lane_vec, lane_vec, row_vec, row_vec,
            pl.BlockSpec((None, 1, DN_HEAD_DIM), lambda s: (layer, 0, 0)),
        ],
        out_specs=pl.BlockSpec((rows_blk, width), lambda s: (cur(s), 0)),
        out_shape=jax.ShapeDtypeStruct((n, width), BF16),
        scratch_shapes=[
            pltpu.VMEM((3, SUBLANES, width), F32),
            pltpu.VMEM((nh, DN_HEAD_DIM, DN_HEAD_DIM), F32),
            pltpu.VMEM((rows_blk, width), F32),
            pltpu.VMEM((rows_blk, width), F32),
            pltpu.VMEM((rows_blk, width), F32),
            pltpu.VMEM((2, rows_blk, width), F32),
            pltpu.VMEM((2, 2 * rows_blk, width), BF16),
            pltpu.VMEM((2, 2 * rows_blk, width), BF16),
            pltpu.VMEM((2, SUBLANES * (rows_blk // CHUNK), LANES), F32),
            pltpu.VMEM((rows_blk, width), F32),
        ],
        compiler_params=_params(("arbitrary",)),
        name="gdn",
    )(proj, proj, proj, proj, conv_w, bdc, bdr, alc, dtc, alr, dtr, norm_g)


def _outproj_ln_kernel(x_ref, od_ref, on_ref, w1_ref, w2_ref, g_ref, b_ref, y_ref, *, alpha):
    mix = (jnp.dot(od_ref[...], w1_ref[...], preferred_element_type=F32)
           + jnp.dot(on_ref[...], w2_ref[...], preferred_element_type=F32))
    y_ref[...] = _layer_norm(alpha * x_ref[...] + mix, g_ref[...], b_ref[...])


def _outproj_ln(x, o_diff, o_dn, w_out, g, b, layer, *, alpha, tm):
    n, d = x.shape
    k1, k2 = o_diff.shape[1], o_dn.shape[1]
    assert k1 == k2
    vec = pl.BlockSpec((None, 1, d), lambda i: (layer, 0, 0))
    return pl.pallas_call(
        functools.partial(_outproj_ln_kernel, alpha=alpha),
        grid=(n // tm,),
        in_specs=[
            pl.BlockSpec((tm, d), lambda i: (i, 0)),
            pl.BlockSpec((tm, k1), lambda i: (i, 0)),
            pl.BlockSpec((tm, k2), lambda i: (i, 0)),
            pl.BlockSpec((None, k1, d), lambda i: (layer, 0, 0)),
            pl.BlockSpec((None, k2, d), lambda i: (layer, 1, 0)),
            vec, vec,
        ],
        out_specs=pl.BlockSpec((tm, d), lambda i: (i, 0)),
        out_shape=jax.ShapeDtypeStruct((n, d), F32),
        compiler_params=_params(("parallel",)),
        name="outproj_ln",
    )(x, o_diff, o_dn, w_out, w_out, g, b)


def _tile(n, pref):
    t = min(n, pref)
    assert n % t == 0, (n, pref)
    return t


def kernel(x, positions, ffn1_w_in, ffn1_w_out, ln1_g, ln1_b, w_in, conv_w, a_log, dt_bias, lam_q1, lam_k1, lam_q2, lam_k2, diff_norm_g, delta_norm_g, w_out, ln2_g, ln2_b, ffn2_w_in, ffn2_w_out, ln3_g, ln3_b):
    batch, seq, d = x.shape
    depth = ffn1_w_in.shape[0]
    d_ff = ffn1_w_out.shape[1]
    n = batch * seq
    diff_width = d // 2
    dn_width = d - diff_width
    n_diff_heads = diff_width // (2 * DIFF_HEAD_DIM)
    nh = dn_width // DN_HEAD_DIM
    qk_cols = 2 * (2 * n_diff_heads * DIFF_HEAD_DIM)
    v_cols = n_diff_heads * 2 * DIFF_HEAD_DIM
    main_cols = qk_cols + v_cols + 4 * dn_width
    assert w_in.shape[2] == main_cols + 2 * nh and 2 * nh <= LANES
    assert seq % SUPER == 0
    alpha = (2 * depth) ** 0.25

    tm = _tile(n, 512)
    tm_proj = _tile(n, 1024)
    tf = _tile(d_ff, 512)
    tn = _tile(v_cols, 1024)
    tq = _tile(seq, 256)
    gdn_rows = _tile(seq, 512)

    ffn1_in, ffn1_out = ffn1_w_in.astype(BF16), (0.5 * ffn1_w_out).astype(BF16)
    ffn2_in, ffn2_out = ffn2_w_in.astype(BF16), (0.5 * ffn2_w_out).astype(BF16)
    w_main = w_in.astype(BF16)
    w_gate = w_main[:, :, main_cols:]
    w_bd = jnp.pad(w_gate, ((0, 0), (0, 0), (0, LANES - 2 * nh)))
    rows_r = -(-2 * nh // SUBLANES) * SUBLANES
    w_out_b = w_out.astype(BF16)
    pad_lane = lambda v: jnp.pad(v.astype(F32), ((0, 0), (nh, LANES - 2 * nh))).reshape(depth, 1, LANES)
    pad_row = lambda v: jnp.pad(v.astype(F32), ((0, 0), (nh, rows_r - 2 * nh))).reshape(depth, rows_r, 1)
    alc, dtc, alr, dtr = pad_lane(a_log), pad_lane(dt_bias), pad_row(a_log), pad_row(dt_bias)
    lam_vecs = jnp.stack([lam_q1, lam_k1, lam_q2, lam_k2], axis=1).astype(F32)
    vec3 = lambda v: v.reshape(depth, 1, v.shape[-1])
    ln1_g, ln1_b, ln2_g, ln2_b, ln3_g, ln3_b = map(vec3, (ln1_g, ln1_b, ln2_g, ln2_b, ln3_g, ln3_b))
    diff_g, delta_g = vec3(diff_norm_g), vec3(delta_norm_g)

    rope_c, rope_sa, rope_sb = _rope_tables(positions, tm)

    h = x.reshape(n, d)
    for l in range(depth):
        lam_init = 0.8 - 0.6 * math.exp(-0.3 * l)
        h, hb16 = _ffn_ln(h, ffn1_in, ffn1_out, ln1_g, ln1_b, l, alpha=alpha, tm=tm, tf=tf, emit_bf16=True)
        qk, v, dn, bdc, bdr = _proj(hb16, w_main, w_bd, (rope_c, rope_sa, rope_sb), l, rows=rows_r,
                                    qk_cols=qk_cols, v_cols=v_cols, dn_cols=4 * dn_width, tm=tm_proj, tn=tn)
        o_diff = _attn(qk, v, lam_vecs, diff_g, l, batch=batch, seq=seq, n_heads=n_diff_heads, tq=tq,
                       lam_init=lam_init)
        o_dn = _gdn(dn, conv_w, bdc, bdr, alc, dtc, alr, dtr, delta_g, l, batch=batch, seq=seq, nh=nh,
                    rows_blk=gdn_rows)
        h = _outproj_ln(h, o_diff, o_dn, w_out_b, ln2_g, ln2_b, l, alpha=alpha, tm=tm)
        (h,) = _ffn_ln(h, ffn2_in, ffn2_out, ln3_g, ln3_b, l, alpha=alpha, tm=tm, tf=tf, emit_bf16=False)
    return h.reshape(batch, seq, d)
```

```python
import functools
import math

import jax
import jax.numpy as jnp
from jax import lax
from jax.experimental import pallas as pl
from jax.experimental.pallas import tpu as pltpu

F32 = jnp.float32
BF16 = jnp.bfloat16

LANES = 128
SUBLANES = 8
DIFF_HEAD_DIM = 64
DN_HEAD_DIM = 128
CONV_K = 4
CHUNK = 64
SUPER = 2 * CHUNK
GDN_CHAIN_GROUP = 16
ROPE_THETA = 500000.0
ROPE_DIM = DIFF_HEAD_DIM // 4
ROPE_HALF = ROPE_DIM // 2
LN_EPS = 1e-5
SUBLN_EPS = 1e-5
GATED_NORM_EPS = 1e-6
L2_EPS = 1e-6
VMEM_LIMIT_BYTES = 56 * 1024 * 1024

_NT = (((1,), (1,)), ((), ()))


def _params(sem):
    return pltpu.CompilerParams(dimension_semantics=sem, vmem_limit_bytes=VMEM_LIMIT_BYTES)


def _layer_norm(y, g, b):
    mu = jnp.mean(y, axis=-1, keepdims=True)
    yc = y - mu
    var = jnp.mean(yc * yc, axis=-1, keepdims=True)
    return yc * lax.rsqrt(var + LN_EPS) * g + b


def _silu(x):
    return x * jax.nn.sigmoid(x)


def _rope_table_kernel(pos_ref, inv_ref, c_ref, sa_ref, sb_ref):
    ang = pos_ref[...].astype(F32) * inv_ref[...]
    lane = lax.broadcasted_iota(jnp.int32, ang.shape, 1) % DIFF_HEAD_DIM
    cos, sin = jnp.cos(ang), jnp.sin(ang)
    c_ref[...] = jnp.where(lane < ROPE_DIM, cos, 1.0)
    sa_ref[...] = jnp.where(lane < ROPE_HALF, -sin, 0.0)
    sb_ref[...] = jnp.where((lane >= ROPE_HALF) & (lane < ROPE_DIM), sin, 0.0)


def _rope_tables(positions, tm):
    n = positions.size
    pos = positions.reshape(n, 1)
    inv_freq = ROPE_THETA ** (-jnp.arange(0, ROPE_DIM, 2, dtype=F32) / ROPE_DIM)
    lane = jnp.arange(LANES) % DIFF_HEAD_DIM
    inv = jnp.where(lane < ROPE_DIM, inv_freq[lane % ROPE_HALF], 0.0).reshape(1, LANES).astype(F32)
    out = jax.ShapeDtypeStruct((n, LANES), F32)
    spec = pl.BlockSpec((tm, LANES), lambda i: (i, 0))
    return pl.pallas_call(
        _rope_table_kernel,
        grid=(n // tm,),
        in_specs=[pl.BlockSpec((tm, 1), lambda i: (i, 0)), pl.BlockSpec((1, LANES), lambda i: (0, 0))],
        out_specs=[spec, spec, spec],
        out_shape=[out, out, out],
        compiler_params=_params(("parallel",)),
        name="rope_tables",
    )(pos, inv)


def _ffn_ln_kernel(x_ref, wg_ref, wu_ref, wo_ref, g_ref, b_ref, *rest, alpha, emit_bf16, n_row, ln_rows):
    if emit_bf16:
        y_ref, yb_ref, xb_ref, acc_ref = rest
    else:
        y_ref, xb_ref, acc_ref = rest
    i, j = pl.program_id(0), pl.program_id(1)
    tm = xb_ref.shape[0]
    slot = i % 2

    @pl.when((i == 0) & (j == 0))
    def _first():
        acc_ref[1] = jnp.zeros(acc_ref.shape[1:], F32)

    def ln_slice():
        r0 = pl.multiple_of(jnp.minimum(j * ln_rows, tm - ln_rows), SUBLANES)
        y = _layer_norm(acc_ref[1 - slot, pl.ds(r0, ln_rows), :], g_ref[...], b_ref[...])
        y_ref[pl.ds(r0, ln_rows), :] = y
        if emit_bf16:
            yb_ref[pl.ds(r0, ln_rows), :] = y.astype(BF16)

    @pl.when((i < n_row) & (j == 0))
    def _init():
        x = x_ref[...]
        xb_ref[...] = x.astype(BF16)
        acc_ref[slot] = alpha * x

    @pl.when(i < n_row)
    def _main():
        ln_slice()
        xb = xb_ref[...]
        hg = jnp.dot(xb, wg_ref[...], preferred_element_type=F32)
        hu = jnp.dot(xb, wu_ref[...], preferred_element_type=F32)
        a = (_silu(hg) * hu).astype(BF16)
        acc_ref[slot] += jnp.dot(a, wo_ref[...], preferred_element_type=F32)

    @pl.when(i == n_row)
    def _tail():
        ln_slice()


def _ffn_ln(x, w_in, w_out_half, g, b, layer, *, alpha, tm, tf, emit_bf16):
    n, d = x.shape
    f = w_out_half.shape[1]
    nf = f // tf
    n_row = n // tm
    ln_rows = -(-tm // (nf * SUBLANES)) * SUBLANES
    assert ln_rows <= tm
    last = n_row - 1
    col = lambda i, j: jnp.where(i < n_row, j, nf - 1)
    prev = pl.BlockSpec((tm, d), lambda i, j: (jnp.maximum(i - 1, 0), 0))
    vec = pl.BlockSpec((None, 1, d), lambda i, j: (layer, 0, 0))
    out_shape = [jax.ShapeDtypeStruct((n, d), F32)]
    out_specs = [prev]
    if emit_bf16:
        out_shape.append(jax.ShapeDtypeStruct((n, d), BF16))
        out_specs.append(prev)
    return pl.pallas_call(
        functools.partial(_ffn_ln_kernel, alpha=alpha, emit_bf16=emit_bf16, n_row=n_row, ln_rows=ln_rows),
        grid=(n_row + 1, nf),
        in_specs=[
            pl.BlockSpec((tm, d), lambda i, j: (jnp.minimum(i, last), 0)),
            pl.BlockSpec((None, d, tf), lambda i, j: (layer, 0, col(i, j))),
            pl.BlockSpec((None, d, tf), lambda i, j: (layer, 0, col(i, j) + nf)),
            pl.BlockSpec((None, tf, d), lambda i, j: (layer, col(i, j), 0)),
            vec, vec,
        ],
        out_specs=out_specs,
        out_shape=out_shape,
        scratch_shapes=[pltpu.VMEM((tm, d), BF16), pltpu.VMEM((2, tm, d), F32)],
        compiler_params=_params(("arbitrary", "arbitrary")),
        name="ffn_ln",
    )(x, w_in, w_in, w_out_half, g, b)


def _proj_kernel(xb_ref, w_ref, wbd_ref, c_ref, sa_ref, sb_ref,
                 qk_ref, v_ref, dn_ref, bdc_ref, bdr_ref, *, n_q, n_qk, n_v):
    j = pl.program_id(1)
    tm, tn = qk_ref.shape
    half = tm // 2

    @pl.when(j == 0)
    def _gates():
        gates = jnp.dot(xb_ref[...], wbd_ref[...], preferred_element_type=F32)
        bdc_ref[...] = gates
        bdr_ref[...] = gates.T[0:bdr_ref.shape[0], :]

    def tile(store):
        for r in range(2):
            rs = slice(r * half, (r + 1) * half)
            store(jnp.dot(xb_ref[rs, :], w_ref[...], preferred_element_type=F32), rs)

    def rope_store(acc, rs):
        scale = jnp.where(j < n_q, DIFF_HEAD_DIM ** -0.5, 1.0)
        c, sa, sb = c_ref[rs, :] * scale, sa_ref[rs, :] * scale, sb_ref[rs, :] * scale
        for t in range(tn // LANES):
            seg = acc[:, t * LANES:(t + 1) * LANES]
            rot = seg * c + pltpu.roll(seg, LANES - ROPE_HALF, 1) * sa + pltpu.roll(seg, ROPE_HALF, 1) * sb
            qk_ref[rs, t * LANES:(t + 1) * LANES] = rot.astype(qk_ref.dtype)

    def v_store(acc, rs):
        v_ref[rs, :] = acc.astype(v_ref.dtype)

    def dn_store(acc, rs):
        dn_ref[rs, :] = acc

    pl.when(j < n_qk)(lambda: tile(rope_store))
    pl.when((j >= n_qk) & (j < n_qk + n_v))(lambda: tile(v_store))
    pl.when(j >= n_qk + n_v)(lambda: tile(dn_store))


def _proj(xb, w_main, w_bd, rope, layer, *, rows, qk_cols, v_cols, dn_cols, tm, tn):
    n, d = xb.shape
    n_qk, n_v, n_dn = qk_cols // tn, v_cols // tn, dn_cols // tn
    c, sa, sb = rope
    tspec = pl.BlockSpec((tm, LANES), lambda i, j: (i, 0))
    return pl.pallas_call(
        functools.partial(_proj_kernel, n_q=n_qk // 2, n_qk=n_qk, n_v=n_v),
        grid=(n // tm, n_qk + n_v + n_dn),
        in_specs=[
            pl.BlockSpec((tm, d), lambda i, j: (i, 0)),
            pl.BlockSpec((None, d, tn), lambda i, j: (layer, 0, j)),
            pl.BlockSpec((None, d, LANES), lambda i, j: (layer, 0, 0)),
            tspec, tspec, tspec,
        ],
        out_specs=[
            pl.BlockSpec((tm, tn), lambda i, j: (i, jnp.minimum(j, n_qk - 1))),
            pl.BlockSpec((tm, tn), lambda i, j: (i, jnp.clip(j - n_qk, 0, n_v - 1))),
            pl.BlockSpec((tm, tn), lambda i, j: (i, jnp.clip(j - n_qk - n_v, 0, n_dn - 1))),
            pl.BlockSpec((tm, LANES), lambda i, j: (i, 0)),
            pl.BlockSpec((rows, tm), lambda i, j: (0, i)),
        ],
        out_shape=[
            jax.ShapeDtypeStruct((n, qk_cols), BF16),
            jax.ShapeDtypeStruct((n, v_cols), BF16),
            jax.ShapeDtypeStruct((n, dn_cols), F32),
            jax.ShapeDtypeStruct((n, LANES), F32),
            jax.ShapeDtypeStruct((rows, n), F32),
        ],
        compiler_params=_params(("parallel", "arbitrary")),
        name="proj",
    )(xb, w_main, w_bd, c, sa, sb)


def _attn_kernel(lam_ref, q_ref, k_ref, v_ref, g_ref, o_ref, *, tq, lam_init):
    s_len = q_ref.shape[0]
    lq = lam_ref[...]
    lam = (jnp.exp(jnp.sum(lq[0:1] * lq[1:2], axis=-1, keepdims=True))
           - jnp.exp(jnp.sum(lq[2:3] * lq[3:4], axis=-1, keepdims=True)) + lam_init)
    lane = lax.broadcasted_iota(jnp.int32, (1, LANES), 1)
    map0 = lane < DIFF_HEAD_DIM
    gain = g_ref[...] * (1.0 - lam_init)
    n_blk = s_len // tq

    def scores(i):
        skv = (i + 1) * tq
        q = q_ref[i * tq:(i + 1) * tq, :]
        zero = jnp.zeros_like(q)
        k = k_ref[0:skv, :]
        s0 = lax.dot_general(jnp.where(map0, q, zero), k, _NT, preferred_element_type=F32)
        s1 = lax.dot_general(jnp.where(map0, zero, q), k, _NT, preferred_element_type=F32)
        row = lax.broadcasted_iota(jnp.int32, (tq, skv), 0) + i * tq
        col = lax.broadcasted_iota(jnp.int32, (tq, skv), 1)
        keep = col <= row
        return jnp.where(keep, s0, -jnp.inf), jnp.where(keep, s1, -jnp.inf)

    def probs(s0, s1):
        e0 = jnp.exp(s0 - jnp.max(s0, axis=-1, keepdims=True))
        e1 = jnp.exp(s1 - jnp.max(s1, axis=-1, keepdims=True))
        r0 = 1.0 / jnp.sum(e0, axis=-1, keepdims=True)
        r1 = lam / jnp.sum(e1, axis=-1, keepdims=True)
        return (e0 * r0 - e1 * r1).astype(BF16)

    def emit(i, p):
        o = jnp.dot(p, v_ref[0:(i + 1) * tq, :], preferred_element_type=F32)
        ms = jnp.mean(o * o, axis=-1, keepdims=True)
        o_ref[i * tq:(i + 1) * tq, :] = (o * lax.rsqrt(ms + SUBLN_EPS) * gain).astype(o_ref.dtype)

    s_next = scores(0)
    for i in range(n_blk):
        s_cur = s_next
        if i + 1 < n_blk:
            s_next = scores(i + 1)
        emit(i, probs(*s_cur))


def _attn(qk, v, lam_vecs, norm_g, layer, *, batch, seq, n_heads, tq, lam_init):
    n = qk.shape[0]
    hd = 2 * DIFF_HEAD_DIM
    return pl.pallas_call(
        functools.partial(_attn_kernel, tq=tq, lam_init=lam_init),
        grid=(batch, n_heads),
        in_specs=[
            pl.BlockSpec((None, 4, DIFF_HEAD_DIM), lambda b, h: (layer, 0, 0)),
            pl.BlockSpec((seq, hd), lambda b, h: (b, h)),
            pl.BlockSpec((seq, hd), lambda b, h: (b, n_heads + h)),
            pl.BlockSpec((seq, hd), lambda b, h: (b, h)),
            pl.BlockSpec((None, 1, hd), lambda b, h: (layer, 0, 0)),
        ],
        out_specs=pl.BlockSpec((seq, hd), lambda b, h: (b, h)),
        out_shape=jax.ShapeDtypeStruct((n, n_heads * hd), BF16),
        compiler_params=_params(("parallel", "parallel")),
        name="diff_attn",
    )(lam_vecs, qk, qk, v, norm_g)


def _chunk_scan(x, pos, axis, reverse=False):
    size = x.shape[axis]
    s = 1
    while s < CHUNK:
        if reverse:
            x = x + jnp.where(pos < CHUNK - s, pltpu.roll(x, size - s, axis), 0.0)
        else:
            x = x + jnp.where(pos >= s, pltpu.roll(x, s, axis), 0.0)
        s *= 2
    return x


def _interleave(lead, fill):
    done = 0
    for idx, item in enumerate(lead):
        item()
        upto = (idx + 1) * len(fill) // len(lead)
        for f in fill[done:upto]:
            f()
        done = upto


def _gdn_kernel(gq_ref, gk_ref, gv_ref, gz_ref, cw_ref, bdc_ref, bdr_ref,
                alc_ref, dtc_ref, alr_ref, dtr_ref, ng_ref, o_ref,
                hist_ref, state_ref, qn_ref, kn_ref, vc_ref, u_ref, wq_ref, akt_ref, cd_ref,
                *, nh, group, nblk):
    rows_blk, width = gq_ref.shape
    n_super = rows_blk // SUPER
    n_chunk = rows_blk // CHUNK
    s_idx = pl.program_id(0)
    wslot = s_idx % 2
    rslot = 1 - wslot

    @pl.when(s_idx == 0)
    def _first():
        hist_ref[...] = jnp.zeros_like(hist_ref)
        state_ref[...] = jnp.zeros_like(state_ref)
        u_ref[1] = jnp.zeros(u_ref.shape[1:], F32)
        wq_ref[1] = jnp.zeros(wq_ref.shape[1:], BF16)
        akt_ref[1] = jnp.zeros(akt_ref.shape[1:], BF16)
        cd_ref[1] = jnp.zeros(cd_ref.shape[1:], F32)

    rows8 = lax.broadcasted_iota(jnp.int32, (SUBLANES, 1), 0)
    seq_start = (s_idx % nblk) == 0

    def conv_silu(u_ref_, idx, dst_ref):
        w = cw_ref[:, idx * width:(idx + 1) * width]
        hist = jnp.where(seq_start, 0.0, hist_ref[idx])
        head = u_ref_[0:SUBLANES, :]
        acc = u_ref_[SUBLANES:rows_blk, :] * w[CONV_K - 1:CONV_K]
        top = head * w[CONV_K - 1:CONV_K]
        for j in range(CONV_K - 1):
            sh = CONV_K - 1 - j
            acc = acc + u_ref_[SUBLANES - sh:rows_blk - sh, :] * w[j:j + 1]
            top = top + jnp.where(rows8 < sh, pltpu.roll(hist, sh, 0), pltpu.roll(head, sh, 0)) * w[j:j + 1]
        hist_ref[idx] = u_ref_[rows_blk - SUBLANES:rows_blk, :]
        dst_ref[0:SUBLANES, :] = _silu(top)
        dst_ref[SUBLANES:rows_blk, :] = _silu(acc)

    def l2_norm_heads():
        for h in range(nh):
            sl = slice(h * DN_HEAD_DIM, (h + 1) * DN_HEAD_DIM)
            qh, kh = qn_ref[:, sl], kn_ref[:, sl]
            qn_ref[:, sl] = qh * (lax.rsqrt(jnp.sum(qh * qh, axis=-1, keepdims=True) + L2_EPS) * DN_HEAD_DIM ** -0.5)
            kn_ref[:, sl] = kh * lax.rsqrt(jnp.sum(kh * kh, axis=-1, keepdims=True) + L2_EPS)

    gates = {}

    def gate_scans():
        rows = lax.broadcasted_iota(jnp.int32, (rows_blk, 1), 0)
        xg = bdc_ref[...]
        gates["beta"] = jax.nn.sigmoid(xg)
        g_raw = -jnp.exp(alc_ref[...]) * jax.nn.softplus(xg + dtc_ref[...])
        g_cum = _chunk_scan(g_raw, rows % CHUNK, 0)
        gates["g"] = g_cum
        eg = jnp.exp(g_cum)
        gates["eg"] = eg
        gates["ekd"] = jnp.exp(_chunk_scan(g_raw, rows % CHUNK, 0, reverse=True) - g_raw)
        cols = lax.broadcasted_iota(jnp.int32, (1, rows_blk), 1)
        gr_raw = -jnp.exp(alr_ref[...]) * jax.nn.softplus(bdr_ref[...] + dtr_ref[...])
        gates["g_row"] = _chunk_scan(gr_raw, cols % CHUNK, 1)
        for ci in range(n_chunk):
            cd_ref[wslot, ci * SUBLANES:(ci + 1) * SUBLANES, :] = eg[(ci + 1) * CHUNK - SUBLANES:(ci + 1) * CHUNK]

    ii = lax.broadcasted_iota(jnp.int32, (SUPER, SUPER), 0)
    jj = lax.broadcasted_iota(jnp.int32, (SUPER, SUPER), 1)
    same = (ii // CHUNK) == (jj // CHUNK)
    incl = same & (ii >= jj)
    strict = same & (ii > jj)
    eye = (ii == jj).astype(F32)

    def lower_left_mask(s):
        return ((ii // (2 * s)) == (jj // (2 * s))) & ((ii // s) % 2 == 1) & ((jj // s) % 2 == 0)

    lower_left = {}
    s = 1
    while s < CHUNK:
        lower_left[s] = lower_left_mask(s)
        s *= 2

    def wy_items(grp):
        st = {}

        def setup():
            st["lmat"], st["t"], st["rhs"] = [], [], []
            for m, h in grp:
                rs = slice(m * SUPER, (m + 1) * SUPER)
                sl = slice(h * DN_HEAD_DIM, (h + 1) * DN_HEAD_DIM)
                gl = nh + h
                qn, kn, v = qn_ref[rs, sl], kn_ref[rs, sl], vc_ref[rs, sl]
                beta, eg, ekd = gates["beta"][rs, h:h + 1], gates["eg"][rs, gl:gl + 1], gates["ekd"][rs, gl:gl + 1]
                decay = jnp.where(
                    incl, jnp.exp(jnp.where(incl, gates["g"][rs, gl:gl + 1] - gates["g_row"][gl:gl + 1, rs], 0.0)),
                    0.0)
                kb = kn * beta
                knb = kn.astype(BF16)
                kk = lax.dot_general(kb.astype(BF16), knb, _NT, preferred_element_type=F32)
                qk = lax.dot_general(qn.astype(BF16), knb, _NT, preferred_element_type=F32)
                lmat = jnp.where(strict, kk * decay, 0.0)
                st["lmat"].append(lmat)
                st["t"].append(eye - jnp.where(lower_left[1], lmat, 0.0))
                st["rhs"].append(jnp.concatenate([v * beta, kb * eg], axis=1).astype(BF16))
                qd = (qn * eg).astype(BF16)
                base = 2 * SUPER * m
                for c2 in range(2):
                    wq_ref[wslot, base + c2 * SUPER + CHUNK:base + (c2 + 1) * SUPER, sl] = (
                        qd[c2 * CHUNK:(c2 + 1) * CHUNK])
                akt_ref[wslot, base:base + SUPER, sl] = (qk * decay).astype(BF16)
                akt_ref[wslot, base + SUPER:base + 2 * SUPER, sl] = (kn * ekd).T.astype(BF16)

        def level(s):
            def run():
                ys = [jnp.dot(jnp.where(lower_left[s], lmat, 0.0).astype(BF16), t.astype(BF16),
                              preferred_element_type=F32) for lmat, t in zip(st["lmat"], st["t"])]
                st["t"] = [t - jnp.dot(t.astype(BF16), y.astype(BF16), preferred_element_type=F32)
                           for t, y in zip(st["t"], ys)]
            return run

        def solve():
            for (m, h), t, rhs in zip(grp, st["t"], st["rhs"]):
                rs = slice(m * SUPER, (m + 1) * SUPER)
                sl = slice(h * DN_HEAD_DIM, (h + 1) * DN_HEAD_DIM)
                sol = jnp.dot(t.astype(BF16), rhs, preferred_element_type=F32)
                u_ref[wslot, rs, sl] = sol[:, :DN_HEAD_DIM]
                w = sol[:, DN_HEAD_DIM:].astype(BF16)
                base = 2 * SUPER * m
                for c2 in range(2):
                    wq_ref[wslot, base + c2 * SUPER:base + c2 * SUPER + CHUNK, sl] = w[c2 * CHUNK:(c2 + 1) * CHUNK]

        levels = []
        s = 2
        while s < CHUNK:
            levels.append(level(s))
            s *= 2
        return [setup] + levels + [solve]

    prep_items = [
        lambda: conv_silu(gq_ref, 0, qn_ref),
        lambda: conv_silu(gk_ref, 1, kn_ref),
        lambda: conv_silu(gv_ref, 2, vc_ref),
        l2_norm_heads,
        gate_scans,
    ]
    chains = [(m, h) for m in range(n_super) for h in range(nh)]
    for g0 in range(0, len(chains), group):
        prep_items += wy_items(chains[g0:g0 + group])

    zeros_half = jnp.zeros((CHUNK, DN_HEAD_DIM), BF16)
    keep_state = jnp.where(((s_idx + nblk - 1) % nblk) == 0, 0.0, 1.0)
    states = [state_ref[h] * keep_state for h in range(nh)]

    def chunk_item(ci):
        def run():
            m, c2 = divmod(ci, 2)
            base = 2 * SUPER * m
            r0 = ci * CHUNK
            cd_row = cd_ref[rslot, ci * SUBLANES + SUBLANES - 1:(ci + 1) * SUBLANES, :]
            for h in range(nh):
                sl = slice(h * DN_HEAD_DIM, (h + 1) * DN_HEAD_DIM)
                wq = wq_ref[rslot, base + c2 * SUPER:base + (c2 + 1) * SUPER, sl]
                res1 = jnp.dot(wq, states[h].astype(BF16), preferred_element_type=F32)
                vb = (u_ref[rslot, r0:r0 + CHUNK, sl] - res1[:CHUNK]).astype(BF16)
                rhs = jnp.concatenate([vb, zeros_half] if c2 == 0 else [zeros_half, vb], axis=0)
                res2 = jnp.dot(akt_ref[rslot, base:base + 2 * SUPER, sl], rhs, preferred_element_type=F32)
                states[h] = states[h] * cd_row[:, nh + h:nh + h + 1] + res2[SUPER:]
                o = res1[CHUNK:] + res2[c2 * CHUNK:(c2 + 1) * CHUNK]
                ms = jnp.mean(o * o, axis=-1, keepdims=True)
                o_ref[r0:r0 + CHUNK, sl] = (o * lax.rsqrt(ms + GATED_NORM_EPS) * ng_ref[...]
                                            * _silu(gz_ref[r0:r0 + CHUNK, sl])).astype(o_ref.dtype)
        return run

    _interleave([chunk_item(ci) for ci in range(n_chunk)], prep_items)

    for h in range(nh):
        state_ref[h] = states[h]


def _gdn(proj, conv_w, bdc, bdr, alc, dtc, alr, dtr, norm_g, layer, *, batch, seq, nh, rows_blk):
    n = proj.shape[0]
    width = nh * DN_HEAD_DIM
    nblk = seq // rows_blk
    total = batch * nblk
    rows_r = bdr.shape[0]
    nxt = lambda s: jnp.minimum(s, total - 1)
    cur = lambda s: jnp.maximum(s - 1, 0)

    lane_vec = pl.BlockSpec((None, 1, LANES), lambda s: (layer, 0, 0))
    row_vec = pl.BlockSpec((None, rows_r, 1), lambda s: (layer, 0, 0))
    return pl.pallas_call(
        functools.partial(_gdn_kernel, nh=nh, group=GDN_CHAIN_GROUP, nblk=nblk),
        grid=(total + 1,),
        in_specs=[
            pl.BlockSpec((rows_blk, width), lambda s: (nxt(s), 0)),
            pl.BlockSpec((rows_blk, width), lambda s: (nxt(s), 1)),
            pl.BlockSpec((rows_blk, width), lambda s: (nxt(s), 2)),
            pl.BlockSpec((rows_blk, width), lambda s: (cur(s), 3)),
            pl.BlockSpec((None, CONV_K, 3 * width), lambda s: (layer, 0, 0)),
            pl.BlockSpec((rows_blk, LANES), lambda s: (nxt(s), 0)),
            pl.BlockSpec((rows_r, rows_blk), lambda s: (0, nxt(s))),
            lane_vec, lane_vec, row_vec, row_vec,
            pl.BlockSpec((None, 1, DN_HEAD_DIM), lambda s: (layer, 0, 0)),
        ],
        out_specs=pl.BlockSpec((rows_blk, width), lambda s: (cur(s), 0)),
        out_shape=jax.ShapeDtypeStruct((n, width), BF16),
        scratch_shapes=[
            pltpu.VMEM((3, SUBLANES, width), F32),
            pltpu.VMEM((nh, DN_HEAD_DIM, DN_HEAD_DIM), F32),
            pltpu.VMEM((rows_blk, width), F32),
            pltpu.VMEM((rows_blk, width), F32),
            pltpu.VMEM((rows_blk, width), F32),
            pltpu.VMEM((2, rows_blk, width), F32),
            pltpu.VMEM((2, 2 * rows_blk, width), BF16),
            pltpu.VMEM((2, 2 * rows_blk, width), BF16),
            pltpu.VMEM((2, SUBLANES * (rows_blk // CHUNK), LANES), F32),
        ],
        compiler_params=_params(("arbitrary",)),
        name="gdn",
    )(proj, proj, proj, proj, conv_w, bdc, bdr, alc, dtc, alr, dtr, norm_g)


def _outproj_ln_kernel(x_ref, od_ref, on_ref, w1_ref, w2_ref, g_ref, b_ref, y_ref, *, alpha):
    half = x_ref.shape[0] // 2
    for r in range(2):
        rs = slice(r * half, (r + 1) * half)
        mix = (jnp.dot(od_ref[rs, :], w1_ref[...], preferred_element_type=F32)
               + jnp.dot(on_ref[rs, :], w2_ref[...], preferred_element_type=F32))
        y_ref[rs, :] = _layer_norm(alpha * x_ref[rs, :] + mix, g_ref[...], b_ref[...])


def _outproj_ln(x, o_diff, o_dn, w_out, g, b, layer, *, alpha, tm):
    n, d = x.shape
    k1, k2 = o_diff.shape[1], o_dn.shape[1]
    assert k1 == k2
    vec = pl.BlockSpec((None, 1, d), lambda i: (layer, 0, 0))
    return pl.pallas_call(
        functools.partial(_outproj_ln_kernel, alpha=alpha),
        grid=(n // tm,),
        in_specs=[
            pl.BlockSpec((tm, d), lambda i: (i, 0)),
            pl.BlockSpec((tm, k1), lambda i: (i, 0)),
            pl.BlockSpec((tm, k2), lambda i: (i, 0)),
            pl.BlockSpec((None, k1, d), lambda i: (layer, 0, 0)),
            pl.BlockSpec((None, k2, d), lambda i: (layer, 1, 0)),
            vec, vec,
        ],
        out_specs=pl.BlockSpec((tm, d), lambda i: (i, 0)),
        out_shape=jax.ShapeDtypeStruct((n, d), F32),
        compiler_params=_params(("parallel",)),
        name="outproj_ln",
    )(x, o_diff, o_dn, w_out, w_out, g, b)


def _tile(n, pref):
    t = min(n, pref)
    assert n % t == 0, (n, pref)
    return t


def kernel(x, positions, ffn1_w_in, ffn1_w_out, ln1_g, ln1_b, w_in, conv_w, a_log, dt_bias, lam_q1, lam_k1, lam_q2, lam_k2, diff_norm_g, delta_norm_g, w_out, ln2_g, ln2_b, ffn2_w_in, ffn2_w_out, ln3_g, ln3_b):
    batch, seq, d = x.shape
    depth = ffn1_w_in.shape[0]
    d_ff = ffn1_w_out.shape[1]
    n = batch * seq
    diff_width = d // 2
    dn_width = d - diff_width
    n_diff_heads = diff_width // (2 * DIFF_HEAD_DIM)
    nh = dn_width // DN_HEAD_DIM
    qk_cols = 2 * (2 * n_diff_heads * DIFF_HEAD_DIM)
    v_cols = n_diff_heads * 2 * DIFF_HEAD_DIM
    main_cols = qk_cols + v_cols + 4 * dn_width
    assert w_in.shape[2] == main_cols + 2 * nh and 2 * nh <= LANES
    assert seq % SUPER == 0
    alpha = (2 * depth) ** 0.25

    tm = _tile(n, 512)
    tm_proj = _tile(n, 1024)
    tf = _tile(d_ff, 512)
    tn = _tile(v_cols, 1024)
    tq = _tile(seq, 256)
    gdn_rows = _tile(seq, 512)

    ffn1_in, ffn1_out = ffn1_w_in.astype(BF16), (0.5 * ffn1_w_out).astype(BF16)
    ffn2_in, ffn2_out = ffn2_w_in.astype(BF16), (0.5 * ffn2_w_out).astype(BF16)
    w_main = w_in.astype(BF16)
    w_gate = w_main[:, :, main_cols:]
    w_bd = jnp.pad(w_gate, ((0, 0), (0, 0), (0, LANES - 2 * nh)))
    rows_r = -(-2 * nh // SUBLANES) * SUBLANES
    w_out_b = w_out.astype(BF16)
    pad_lane = lambda v: jnp.pad(v.astype(F32), ((0, 0), (nh, LANES - 2 * nh))).reshape(depth, 1, LANES)
    pad_row = lambda v: jnp.pad(v.astype(F32), ((0, 0), (nh, rows_r - 2 * nh))).reshape(depth, rows_r, 1)
    alc, dtc, alr, dtr = pad_lane(a_log), pad_lane(dt_bias), pad_row(a_log), pad_row(dt_bias)
    lam_vecs = jnp.stack([lam_q1, lam_k1, lam_q2, lam_k2], axis=1).astype(F32)
    vec3 = lambda v: v.reshape(depth, 1, v.shape[-1])
    ln1_g, ln1_b, ln2_g, ln2_b, ln3_g, ln3_b = map(vec3, (ln1_g, ln1_b, ln2_g, ln2_b, ln3_g, ln3_b))
    diff_g, delta_g = vec3(diff_norm_g), vec3(delta_norm_g)

    rope_c, rope_sa, rope_sb = _rope_tables(positions, tm)

    h = x.reshape(n, d)
    for l in range(depth):
        lam_init = 0.8 - 0.6 * math.exp(-0.3 * l)
        h, hb16 = _ffn_ln(h, ffn1_in, ffn1_out, ln1_g, ln1_b, l, alpha=alpha, tm=tm, tf=tf, emit_bf16=True)
        qk, v, dn, bdc, bdr = _proj(hb16, w_main, w_bd, (rope_c, rope_sa, rope_sb), l, rows=rows_r,
                                    qk_cols=qk_cols, v_cols=v_cols, dn_cols=4 * dn_width, tm=tm_proj, tn=tn)
        o_diff = _attn(qk, v, lam_vecs, diff_g, l, batch=batch, seq=seq, n_heads=n_diff_heads, tq=tq,
                       lam_init=lam_init)
        o_dn = _gdn(dn, conv_w, bdc, bdr, alc, dtc, alr, dtr, delta_g, l, batch=batch, seq=seq, nh=nh,
                    rows_blk=gdn_rows)
        h = _outproj_ln(h, o_diff, o_dn, w_out_b, ln2_g, ln2_b, l, alpha=alpha, tm=tm)
        (h,) = _ffn_ln(h, ffn2_in, ffn2_out, ln3_g, ln3_b, l, alpha=alpha, tm=tm, tf=tf, emit_bf16=False)
    return h.reshape(batch, seq, d)
```

```python
import functools
import math

import jax
import jax.numpy as jnp
from jax import lax
from jax.experimental import pallas as pl
from jax.experimental.pallas import tpu as pltpu

F32 = jnp.float32
BF16 = jnp.bfloat16

LANES = 128
SUBLANES = 8
DIFF_HEAD_DIM = 64
DN_HEAD_DIM = 128
CONV_K = 4
CHUNK = 64
SUPER = 2 * CHUNK
GDN_CHAIN_GROUP = 16
ROPE_THETA = 500000.0
ROPE_DIM = DIFF_HEAD_DIM // 4
ROPE_HALF = ROPE_DIM // 2
LN_EPS = 1e-5
SUBLN_EPS = 1e-5
GATED_NORM_EPS = 1e-6
L2_EPS = 1e-6
LOG2_E = math.log2(math.e)
VMEM_LIMIT_BYTES = 56 * 1024 * 1024

_NT = (((1,), (1,)), ((), ()))


def _params(sem):
    return pltpu.CompilerParams(dimension_semantics=sem, vmem_limit_bytes=VMEM_LIMIT_BYTES)


def _layer_norm(y, g, b):
    mu = jnp.mean(y, axis=-1, keepdims=True)
    yc = y - mu
    var = jnp.mean(yc * yc, axis=-1, keepdims=True)
    return yc * lax.rsqrt(var + LN_EPS) * g + b


def _silu(x):
    return x * jax.nn.sigmoid(x)


def _rope_table_kernel(pos_ref, inv_ref, c_ref, sa_ref, sb_ref):
    ang = pos_ref[...].astype(F32) * inv_ref[...]
    lane = lax.broadcasted_iota(jnp.int32, ang.shape, 1) % DIFF_HEAD_DIM
    cos, sin = jnp.cos(ang), jnp.sin(ang)
    c_ref[...] = jnp.where(lane < ROPE_DIM, cos, 1.0)
    sa_ref[...] = jnp.where(lane < ROPE_HALF, -sin, 0.0)
    sb_ref[...] = jnp.where((lane >= ROPE_HALF) & (lane < ROPE_DIM), sin, 0.0)


def _rope_tables(positions, tm):
    n = positions.size
    pos = positions.reshape(n, 1)
    inv_freq = ROPE_THETA ** (-jnp.arange(0, ROPE_DIM, 2, dtype=F32) / ROPE_DIM)
    lane = jnp.arange(LANES) % DIFF_HEAD_DIM
    inv = jnp.where(lane < ROPE_DIM, inv_freq[lane % ROPE_HALF], 0.0).reshape(1, LANES).astype(F32)
    out = jax.ShapeDtypeStruct((n, LANES), F32)
    spec = pl.BlockSpec((tm, LANES), lambda i: (i, 0))
    return pl.pallas_call(
        _rope_table_kernel,
        grid=(n // tm,),
        in_specs=[pl.BlockSpec((tm, 1), lambda i: (i, 0)), pl.BlockSpec((1, LANES), lambda i: (0, 0))],
        out_specs=[spec, spec, spec],
        out_shape=[out, out, out],
        compiler_params=_params(("parallel",)),
        name="rope_tables",
    )(pos, inv)


def _ffn_ln_kernel(x_ref, wg_ref, wu_ref, wo_ref, g_ref, b_ref, *rest, alpha, emit_bf16, n_row, ln_rows):
    if emit_bf16:
        y_ref, yb_ref, xb_ref, acc_ref = rest
    else:
        y_ref, xb_ref, acc_ref = rest
    i, j = pl.program_id(0), pl.program_id(1)
    tm = xb_ref.shape[0]
    slot = i % 2

    @pl.when((i == 0) & (j == 0))
    def _first():
        acc_ref[1] = jnp.zeros(acc_ref.shape[1:], F32)

    def ln_slice():
        r0 = pl.multiple_of(jnp.minimum(j * ln_rows, tm - ln_rows), SUBLANES)
        y = _layer_norm(acc_ref[1 - slot, pl.ds(r0, ln_rows), :], g_ref[...], b_ref[...])
        y_ref[pl.ds(r0, ln_rows), :] = y
        if emit_bf16:
            yb_ref[pl.ds(r0, ln_rows), :] = y.astype(BF16)

    @pl.when((i < n_row) & (j == 0))
    def _init():
        x = x_ref[...]
        xb_ref[...] = x.astype(BF16)
        acc_ref[slot] = alpha * x

    @pl.when(i < n_row)
    def _main():
        ln_slice()
        xb = xb_ref[...]
        hg = jnp.dot(xb, wg_ref[...], preferred_element_type=F32)
        hu = jnp.dot(xb, wu_ref[...], preferred_element_type=F32)
        a = (_silu(hg) * hu).astype(BF16)
        acc_ref[slot] += jnp.dot(a, wo_ref[...], preferred_element_type=F32)

    @pl.when(i == n_row)
    def _tail():
        ln_slice()


def _ffn_ln(x, w_in, w_out_half, g, b, layer, *, alpha, tm, tf, emit_bf16):
    n, d = x.shape
    f = w_out_half.shape[1]
    nf = f // tf
    n_row = n // tm
    ln_rows = -(-tm // (nf * SUBLANES)) * SUBLANES
    assert ln_rows <= tm
    last = n_row - 1
    col = lambda i, j: jnp.where(i < n_row, j, nf - 1)
    prev = pl.BlockSpec((tm, d), lambda i, j: (jnp.maximum(i - 1, 0), 0))
    vec = pl.BlockSpec((None, 1, d), lambda i, j: (layer, 0, 0))
    out_shape = [jax.ShapeDtypeStruct((n, d), F32)]
    out_specs = [prev]
    if emit_bf16:
        out_shape.append(jax.ShapeDtypeStruct((n, d), BF16))
        out_specs.append(prev)
    return pl.pallas_call(
        functools.partial(_ffn_ln_kernel, alpha=alpha, emit_bf16=emit_bf16, n_row=n_row, ln_rows=ln_rows),
        grid=(n_row + 1, nf),
        in_specs=[
            pl.BlockSpec((tm, d), lambda i, j: (jnp.minimum(i, last), 0)),
            pl.BlockSpec((None, d, tf), lambda i, j: (layer, 0, col(i, j))),
            pl.BlockSpec((None, d, tf), lambda i, j: (layer, 0, col(i, j) + nf)),
            pl.BlockSpec((None, tf, d), lambda i, j: (layer, col(i, j), 0)),
            vec, vec,
        ],
        out_specs=out_specs,
        out_shape=out_shape,
        scratch_shapes=[pltpu.VMEM((tm, d), BF16), pltpu.VMEM((2, tm, d), F32)],
        compiler_params=_params(("arbitrary", "arbitrary")),
        name="ffn_ln",
    )(x, w_in, w_in, w_out_half, g, b)


def _proj_kernel(xb_ref, w_ref, wbd_ref, c_ref, sa_ref, sb_ref,
                 qk_ref, v_ref, dn_ref, bdc_ref, bdr_ref, *, n_q, n_qk, n_v):
    j = pl.program_id(1)
    tm, tn = qk_ref.shape
    half = tm // 2

    @pl.when(j == 0)
    def _gates():
        gates = jnp.dot(xb_ref[...], wbd_ref[...], preferred_element_type=F32)
        bdc_ref[...] = gates
        bdr_ref[...] = gates.T[0:bdr_ref.shape[0], :]

    def tile(store):
        for r in range(2):
            rs = slice(r * half, (r + 1) * half)
            store(jnp.dot(xb_ref[rs, :], w_ref[...], preferred_element_type=F32), rs)

    def rope_store(acc, rs):
        scale = jnp.where(j < n_q, DIFF_HEAD_DIM ** -0.5 * LOG2_E, 1.0)
        c, sa, sb = c_ref[rs, :] * scale, sa_ref[rs, :] * scale, sb_ref[rs, :] * scale
        for t in range(tn // LANES):
            seg = acc[:, t * LANES:(t + 1) * LANES]
            rot = seg * c + pltpu.roll(seg, LANES - ROPE_HALF, 1) * sa + pltpu.roll(seg, ROPE_HALF, 1) * sb
            qk_ref[rs, t * LANES:(t + 1) * LANES] = rot.astype(qk_ref.dtype)

    def v_store(acc, rs):
        v_ref[rs, :] = acc.astype(v_ref.dtype)

    def dn_store(acc, rs):
        dn_ref[rs, :] = acc

    pl.when(j < n_qk)(lambda: tile(rope_store))
    pl.when((j >= n_qk) & (j < n_qk + n_v))(lambda: tile(v_store))
    pl.when(j >= n_qk + n_v)(lambda: tile(dn_store))


def _proj(xb, w_main, w_bd, rope, layer, *, rows, qk_cols, v_cols, dn_cols, tm, tn):
    n, d = xb.shape
    n_qk, n_v, n_dn = qk_cols // tn, v_cols // tn, dn_cols // tn
    c, sa, sb = rope
    tspec = pl.BlockSpec((tm, LANES), lambda i, j: (i, 0))
    return pl.pallas_call(
        functools.partial(_proj_kernel, n_q=n_qk // 2, n_qk=n_qk, n_v=n_v),
        grid=(n // tm, n_qk + n_v + n_dn),
        in_specs=[
            pl.BlockSpec((tm, d), lambda i, j: (i, 0)),
            pl.BlockSpec((None, d, tn), lambda i, j: (layer, 0, j)),
            pl.BlockSpec((None, d, LANES), lambda i, j: (layer, 0, 0)),
            tspec, tspec, tspec,
        ],
        out_specs=[
            pl.BlockSpec((tm, tn), lambda i, j: (i, jnp.minimum(j, n_qk - 1))),
            pl.BlockSpec((tm, tn), lambda i, j: (i, jnp.clip(j - n_qk, 0, n_v - 1))),
            pl.BlockSpec((tm, tn), lambda i, j: (i, jnp.clip(j - n_qk - n_v, 0, n_dn - 1))),
            pl.BlockSpec((tm, LANES), lambda i, j: (i, 0)),
            pl.BlockSpec((rows, tm), lambda i, j: (0, i)),
        ],
        out_shape=[
            jax.ShapeDtypeStruct((n, qk_cols), BF16),
            jax.ShapeDtypeStruct((n, v_cols), BF16),
            jax.ShapeDtypeStruct((n, dn_cols), F32),
            jax.ShapeDtypeStruct((n, LANES), F32),
            jax.ShapeDtypeStruct((rows, n), F32),
        ],
        compiler_params=_params(("parallel", "arbitrary")),
        name="proj",
    )(xb, w_main, w_bd, c, sa, sb)


def _attn_kernel(lam_ref, q_ref, k_ref, v_ref, g_ref, o_ref, *, tq, lam_init):
    s_len = q_ref.shape[0]
    lq = lam_ref[...]
    lam = (jnp.exp(jnp.sum(lq[0:1] * lq[1:2], axis=-1, keepdims=True))
           - jnp.exp(jnp.sum(lq[2:3] * lq[3:4], axis=-1, keepdims=True)) + lam_init)
    lane = lax.broadcasted_iota(jnp.int32, (1, LANES), 1)
    map0 = lane < DIFF_HEAD_DIM
    gain = g_ref[...] * (1.0 - lam_init)
    n_blk = s_len // tq

    def scores(i):
        skv = (i + 1) * tq
        q = q_ref[i * tq:(i + 1) * tq, :]
        zero = jnp.zeros_like(q)
        k = k_ref[0:skv, :]
        s0 = lax.dot_general(jnp.where(map0, q, zero), k, _NT, preferred_element_type=F32)
        s1 = lax.dot_general(jnp.where(map0, zero, q), k, _NT, preferred_element_type=F32)
        row = lax.broadcasted_iota(jnp.int32, (tq, skv), 0) + i * tq
        col = lax.broadcasted_iota(jnp.int32, (tq, skv), 1)
        keep = col <= row
        return jnp.where(keep, s0, -jnp.inf), jnp.where(keep, s1, -jnp.inf)

    def probs(s0, s1):
        e0 = jnp.exp2(s0 - jnp.max(s0, axis=-1, keepdims=True))
        e1 = jnp.exp2(s1 - jnp.max(s1, axis=-1, keepdims=True))
        r0 = 1.0 / jnp.sum(e0, axis=-1, keepdims=True)
        r1 = lam / jnp.sum(e1, axis=-1, keepdims=True)
        return (e0 * r0 - e1 * r1).astype(BF16)

    def emit(i, p):
        o = jnp.dot(p, v_ref[0:(i + 1) * tq, :], preferred_element_type=F32)
        ms = jnp.mean(o * o, axis=-1, keepdims=True)
        o_ref[i * tq:(i + 1) * tq, :] = (o * lax.rsqrt(ms + SUBLN_EPS) * gain).astype(o_ref.dtype)

    s_next = scores(0)
    for i in range(n_blk):
        s_cur = s_next
        if i + 1 < n_blk:
            s_next = scores(i + 1)
        emit(i, probs(*s_cur))


def _attn(qk, v, lam_vecs, norm_g, layer, *, batch, seq, n_heads, tq, lam_init):
    n = qk.shape[0]
    hd = 2 * DIFF_HEAD_DIM
    return pl.pallas_call(
        functools.partial(_attn_kernel, tq=tq, lam_init=lam_init),
        grid=(batch, n_heads),
        in_specs=[
            pl.BlockSpec((None, 4, DIFF_HEAD_DIM), lambda b, h: (layer, 0, 0)),
            pl.BlockSpec((seq, hd), lambda b, h: (b, h)),
            pl.BlockSpec((seq, hd), lambda b, h: (b, n_heads + h)),
            pl.BlockSpec((seq, hd), lambda b, h: (b, h)),
            pl.BlockSpec((None, 1, hd), lambda b, h: (layer, 0, 0)),
        ],
        out_specs=pl.BlockSpec((seq, hd), lambda b, h: (b, h)),
        out_shape=jax.ShapeDtypeStruct((n, n_heads * hd), BF16),
        compiler_params=_params(("parallel", "parallel")),
        name="diff_attn",
    )(lam_vecs, qk, qk, v, norm_g)


def _chunk_scan(x, pos, axis, reverse=False):
    size = x.shape[axis]
    s = 1
    while s < CHUNK:
        if reverse:
            x = x + jnp.where(pos < CHUNK - s, pltpu.roll(x, size - s, axis), 0.0)
        else:
            x = x + jnp.where(pos >= s, pltpu.roll(x, s, axis), 0.0)
        s *= 2
    return x


def _interleave(lead, fill):
    done = 0
    for idx, item in enumerate(lead):
        item()
        upto = (idx + 1) * len(fill) // len(lead)
        for f in fill[done:upto]:
            f()
        done = upto


def _gdn_kernel(gq_ref, gk_ref, gv_ref, gz_ref, cw_ref, bdc_ref, bdr_ref,
                alc_ref, dtc_ref, alr_ref, dtr_ref, ng_ref, o_ref,
                hist_ref, state_ref, qn_ref, kn_ref, vc_ref, u_ref, wq_ref, akt_ref, cd_ref,
                *, nh, group, nblk):
    rows_blk, width = gq_ref.shape
    n_super = rows_blk // SUPER
    n_chunk = rows_blk // CHUNK
    s_idx = pl.program_id(0)
    wslot = s_idx % 2
    rslot = 1 - wslot

    @pl.when(s_idx == 0)
    def _first():
        hist_ref[...] = jnp.zeros_like(hist_ref)
        state_ref[...] = jnp.zeros_like(state_ref)
        u_ref[1] = jnp.zeros(u_ref.shape[1:], F32)
        wq_ref[1] = jnp.zeros(wq_ref.shape[1:], BF16)
        akt_ref[1] = jnp.zeros(akt_ref.shape[1:], BF16)
        cd_ref[1] = jnp.zeros(cd_ref.shape[1:], F32)

    rows8 = lax.broadcasted_iota(jnp.int32, (SUBLANES, 1), 0)
    seq_start = (s_idx % nblk) == 0

    def conv_silu(u_ref_, idx, dst_ref):
        w = cw_ref[:, idx * width:(idx + 1) * width]
        hist = jnp.where(seq_start, 0.0, hist_ref[idx])
        head = u_ref_[0:SUBLANES, :]
        acc = u_ref_[SUBLANES:rows_blk, :] * w[CONV_K - 1:CONV_K]
        top = head * w[CONV_K - 1:CONV_K]
        for j in range(CONV_K - 1):
            sh = CONV_K - 1 - j
            acc = acc + u_ref_[SUBLANES - sh:rows_blk - sh, :] * w[j:j + 1]
            top = top + jnp.where(rows8 < sh, pltpu.roll(hist, sh, 0), pltpu.roll(head, sh, 0)) * w[j:j + 1]
        hist_ref[idx] = u_ref_[rows_blk - SUBLANES:rows_blk, :]
        dst_ref[0:SUBLANES, :] = _silu(top)
        dst_ref[SUBLANES:rows_blk, :] = _silu(acc)

    def l2_norm_heads():
        for h in range(nh):
            sl = slice(h * DN_HEAD_DIM, (h + 1) * DN_HEAD_DIM)
            qh, kh = qn_ref[:, sl], kn_ref[:, sl]
            qn_ref[:, sl] = qh * (lax.rsqrt(jnp.sum(qh * qh, axis=-1, keepdims=True) + L2_EPS) * DN_HEAD_DIM ** -0.5)
            kn_ref[:, sl] = kh * lax.rsqrt(jnp.sum(kh * kh, axis=-1, keepdims=True) + L2_EPS)

    gates = {}

    def gate_scans():
        rows = lax.broadcasted_iota(jnp.int32, (rows_blk, 1), 0)
        xg = bdc_ref[...]
        gates["beta"] = jax.nn.sigmoid(xg)
        g_raw = -jnp.exp(alc_ref[...]) * jax.nn.softplus(xg + dtc_ref[...])
        g_cum = _chunk_scan(g_raw, rows % CHUNK, 0)
        gates["g"] = g_cum
        eg = jnp.exp(g_cum)
        gates["eg"] = eg
        gates["ekd"] = jnp.exp(_chunk_scan(g_raw, rows % CHUNK, 0, reverse=True) - g_raw)
        cols = lax.broadcasted_iota(jnp.int32, (1, rows_blk), 1)
        gr_raw = -jnp.exp(alr_ref[...]) * jax.nn.softplus(bdr_ref[...] + dtr_ref[...])
        gates["g_row"] = _chunk_scan(gr_raw, cols % CHUNK, 1)
        for ci in range(n_chunk):
            cd_ref[wslot, ci * SUBLANES:(ci + 1) * SUBLANES, :] = eg[(ci + 1) * CHUNK - SUBLANES:(ci + 1) * CHUNK]

    ii = lax.broadcasted_iota(jnp.int32, (SUPER, SUPER), 0)
    jj = lax.broadcasted_iota(jnp.int32, (SUPER, SUPER), 1)
    same = (ii // CHUNK) == (jj // CHUNK)
    incl = same & (ii >= jj)
    strict = same & (ii > jj)
    eye = (ii == jj).astype(F32)

    def lower_left_mask(s):
        return ((ii // (2 * s)) == (jj // (2 * s))) & ((ii // s) % 2 == 1) & ((jj // s) % 2 == 0)

    lower_left = {}
    s = 1
    while s < CHUNK:
        lower_left[s] = lower_left_mask(s)
        s *= 2

    def wy_items(grp):
        st = {}

        def setup():
            st["lmat"], st["t"], st["rhs"] = [], [], []
            for m, h in grp:
                rs = slice(m * SUPER, (m + 1) * SUPER)
                sl = slice(h * DN_HEAD_DIM, (h + 1) * DN_HEAD_DIM)
                gl = nh + h
                qn, kn, v = qn_ref[rs, sl], kn_ref[rs, sl], vc_ref[rs, sl]
                beta, eg, ekd = gates["beta"][rs, h:h + 1], gates["eg"][rs, gl:gl + 1], gates["ekd"][rs, gl:gl + 1]
                decay = jnp.where(
                    incl, jnp.exp(jnp.where(incl, gates["g"][rs, gl:gl + 1] - gates["g_row"][gl:gl + 1, rs], 0.0)),
                    0.0)
                kb = kn * beta
                knb = kn.astype(BF16)
                kk = lax.dot_general(kb.astype(BF16), knb, _NT, preferred_element_type=F32)
                qk = lax.dot_general(qn.astype(BF16), knb, _NT, preferred_element_type=F32)
                lmat = jnp.where(strict, kk * decay, 0.0)
                st["lmat"].append(lmat)
                st["t"].append(eye - jnp.where(lower_left[1], lmat, 0.0))
                st["rhs"].append(jnp.concatenate([v * beta, kb * eg], axis=1).astype(BF16))
                qd = (qn * eg).astype(BF16)
                base = 2 * SUPER * m
                for c2 in range(2):
                    wq_ref[wslot, base + c2 * SUPER + CHUNK:base + (c2 + 1) * SUPER, sl] = (
                        qd[c2 * CHUNK:(c2 + 1) * CHUNK])
                akt_ref[wslot, base:base + SUPER, sl] = (qk * decay).astype(BF16)
                akt_ref[wslot, base + SUPER:base + 2 * SUPER, sl] = (kn * ekd).T.astype(BF16)

        def level(s):
            def run():
                ys = [jnp.dot(jnp.where(lower_left[s], lmat, 0.0).astype(BF16), t.astype(BF16),
                              preferred_element_type=F32) for lmat, t in zip(st["lmat"], st["t"])]
                st["t"] = [t - jnp.dot(t.astype(BF16), y.astype(BF16), preferred_element_type=F32)
                           for t, y in zip(st["t"], ys)]
            return run

        def solve():
            for (m, h), t, rhs in zip(grp, st["t"], st["rhs"]):
                rs = slice(m * SUPER, (m + 1) * SUPER)
                sl = slice(h * DN_HEAD_DIM, (h + 1) * DN_HEAD_DIM)
                sol = jnp.dot(t.astype(BF16), rhs, preferred_element_type=F32)
                u_ref[wslot, rs, sl] = sol[:, :DN_HEAD_DIM]
                w = sol[:, DN_HEAD_DIM:].astype(BF16)
                base = 2 * SUPER * m
                for c2 in range(2):
                    wq_ref[wslot, base + c2 * SUPER:base + c2 * SUPER + CHUNK, sl] = w[c2 * CHUNK:(c2 + 1) * CHUNK]

        levels = []
        s = 2
        while s < CHUNK:
            levels.append(level(s))
            s *= 2
        return [setup] + levels + [solve]

    prep_items = [
        lambda: conv_silu(gq_ref, 0, qn_ref),
        lambda: conv_silu(gk_ref, 1, kn_ref),
        lambda: conv_silu(gv_ref, 2, vc_ref),
        l2_norm_heads,
        gate_scans,
    ]
    chains = [(m, h) for m in range(n_super) for h in range(nh)]
    for g0 in range(0, len(chains), group):
        prep_items += wy_items(chains[g0:g0 + group])

    zeros_half = jnp.zeros((CHUNK, DN_HEAD_DIM), BF16)
    keep_state = jnp.where(((s_idx + nblk - 1) % nblk) == 0, 0.0, 1.0)
    states = [state_ref[h] * keep_state for h in range(nh)]

    def chunk_item(ci):
        def run():
            m, c2 = divmod(ci, 2)
            base = 2 * SUPER * m
            r0 = ci * CHUNK
            cd_row = cd_ref[rslot, ci * SUBLANES + SUBLANES - 1:(ci + 1) * SUBLANES, :]
            for h in range(nh):
                sl = slice(h * DN_HEAD_DIM, (h + 1) * DN_HEAD_DIM)
                wq = wq_ref[rslot, base + c2 * SUPER:base + (c2 + 1) * SUPER, sl]
                res1 = jnp.dot(wq, states[h].astype(BF16), preferred_element_type=F32)
                vb = (u_ref[rslot, r0:r0 + CHUNK, sl] - res1[:CHUNK]).astype(BF16)
                rhs = jnp.concatenate([vb, zeros_half] if c2 == 0 else [zeros_half, vb], axis=0)
                res2 = jnp.dot(akt_ref[rslot, base:base + 2 * SUPER, sl], rhs, preferred_element_type=F32)
                states[h] = states[h] * cd_row[:, nh + h:nh + h + 1] + res2[SUPER:]
                o = res1[CHUNK:] + res2[c2 * CHUNK:(c2 + 1) * CHUNK]
                ms = jnp.mean(o * o, axis=-1, keepdims=True)
                o_ref[r0:r0 + CHUNK, sl] = (o * lax.rsqrt(ms + GATED_NORM_EPS) * ng_ref[...]
                                            * _silu(gz_ref[r0:r0 + CHUNK, sl])).astype(o_ref.dtype)
        return run

    _interleave([chunk_item(ci) for ci in range(n_chunk)], prep_items)

    for h in range(nh):
        state_ref[h] = states[h]


def _gdn(proj, conv_w, bdc, bdr, alc, dtc, alr, dtr, norm_g, layer, *, batch, seq, nh, rows_blk):
    n = proj.shape[0]
    width = nh * DN_HEAD_DIM
    nblk = seq // rows_blk
    total = batch * nblk
    rows_r = bdr.shape[0]
    nxt = lambda s: jnp.minimum(s, total - 1)
    cur = lambda s: jnp.maximum(s - 1, 0)

    lane_vec = pl.BlockSpec((None, 1, LANES), lambda s: (layer, 0, 0))
    row_vec = pl.BlockSpec((None, rows_r, 1), lambda s: (layer, 0, 0))
    return pl.pallas_call(
        functools.partial(_gdn_kernel, nh=nh, group=GDN_CHAIN_GROUP, nblk=nblk),
        grid=(total + 1,),
        in_specs=[
            pl.BlockSpec((rows_blk, width), lambda s: (nxt(s), 0)),
            pl.BlockSpec((rows_blk, width), lambda s: (nxt(s), 1)),
            pl.BlockSpec((rows_blk, width), lambda s: (nxt(s), 2)),
            pl.BlockSpec((rows_blk, width), lambda s: (cur(s), 3)),
            pl.BlockSpec((None, CONV_K, 3 * width), lambda s: (layer, 0, 0)),
            pl.BlockSpec((rows_blk, LANES), lambda s: (nxt(s), 0)),
            pl.BlockSpec((rows_r, rows_blk), lambda s: (0, nxt(s))),
            lane_vec, lane_vec, row_vec, row_vec,
            pl.BlockSpec((None, 1, DN_HEAD_DIM), lambda s: (layer, 0, 0)),
        ],
        out_specs=pl.BlockSpec((rows_blk, width), lambda s: (cur(s), 0)),
        out_shape=jax.ShapeDtypeStruct((n, width), BF16),
        scratch_shapes=[
            pltpu.VMEM((3, SUBLANES, width), F32),
            pltpu.VMEM((nh, DN_HEAD_DIM, DN_HEAD_DIM), F32),
            pltpu.VMEM((rows_blk, width), F32),
            pltpu.VMEM((rows_blk, width), F32),
            pltpu.VMEM((rows_blk, width), F32),
            pltpu.VMEM((2, rows_blk, width), F32),
            pltpu.VMEM((2, 2 * rows_blk, width), BF16),
            pltpu.VMEM((2, 2 * rows_blk, width), BF16),
            pltpu.VMEM((2, SUBLANES * (rows_blk // CHUNK), LANES), F32),
        ],
        compiler_params=_params(("arbitrary",)),
        name="gdn",
    )(proj, proj, proj, proj, conv_w, bdc, bdr, alc, dtc, alr, dtr, norm_g)


def _outproj_ln_kernel(x_ref, od_ref, on_ref, w1_ref, w2_ref, g_ref, b_ref, y_ref, *, alpha):
    half = x_ref.shape[0] // 2
    for r in range(2):
        rs = slice(r * half, (r + 1) * half)
        mix = (jnp.dot(od_ref[rs, :], w1_ref[...], preferred_element_type=F32)
               + jnp.dot(on_ref[rs, :], w2_ref[...], preferred_element_type=F32))
        y_ref[rs, :] = _layer_norm(alpha * x_ref[rs, :] + mix, g_ref[...], b_ref[...])


def _outproj_ln(x, o_diff, o_dn, w_out, g, b, layer, *, alpha, tm):
    n, d = x.shape
    k1, k2 = o_diff.shape[1], o_dn.shape[1]
    assert k1 == k2
    vec = pl.BlockSpec((None, 1, d), lambda i: (layer, 0, 0))
    return pl.pallas_call(
        functools.partial(_outproj_ln_kernel, alpha=alpha),
        grid=(n // tm,),
        in_specs=[
            pl.BlockSpec((tm, d), lambda i: (i, 0)),
            pl.BlockSpec((tm, k1), lambda i: (i, 0)),
            pl.BlockSpec((tm, k2), lambda i: (i, 0)),
            pl.BlockSpec((None, k1, d), lambda i: (layer, 0, 0)),
            pl.BlockSpec((None, k2, d), lambda i: (layer, 1, 0)),
            vec, vec,
        ],
        out_specs=pl.BlockSpec((tm, d), lambda i: (i, 0)),
        out_shape=jax.ShapeDtypeStruct((n, d), F32),
        compiler_params=_params(("parallel",)),
        name="outproj_ln",
    )(x, o_diff, o_dn, w_out, w_out, g, b)


def _tile(n, pref):
    t = min(n, pref)
    assert n % t == 0, (n, pref)
    return t


def kernel(x, positions, ffn1_w_in, ffn1_w_out, ln1_g, ln1_b, w_in, conv_w, a_log, dt_bias, lam_q1, lam_k1, lam_q2, lam_k2, diff_norm_g, delta_norm_g, w_out, ln2_g, ln2_b, ffn2_w_in, ffn2_w_out, ln3_g, ln3_b):
    batch, seq, d = x.shape
    depth = ffn1_w_in.shape[0]
    d_ff = ffn1_w_out.shape[1]
    n = batch * seq
    diff_width = d // 2
    dn_width = d - diff_width
    n_diff_heads = diff_width // (2 * DIFF_HEAD_DIM)
    nh = dn_width // DN_HEAD_DIM
    qk_cols = 2 * (2 * n_diff_heads * DIFF_HEAD_DIM)
    v_cols = n_diff_heads * 2 * DIFF_HEAD_DIM
    main_cols = qk_cols + v_cols + 4 * dn_width
    assert w_in.shape[2] == main_cols + 2 * nh and 2 * nh <= LANES
    assert seq % SUPER == 0
    alpha = (2 * depth) ** 0.25

    tm = _tile(n, 512)
    tm_proj = _tile(n, 1024)
    tf = _tile(d_ff, 512)
    tn = _tile(v_cols, 1024)
    tq = _tile(seq, 256)
    gdn_rows = _tile(seq, 512)

    ffn1_in, ffn1_out = ffn1_w_in.astype(BF16), (0.5 * ffn1_w_out).astype(BF16)
    ffn2_in, ffn2_out = ffn2_w_in.astype(BF16), (0.5 * ffn2_w_out).astype(BF16)
    w_main = w_in.astype(BF16)
    w_gate = w_main[:, :, main_cols:]
    w_bd = jnp.pad(w_gate, ((0, 0), (0, 0), (0, LANES - 2 * nh)))
    rows_r = -(-2 * nh // SUBLANES) * SUBLANES
    w_out_b = w_out.astype(BF16)
    pad_lane = lambda v: jnp.pad(v.astype(F32), ((0, 0), (nh, LANES - 2 * nh))).reshape(depth, 1, LANES)
    pad_row = lambda v: jnp.pad(v.astype(F32), ((0, 0), (nh, rows_r - 2 * nh))).reshape(depth, rows_r, 1)
    alc, dtc, alr, dtr = pad_lane(a_log), pad_lane(dt_bias), pad_row(a_log), pad_row(dt_bias)
    lam_vecs = jnp.stack([lam_q1, lam_k1, lam_q2, lam_k2], axis=1).astype(F32)
    vec3 = lambda v: v.reshape(depth, 1, v.shape[-1])
    ln1_g, ln1_b, ln2_g, ln2_b, ln3_g, ln3_b = map(vec3, (ln1_g, ln1_b, ln2_g, ln2_b, ln3_g, ln3_b))
    diff_g, delta_g = vec3(diff_norm_g), vec3(delta_norm_g)

    rope_c, rope_sa, rope_sb = _rope_tables(positions, tm)

    h = x.reshape(n, d)
    for l in range(depth):
        lam_init = 0.8 - 0.6 * math.exp(-0.3 * l)
        h, hb16 = _ffn_ln(h, ffn1_in, ffn1_out, ln1_g, ln1_b, l, alpha=alpha, tm=tm, tf=tf, emit_bf16=True)
        qk, v, dn, bdc, bdr = _proj(hb16, w_main, w_bd, (rope_c, rope_sa, rope_sb), l, rows=rows_r,
                                    qk_cols=qk_cols, v_cols=v_cols, dn_cols=4 * dn_width, tm=tm_proj, tn=tn)
        o_diff = _attn(qk, v, lam_vecs, diff_g, l, batch=batch, seq=seq, n_heads=n_diff_heads, tq=tq,
                       lam_init=lam_init)
        o_dn = _gdn(dn, conv_w, bdc, bdr, alc, dtc, alr, dtr, delta_g, l, batch=batch, seq=seq, nh=nh,
                    rows_blk=gdn_rows)
        h = _outproj_ln(h, o_diff, o_dn, w_out_b, ln2_g, ln2_b, l, alpha=alpha, tm=tm)
        (h,) = _ffn_ln(h, ffn2_in, ffn2_out, ln3_g, ln3_b, l, alpha=alpha, tm=tm, tf=tf, emit_bf16=False)
    return h.reshape(batch, seq, d)
```

```python
import functools
import math

import jax
import jax.numpy as jnp
from jax import lax
from jax.experimental import pallas as pl
from jax.experimental.pallas import tpu as pltpu

F32 = jnp.float32
BF16 = jnp.bfloat16

LANES = 128
SUBLANES = 8
DIFF_HEAD_DIM = 64
DN_HEAD_DIM = 128
CONV_K = 4
CHUNK = 64
SUPER = 2 * CHUNK
GDN_CHAIN_GROUP = 16
ROPE_THETA = 500000.0
ROPE_DIM = DIFF_HEAD_DIM // 4
ROPE_HALF = ROPE_DIM // 2
LN_EPS = 1e-5
SUBLN_EPS = 1e-5
GATED_NORM_EPS = 1e-6
L2_EPS = 1e-6
LOG2_E = math.log2(math.e)
VMEM_LIMIT_BYTES = 56 * 1024 * 1024

_NT = (((1,), (1,)), ((), ()))


def _params(sem):
    return pltpu.CompilerParams(dimension_semantics=sem, vmem_limit_bytes=VMEM_LIMIT_BYTES)


def _layer_norm(y, g, b):
    mu = jnp.mean(y, axis=-1, keepdims=True)
    yc = y - mu
    var = jnp.mean(yc * yc, axis=-1, keepdims=True)
    return yc * lax.rsqrt(var + LN_EPS) * g + b


def _silu(x):
    return x * jax.nn.sigmoid(x)


def _rope_table_kernel(pos_ref, inv_ref, c_ref, sa_ref, sb_ref):
    ang = pos_ref[...].astype(F32) * inv_ref[...]
    lane = lax.broadcasted_iota(jnp.int32, ang.shape, 1) % DIFF_HEAD_DIM
    cos, sin = jnp.cos(ang), jnp.sin(ang)
    c_ref[...] = jnp.where(lane < ROPE_DIM, cos, 1.0)
    sa_ref[...] = jnp.where(lane < ROPE_HALF, -sin, 0.0)
    sb_ref[...] = jnp.where((lane >= ROPE_HALF) & (lane < ROPE_DIM), sin, 0.0)


def _rope_tables(positions, tm):
    n = positions.size
    pos = positions.reshape(n, 1)
    inv_freq = ROPE_THETA ** (-jnp.arange(0, ROPE_DIM, 2, dtype=F32) / ROPE_DIM)
    lane = jnp.arange(LANES) % DIFF_HEAD_DIM
    inv = jnp.where(lane < ROPE_DIM, inv_freq[lane % ROPE_HALF], 0.0).reshape(1, LANES).astype(F32)
    out = jax.ShapeDtypeStruct((n, LANES), F32)
    spec = pl.BlockSpec((tm, LANES), lambda i: (i, 0))
    return pl.pallas_call(
        _rope_table_kernel,
        grid=(n // tm,),
        in_specs=[pl.BlockSpec((tm, 1), lambda i: (i, 0)), pl.BlockSpec((1, LANES), lambda i: (0, 0))],
        out_specs=[spec, spec, spec],
        out_shape=[out, out, out],
        compiler_params=_params(("parallel",)),
        name="rope_tables",
    )(pos, inv)


def _ffn_ln_kernel(x_ref, wg_ref, wu_ref, wo_ref, g_ref, b_ref, *rest, alpha, emit_bf16, n_row, ln_rows):
    if emit_bf16:
        y_ref, yb_ref, xb_ref, acc_ref = rest
    else:
        y_ref, xb_ref, acc_ref = rest
    i, j = pl.program_id(0), pl.program_id(1)
    tm = xb_ref.shape[0]
    slot = i % 2

    @pl.when((i == 0) & (j == 0))
    def _first():
        acc_ref[1] = jnp.zeros(acc_ref.shape[1:], F32)

    def ln_slice():
        r0 = pl.multiple_of(jnp.minimum(j * ln_rows, tm - ln_rows), SUBLANES)
        y = _layer_norm(acc_ref[1 - slot, pl.ds(r0, ln_rows), :], g_ref[...], b_ref[...])
        y_ref[pl.ds(r0, ln_rows), :] = y
        if emit_bf16:
            yb_ref[pl.ds(r0, ln_rows), :] = y.astype(BF16)

    @pl.when((i < n_row) & (j == 0))
    def _init():
        x = x_ref[...]
        xb_ref[...] = x.astype(BF16)
        acc_ref[slot] = alpha * x

    @pl.when(i < n_row)
    def _main():
        ln_slice()
        xb = xb_ref[...]
        hg = jnp.dot(xb, wg_ref[...], preferred_element_type=F32)
        hu = jnp.dot(xb, wu_ref[...], preferred_element_type=F32)
        a = (_silu(hg) * hu).astype(BF16)
        acc_ref[slot] += jnp.dot(a, wo_ref[...], preferred_element_type=F32)

    @pl.when(i == n_row)
    def _tail():
        ln_slice()


def _ffn_ln(x, w_in, w_out_half, g, b, layer, *, alpha, tm, tf, emit_bf16):
    n, d = x.shape
    f = w_out_half.shape[1]
    nf = f // tf
    n_row = n // tm
    ln_rows = -(-tm // (nf * SUBLANES)) * SUBLANES
    assert ln_rows <= tm
    last = n_row - 1
    col = lambda i, j: jnp.where(i < n_row, j, nf - 1)
    prev = pl.BlockSpec((tm, d), lambda i, j: (jnp.maximum(i - 1, 0), 0))
    vec = pl.BlockSpec((None, 1, d), lambda i, j: (layer, 0, 0))
    out_shape = [jax.ShapeDtypeStruct((n, d), F32)]
    out_specs = [prev]
    if emit_bf16:
        out_shape.append(jax.ShapeDtypeStruct((n, d), BF16))
        out_specs.append(prev)
    return pl.pallas_call(
        functools.partial(_ffn_ln_kernel, alpha=alpha, emit_bf16=emit_bf16, n_row=n_row, ln_rows=ln_rows),
        grid=(n_row + 1, nf),
        in_specs=[
            pl.BlockSpec((tm, d), lambda i, j: (jnp.minimum(i, last), 0)),
            pl.BlockSpec((None, d, tf), lambda i, j: (layer, 0, col(i, j))),
            pl.BlockSpec((None, d, tf), lambda i, j: (layer, 0, col(i, j) + nf)),
            pl.BlockSpec((None, tf, d), lambda i, j: (layer, col(i, j), 0)),
            vec, vec,
        ],
        out_specs=out_specs,
        out_shape=out_shape,
        scratch_shapes=[pltpu.VMEM((tm, d), BF16), pltpu.VMEM((2, tm, d), F32)],
        compiler_params=_params(("arbitrary", "arbitrary")),
        name="ffn_ln",
    )(x, w_in, w_in, w_out_half, g, b)


def _proj_kernel(xb_ref, w_ref, wbd_ref, c_ref, sa_ref, sb_ref,
                 qk_ref, v_ref, dn_ref, bdc_ref, bdr_ref, *, n_q, n_qk, n_v):
    j = pl.program_id(1)
    tm, tn = qk_ref.shape
    half = tm // 2

    @pl.when(j == 0)
    def _gates():
        gates = jnp.dot(xb_ref[...], wbd_ref[...], preferred_element_type=F32)
        bdc_ref[...] = gates
        bdr_ref[...] = gates.T[0:bdr_ref.shape[0], :]

    def tile(store):
        for r in range(2):
            rs = slice(r * half, (r + 1) * half)
            store(jnp.dot(xb_ref[rs, :], w_ref[...], preferred_element_type=F32), rs)

    def rope_store(acc, rs):
        scale = jnp.where(j < n_q, DIFF_HEAD_DIM ** -0.5 * LOG2_E, 1.0)
        c, sa, sb = c_ref[rs, :] * scale, sa_ref[rs, :] * scale, sb_ref[rs, :] * scale
        for t in range(tn // LANES):
            seg = acc[:, t * LANES:(t + 1) * LANES]
            rot = seg * c + pltpu.roll(seg, LANES - ROPE_HALF, 1) * sa + pltpu.roll(seg, ROPE_HALF, 1) * sb
            qk_ref[rs, t * LANES:(t + 1) * LANES] = rot.astype(qk_ref.dtype)

    def v_store(acc, rs):
        v_ref[rs, :] = acc.astype(v_ref.dtype)

    def dn_store(acc, rs):
        dn_ref[rs, :] = acc

    pl.when(j < n_qk)(lambda: tile(rope_store))
    pl.when((j >= n_qk) & (j < n_qk + n_v))(lambda: tile(v_store))
    pl.when(j >= n_qk + n_v)(lambda: tile(dn_store))


def _proj(xb, w_main, w_bd, rope, layer, *, rows, qk_cols, v_cols, dn_cols, tm, tn):
    n, d = xb.shape
    n_qk, n_v, n_dn = qk_cols // tn, v_cols // tn, dn_cols // tn
    c, sa, sb = rope
    tspec = pl.BlockSpec((tm, LANES), lambda i, j: (i, 0))
    return pl.pallas_call(
        functools.partial(_proj_kernel, n_q=n_qk // 2, n_qk=n_qk, n_v=n_v),
        grid=(n // tm, n_qk + n_v + n_dn),
        in_specs=[
            pl.BlockSpec((tm, d), lambda i, j: (i, 0)),
            pl.BlockSpec((None, d, tn), lambda i, j: (layer, 0, j)),
            pl.BlockSpec((None, d, LANES), lambda i, j: (layer, 0, 0)),
            tspec, tspec, tspec,
        ],
        out_specs=[
            pl.BlockSpec((tm, tn), lambda i, j: (i, jnp.minimum(j, n_qk - 1))),
            pl.BlockSpec((tm, tn), lambda i, j: (i, jnp.clip(j - n_qk, 0, n_v - 1))),
            pl.BlockSpec((tm, tn), lambda i, j: (i, jnp.clip(j - n_qk - n_v, 0, n_dn - 1))),
            pl.BlockSpec((tm, LANES), lambda i, j: (i, 0)),
            pl.BlockSpec((rows, tm), lambda i, j: (0, i)),
        ],
        out_shape=[
            jax.ShapeDtypeStruct((n, qk_cols), BF16),
            jax.ShapeDtypeStruct((n, v_cols), BF16),
            jax.ShapeDtypeStruct((n, dn_cols), F32),
            jax.ShapeDtypeStruct((n, LANES), F32),
            jax.ShapeDtypeStruct((rows, n), F32),
        ],
        compiler_params=_params(("parallel", "arbitrary")),
        name="proj",
    )(xb, w_main, w_bd, c, sa, sb)


def _attn_kernel(lam_ref, q_ref, k_ref, v_ref, g_ref, o_ref, *, tq, lam_init):
    s_len = q_ref.shape[0]
    lq = lam_ref[...]
    lam = (jnp.exp(jnp.sum(lq[0:1] * lq[1:2], axis=-1, keepdims=True))
           - jnp.exp(jnp.sum(lq[2:3] * lq[3:4], axis=-1, keepdims=True)) + lam_init)
    lane = lax.broadcasted_iota(jnp.int32, (1, LANES), 1)
    map0 = lane < DIFF_HEAD_DIM
    gain = g_ref[...] * (1.0 - lam_init)
    n_blk = s_len // tq

    def scores(i):
        skv = (i + 1) * tq
        q = q_ref[i * tq:(i + 1) * tq, :]
        zero = jnp.zeros_like(q)
        k = k_ref[0:skv, :]
        s0 = lax.dot_general(jnp.where(map0, q, zero), k, _NT, preferred_element_type=F32)
        s1 = lax.dot_general(jnp.where(map0, zero, q), k, _NT, preferred_element_type=F32)
        row = lax.broadcasted_iota(jnp.int32, (tq, skv), 0) + i * tq
        col = lax.broadcasted_iota(jnp.int32, (tq, skv), 1)
        keep = col <= row
        return jnp.where(keep, s0, -jnp.inf), jnp.where(keep, s1, -jnp.inf)

    def probs(s0, s1):
        e0 = jnp.exp2(s0 - jnp.max(s0, axis=-1, keepdims=True))
        e1 = jnp.exp2(s1 - jnp.max(s1, axis=-1, keepdims=True))
        l0 = jnp.sum(e0, axis=-1, keepdims=True)
        l1 = jnp.sum(e1, axis=-1, keepdims=True)
        return (e0 - e1 * (lam * l0 / l1)).astype(BF16), 1.0 / l0

    def emit(i, p_and_r0):
        p, r0 = p_and_r0
        o = jnp.dot(p, v_ref[0:(i + 1) * tq, :], preferred_element_type=F32) * r0
        ms = jnp.mean(o * o, axis=-1, keepdims=True)
        o_ref[i * tq:(i + 1) * tq, :] = (o * lax.rsqrt(ms + SUBLN_EPS) * gain).astype(o_ref.dtype)

    s_next = scores(0)
    for i in range(n_blk):
        s_cur = s_next
        if i + 1 < n_blk:
            s_next = scores(i + 1)
        emit(i, probs(*s_cur))


def _attn(qk, v, lam_vecs, norm_g, layer, *, batch, seq, n_heads, tq, lam_init):
    n = qk.shape[0]
    hd = 2 * DIFF_HEAD_DIM
    return pl.pallas_call(
        functools.partial(_attn_kernel, tq=tq, lam_init=lam_init),
        grid=(batch, n_heads),
        in_specs=[
            pl.BlockSpec((None, 4, DIFF_HEAD_DIM), lambda b, h: (layer, 0, 0)),
            pl.BlockSpec((seq, hd), lambda b, h: (b, h)),
            pl.BlockSpec((seq, hd), lambda b, h: (b, n_heads + h)),
            pl.BlockSpec((seq, hd), lambda b, h: (b, h)),
            pl.BlockSpec((None, 1, hd), lambda b, h: (layer, 0, 0)),
        ],
        out_specs=pl.BlockSpec((seq, hd), lambda b, h: (b, h)),
        out_shape=jax.ShapeDtypeStruct((n, n_heads * hd), BF16),
        compiler_params=_params(("parallel", "parallel")),
        name="diff_attn",
    )(lam_vecs, qk, qk, v, norm_g)


def _chunk_scan(x, pos, axis, reverse=False):
    size = x.shape[axis]
    s = 1
    while s < CHUNK:
        if reverse:
            x = x + jnp.where(pos < CHUNK - s, pltpu.roll(x, size - s, axis), 0.0)
        else:
            x = x + jnp.where(pos >= s, pltpu.roll(x, s, axis), 0.0)
        s *= 2
    return x


def _interleave(lead, fill):
    done = 0
    for idx, item in enumerate(lead):
        item()
        upto = (idx + 1) * len(fill) // len(lead)
        for f in fill[done:upto]:
            f()
        done = upto


def _gdn_kernel(gq_ref, gk_ref, gv_ref, gz_ref, cw_ref, bdc_ref, bdr_ref,
                alc_ref, dtc_ref, alr_ref, dtr_ref, ng_ref, o_ref,
                hist_ref, state_ref, qn_ref, kn_ref, vc_ref, u_ref, wq_ref, akt_ref, cd_ref,
                *, nh, group, nblk):
    rows_blk, width = gq_ref.shape
    n_super = rows_blk // SUPER
    n_chunk = rows_blk // CHUNK
    s_idx = pl.program_id(0)
    wslot = s_idx % 2
    rslot = 1 - wslot

    @pl.when(s_idx == 0)
    def _first():
        hist_ref[...] = jnp.zeros_like(hist_ref)
        state_ref[...] = jnp.zeros_like(state_ref)
        u_ref[1] = jnp.zeros(u_ref.shape[1:], F32)
        wq_ref[1] = jnp.zeros(wq_ref.shape[1:], BF16)
        akt_ref[1] = jnp.zeros(akt_ref.shape[1:], BF16)
        cd_ref[1] = jnp.zeros(cd_ref.shape[1:], F32)

    rows8 = lax.broadcasted_iota(jnp.int32, (SUBLANES, 1), 0)
    seq_start = (s_idx % nblk) == 0

    def conv_silu(u_ref_, idx, dst_ref):
        w = cw_ref[:, idx * width:(idx + 1) * width]
        hist = jnp.where(seq_start, 0.0, hist_ref[idx])
        head = u_ref_[0:SUBLANES, :]
        acc = u_ref_[SUBLANES:rows_blk, :] * w[CONV_K - 1:CONV_K]
        top = head * w[CONV_K - 1:CONV_K]
        for j in range(CONV_K - 1):
            sh = CONV_K - 1 - j
            acc = acc + u_ref_[SUBLANES - sh:rows_blk - sh, :] * w[j:j + 1]
            top = top + jnp.where(rows8 < sh, pltpu.roll(hist, sh, 0), pltpu.roll(head, sh, 0)) * w[j:j + 1]
        hist_ref[idx] = u_ref_[rows_blk - SUBLANES:rows_blk, :]
        dst_ref[0:SUBLANES, :] = _silu(top)
        dst_ref[SUBLANES:rows_blk, :] = _silu(acc)

    def l2_norm_heads():
        for h in range(nh):
            sl = slice(h * DN_HEAD_DIM, (h + 1) * DN_HEAD_DIM)
            qh, kh = qn_ref[:, sl], kn_ref[:, sl]
            qn_ref[:, sl] = qh * (lax.rsqrt(jnp.sum(qh * qh, axis=-1, keepdims=True) + L2_EPS) * DN_HEAD_DIM ** -0.5)
            kn_ref[:, sl] = kh * lax.rsqrt(jnp.sum(kh * kh, axis=-1, keepdims=True) + L2_EPS)

    gates = {}

    def gate_scans():
        rows = lax.broadcasted_iota(jnp.int32, (rows_blk, 1), 0)
        xg = bdc_ref[...]
        gates["beta"] = jax.nn.sigmoid(xg)
        g_raw = -jnp.exp(alc_ref[...]) * jax.nn.softplus(xg + dtc_ref[...])
        g_cum = _chunk_scan(g_raw, rows % CHUNK, 0)
        gates["g"] = g_cum
        eg = jnp.exp(g_cum)
        gates["eg"] = eg
        gates["ekd"] = jnp.exp(_chunk_scan(g_raw, rows % CHUNK, 0, reverse=True) - g_raw)
        cols = lax.broadcasted_iota(jnp.int32, (1, rows_blk), 1)
        gr_raw = -jnp.exp(alr_ref[...]) * jax.nn.softplus(bdr_ref[...] + dtr_ref[...])
        gates["g_row"] = _chunk_scan(gr_raw, cols % CHUNK, 1)
        for ci in range(n_chunk):
            cd_ref[wslot, ci * SUBLANES:(ci + 1) * SUBLANES, :] = eg[(ci + 1) * CHUNK - SUBLANES:(ci + 1) * CHUNK]

    ii = lax.broadcasted_iota(jnp.int32, (SUPER, SUPER), 0)
    jj = lax.broadcasted_iota(jnp.int32, (SUPER, SUPER), 1)
    same = (ii // CHUNK) == (jj // CHUNK)
    incl = same & (ii >= jj)
    strict = same & (ii > jj)
    eye = (ii == jj).astype(F32)

    def lower_left_mask(s):
        return ((ii // (2 * s)) == (jj // (2 * s))) & ((ii // s) % 2 == 1) & ((jj // s) % 2 == 0)

    lower_left = {}
    s = 1
    while s < CHUNK:
        lower_left[s] = lower_left_mask(s)
        s *= 2

    def wy_items(grp):
        st = {}

        def setup():
            st["lmat"], st["t"], st["rhs"] = [], [], []
            for m, h in grp:
                rs = slice(m * SUPER, (m + 1) * SUPER)
                sl = slice(h * DN_HEAD_DIM, (h + 1) * DN_HEAD_DIM)
                gl = nh + h
                qn, kn, v = qn_ref[rs, sl], kn_ref[rs, sl], vc_ref[rs, sl]
                beta, eg, ekd = gates["beta"][rs, h:h + 1], gates["eg"][rs, gl:gl + 1], gates["ekd"][rs, gl:gl + 1]
                decay = jnp.where(incl, jnp.exp(gates["g"][rs, gl:gl + 1] - gates["g_row"][gl:gl + 1, rs]), 0.0)
                kb = kn * beta
                knb = kn.astype(BF16)
                kk = lax.dot_general(kb.astype(BF16), knb, _NT, preferred_element_type=F32)
                qk = lax.dot_general(qn.astype(BF16), knb, _NT, preferred_element_type=F32)
                lmat = jnp.where(strict, kk * decay, 0.0)
                st["lmat"].append(lmat)
                st["t"].append(eye - jnp.where(lower_left[1], lmat, 0.0))
                st["rhs"].append(jnp.concatenate([v * beta, kb * eg], axis=1).astype(BF16))
                qd = (qn * eg).astype(BF16)
                base = 2 * SUPER * m
                for c2 in range(2):
                    wq_ref[wslot, base + c2 * SUPER + CHUNK:base + (c2 + 1) * SUPER, sl] = (
                        qd[c2 * CHUNK:(c2 + 1) * CHUNK])
                akt_ref[wslot, base:base + SUPER, sl] = (qk * decay).astype(BF16)
                akt_ref[wslot, base + SUPER:base + 2 * SUPER, sl] = (kn * ekd).T.astype(BF16)

        def level(s):
            def run():
                ys = [jnp.dot(jnp.where(lower_left[s], lmat, 0.0).astype(BF16), t.astype(BF16),
                              preferred_element_type=F32) for lmat, t in zip(st["lmat"], st["t"])]
                st["t"] = [t - jnp.dot(t.astype(BF16), y.astype(BF16), preferred_element_type=F32)
                           for t, y in zip(st["t"], ys)]
            return run

        def solve():
            for (m, h), t, rhs in zip(grp, st["t"], st["rhs"]):
                rs = slice(m * SUPER, (m + 1) * SUPER)
                sl = slice(h * DN_HEAD_DIM, (h + 1) * DN_HEAD_DIM)
                sol = jnp.dot(t.astype(BF16), rhs, preferred_element_type=F32)
                u_ref[wslot, rs, sl] = sol[:, :DN_HEAD_DIM]
                w = sol[:, DN_HEAD_DIM:].astype(BF16)
                base = 2 * SUPER * m
                for c2 in range(2):
                    wq_ref[wslot, base + c2 * SUPER:base + c2 * SUPER + CHUNK, sl] = w[c2 * CHUNK:(c2 + 1) * CHUNK]

        levels = []
        s = 2
        while s < CHUNK:
            levels.append(level(s))
            s *= 2
        return [setup] + levels + [solve]

    prep_items = [
        lambda: conv_silu(gq_ref, 0, qn_ref),
        lambda: conv_silu(gk_ref, 1, kn_ref),
        lambda: conv_silu(gv_ref, 2, vc_ref),
        l2_norm_heads,
        gate_scans,
    ]
    chains = [(m, h) for m in range(n_super) for h in range(nh)]
    for g0 in range(0, len(chains), group):
        prep_items += wy_items(chains[g0:g0 + group])

    zeros_half = jnp.zeros((CHUNK, DN_HEAD_DIM), BF16)
    keep_state = jnp.where(((s_idx + nblk - 1) % nblk) == 0, 0.0, 1.0)
    states = [state_ref[h] * keep_state for h in range(nh)]

    def chunk_item(ci):
        def run():
            m, c2 = divmod(ci, 2)
            base = 2 * SUPER * m
            r0 = ci * CHUNK
            cd_row = cd_ref[rslot, ci * SUBLANES + SUBLANES - 1:(ci + 1) * SUBLANES, :]
            for h in range(nh):
                sl = slice(h * DN_HEAD_DIM, (h + 1) * DN_HEAD_DIM)
                wq = wq_ref[rslot, base + c2 * SUPER:base + (c2 + 1) * SUPER, sl]
                res1 = jnp.dot(wq, states[h].astype(BF16), preferred_element_type=F32)
                vb = (u_ref[rslot, r0:r0 + CHUNK, sl] - res1[:CHUNK]).astype(BF16)
                rhs = jnp.concatenate([vb, zeros_half] if c2 == 0 else [zeros_half, vb], axis=0)
                res2 = jnp.dot(akt_ref[rslot, base:base + 2 * SUPER, sl], rhs, preferred_element_type=F32)
                states[h] = states[h] * cd_row[:, nh + h:nh + h + 1] + res2[SUPER:]
                o = res1[CHUNK:] + res2[c2 * CHUNK:(c2 + 1) * CHUNK]
                ms = jnp.mean(o * o, axis=-1, keepdims=True)
                o_ref[r0:r0 + CHUNK, sl] = (o * lax.rsqrt(ms + GATED_NORM_EPS) * ng_ref[...]
                                            * _silu(gz_ref[r0:r0 + CHUNK, sl])).astype(o_ref.dtype)
        return run

    _interleave([chunk_item(ci) for ci in range(n_chunk)], prep_items)

    for h in range(nh):
        state_ref[h] = states[h]


def _gdn(proj, conv_w, bdc, bdr, alc, dtc, alr, dtr, norm_g, layer, *, batch, seq, nh, rows_blk):
    n = proj.shape[0]
    width = nh * DN_HEAD_DIM
    nblk = seq // rows_blk
    total = batch * nblk
    rows_r = bdr.shape[0]
    nxt = lambda s: jnp.minimum(s, total - 1)
    cur = lambda s: jnp.maximum(s - 1, 0)

    lane_vec = pl.BlockSpec((None, 1, LANES), lambda s: (layer, 0, 0))
    row_vec = pl.BlockSpec((None, rows_r, 1), lambda s: (layer, 0, 0))
    return pl.pallas_call(
        functools.partial(_gdn_kernel, nh=nh, group=GDN_CHAIN_GROUP, nblk=nblk),
        grid=(total + 1,),
        in_specs=[
            pl.BlockSpec((rows_blk, width), lambda s: (nxt(s), 0)),
            pl.BlockSpec((rows_blk, width), lambda s: (nxt(s), 1)),
            pl.BlockSpec((rows_blk, width), lambda s: (nxt(s), 2)),
            pl.BlockSpec((rows_blk, width), lambda s: (cur(s), 3)),
            pl.BlockSpec((None, CONV_K, 3 * width), lambda s: (layer, 0, 0)),
            pl.BlockSpec((rows_blk, LANES), lambda s: (nxt(s), 0)),
            pl.BlockSpec((rows_r, rows_blk), lambda s: (0, nxt(s))),
            lane_vec, lane_vec, row_vec, row_vec,
            pl.BlockSpec((None, 1, DN_HEAD_DIM), lambda s: (layer, 0, 0)),
        ],
        out_specs=pl.BlockSpec((rows_blk, width), lambda s: (cur(s), 0)),
        out_shape=jax.ShapeDtypeStruct((n, width), BF16),
        scratch_shapes=[
            pltpu.VMEM((3, SUBLANES, width), F32),
            pltpu.VMEM((nh, DN_HEAD_DIM, DN_HEAD_DIM), F32),
            pltpu.VMEM((rows_blk, width), F32),
            pltpu.VMEM((rows_blk, width), F32),
            pltpu.VMEM((rows_blk, width), F32),
            pltpu.VMEM((2, rows_blk, width), F32),
            pltpu.VMEM((2, 2 * rows_blk, width), BF16),
            pltpu.VMEM((2, 2 * rows_blk, width), BF16),
            pltpu.VMEM((2, SUBLANES * (rows_blk // CHUNK), LANES), F32),
        ],
        compiler_params=_params(("arbitrary",)),
        name="gdn",
    )(proj, proj, proj, proj, conv_w, bdc, bdr, alc, dtc, alr, dtr, norm_g)


def _outproj_ln_kernel(x_ref, od_ref, on_ref, w1_ref, w2_ref, g_ref, b_ref, y_ref, *, alpha):
    half = x_ref.shape[0] // 2
    for r in range(2):
        rs = slice(r * half, (r + 1) * half)
        mix = (jnp.dot(od_ref[rs, :], w1_ref[...], preferred_element_type=F32)
               + jnp.dot(on_ref[rs, :], w2_ref[...], preferred_element_type=F32))
        y_ref[rs, :] = _layer_norm(alpha * x_ref[rs, :] + mix, g_ref[...], b_ref[...])


def _outproj_ln(x, o_diff, o_dn, w_out, g, b, layer, *, alpha, tm):
    n, d = x.shape
    k1, k2 = o_diff.shape[1], o_dn.shape[1]
    assert k1 == k2
    vec = pl.BlockSpec((None, 1, d), lambda i: (layer, 0, 0))
    return pl.pallas_call(
        functools.partial(_outproj_ln_kernel, alpha=alpha),
        grid=(n // tm,),
        in_specs=[
            pl.BlockSpec((tm, d), lambda i: (i, 0)),
            pl.BlockSpec((tm, k1), lambda i: (i, 0)),
            pl.BlockSpec((tm, k2), lambda i: (i, 0)),
            pl.BlockSpec((None, k1, d), lambda i: (layer, 0, 0)),
            pl.BlockSpec((None, k2, d), lambda i: (layer, 1, 0)),
            vec, vec,
        ],
        out_specs=pl.BlockSpec((tm, d), lambda i: (i, 0)),
        out_shape=jax.ShapeDtypeStruct((n, d), F32),
        compiler_params=_params(("parallel",)),
        name="outproj_ln",
    )(x, o_diff, o_dn, w_out, w_out, g, b)


def _tile(n, pref):
    t = min(n, pref)
    assert n % t == 0, (n, pref)
    return t


def kernel(x, positions, ffn1_w_in, ffn1_w_out, ln1_g, ln1_b, w_in, conv_w, a_log, dt_bias, lam_q1, lam_k1, lam_q2, lam_k2, diff_norm_g, delta_norm_g, w_out, ln2_g, ln2_b, ffn2_w_in, ffn2_w_out, ln3_g, ln3_b):
    batch, seq, d = x.shape
    depth = ffn1_w_in.shape[0]
    d_ff = ffn1_w_out.shape[1]
    n = batch * seq
    diff_width = d // 2
    dn_width = d - diff_width
    n_diff_heads = diff_width // (2 * DIFF_HEAD_DIM)
    nh = dn_width // DN_HEAD_DIM
    qk_cols = 2 * (2 * n_diff_heads * DIFF_HEAD_DIM)
    v_cols = n_diff_heads * 2 * DIFF_HEAD_DIM
    main_cols = qk_cols + v_cols + 4 * dn_width
    assert w_in.shape[2] == main_cols + 2 * nh and 2 * nh <= LANES
    assert seq % SUPER == 0
    alpha = (2 * depth) ** 0.25

    tm = _tile(n, 512)
    tm_proj = _tile(n, 1024)
    tf = _tile(d_ff, 512)
    tn = _tile(v_cols, 1024)
    tq = _tile(seq, 256)
    gdn_rows = _tile(seq, 512)

    ffn1_in, ffn1_out = ffn1_w_in.astype(BF16), (0.5 * ffn1_w_out).astype(BF16)
    ffn2_in, ffn2_out = ffn2_w_in.astype(BF16), (0.5 * ffn2_w_out).astype(BF16)
    w_main = w_in.astype(BF16)
    w_gate = w_main[:, :, main_cols:]
    w_bd = jnp.pad(w_gate, ((0, 0), (0, 0), (0, LANES - 2 * nh)))
    rows_r = -(-2 * nh // SUBLANES) * SUBLANES
    w_out_b = w_out.astype(BF16)
    pad_lane = lambda v: jnp.pad(v.astype(F32), ((0, 0), (nh, LANES - 2 * nh))).reshape(depth, 1, LANES)
    pad_row = lambda v: jnp.pad(v.astype(F32), ((0, 0), (nh, rows_r - 2 * nh))).reshape(depth, rows_r, 1)
    alc, dtc, alr, dtr = pad_lane(a_log), pad_lane(dt_bias), pad_row(a_log), pad_row(dt_bias)
    lam_vecs = jnp.stack([lam_q1, lam_k1, lam_q2, lam_k2], axis=1).astype(F32)
    vec3 = lambda v: v.reshape(depth, 1, v.shape[-1])
    ln1_g, ln1_b, ln2_g, ln2_b, ln3_g, ln3_b = map(vec3, (ln1_g, ln1_b, ln2_g, ln2_b, ln3_g, ln3_b))
    diff_g, delta_g = vec3(diff_norm_g), vec3(delta_norm_g)

    rope_c, rope_sa, rope_sb = _rope_tables(positions, tm)

    h = x.reshape(n, d)
    for l in range(depth):
        lam_init = 0.8 - 0.6 * math.exp(-0.3 * l)
        h, hb16 = _ffn_ln(h, ffn1_in, ffn1_out, ln1_g, ln1_b, l, alpha=alpha, tm=tm, tf=tf, emit_bf16=True)
        qk, v, dn, bdc, bdr = _proj(hb16, w_main, w_bd, (rope_c, rope_sa, rope_sb), l, rows=rows_r,
                                    qk_cols=qk_cols, v_cols=v_cols, dn_cols=4 * dn_width, tm=tm_proj, tn=tn)
        o_diff = _attn(qk, v, lam_vecs, diff_g, l, batch=batch, seq=seq, n_heads=n_diff_heads, tq=tq,
                       lam_init=lam_init)
        o_dn = _gdn(dn, conv_w, bdc, bdr, alc, dtc, alr, dtr, delta_g, l, batch=batch, seq=seq, nh=nh,
                    rows_blk=gdn_rows)
        h = _outproj_ln(h, o_diff, o_dn, w_out_b, ln2_g, ln2_b, l, alpha=alpha, tm=tm)
        (h,) = _ffn_ln(h, ffn2_in, ffn2_out, ln3_g, ln3_b, l, alpha=alpha, tm=tm, tf=tf, emit_bf16=False)
    return h.reshape(batch, seq, d)
```

```python
import functools
import math

import jax
import jax.numpy as jnp
from jax import lax
from jax.experimental import pallas as pl
from jax.experimental.pallas import tpu as pltpu

F32 = jnp.float32
BF16 = jnp.bfloat16

LANES = 128
SUBLANES = 8
DIFF_HEAD_DIM = 64
DN_HEAD_DIM = 128
CONV_K = 4
CHUNK = 64
SUPER = 2 * CHUNK
GDN_CHAIN_GROUP = 16
ROPE_THETA = 500000.0
ROPE_DIM = DIFF_HEAD_DIM // 4
ROPE_HALF = ROPE_DIM // 2
LN_EPS = 1e-5
SUBLN_EPS = 1e-5
GATED_NORM_EPS = 1e-6
L2_EPS = 1e-6
LOG2_E = math.log2(math.e)
VMEM_LIMIT_BYTES = 56 * 1024 * 1024

_NT = (((1,), (1,)), ((), ()))


def _params(sem):
    return pltpu.CompilerParams(dimension_semantics=sem, vmem_limit_bytes=VMEM_LIMIT_BYTES)


def _layer_norm(y, g, b):
    mu = jnp.mean(y, axis=-1, keepdims=True)
    yc = y - mu
    var = jnp.mean(yc * yc, axis=-1, keepdims=True)
    return yc * lax.rsqrt(var + LN_EPS) * g + b


def _silu(x):
    return x * jax.nn.sigmoid(x)


def _rope_table_kernel(pos_ref, inv_ref, c_ref, sa_ref, sb_ref):
    ang = pos_ref[...].astype(F32) * inv_ref[...]
    lane = lax.broadcasted_iota(jnp.int32, ang.shape, 1) % DIFF_HEAD_DIM
    cos, sin = jnp.cos(ang), jnp.sin(ang)
    c_ref[...] = jnp.where(lane < ROPE_DIM, cos, 1.0)
    sa_ref[...] = jnp.where(lane < ROPE_HALF, -sin, 0.0)
    sb_ref[...] = jnp.where((lane >= ROPE_HALF) & (lane < ROPE_DIM), sin, 0.0)


def _rope_tables(positions, tm):
    n = positions.size
    pos = positions.reshape(n, 1)
    inv_freq = ROPE_THETA ** (-jnp.arange(0, ROPE_DIM, 2, dtype=F32) / ROPE_DIM)
    lane = jnp.arange(LANES) % DIFF_HEAD_DIM
    inv = jnp.where(lane < ROPE_DIM, inv_freq[lane % ROPE_HALF], 0.0).reshape(1, LANES).astype(F32)
    out = jax.ShapeDtypeStruct((n, LANES), F32)
    spec = pl.BlockSpec((tm, LANES), lambda i: (i, 0))
    return pl.pallas_call(
        _rope_table_kernel,
        grid=(n // tm,),
        in_specs=[pl.BlockSpec((tm, 1), lambda i: (i, 0)), pl.BlockSpec((1, LANES), lambda i: (0, 0))],
        out_specs=[spec, spec, spec],
        out_shape=[out, out, out],
        compiler_params=_params(("parallel",)),
        name="rope_tables",
    )(pos, inv)


def _ffn_ln_kernel(x_ref, wg_ref, wu_ref, wo_ref, g_ref, b_ref, *rest, alpha, emit_bf16, n_row, ln_rows):
    if emit_bf16:
        y_ref, yb_ref, xb_ref, acc_ref = rest
    else:
        y_ref, xb_ref, acc_ref = rest
    i, j = pl.program_id(0), pl.program_id(1)
    tm = xb_ref.shape[0]
    slot = i % 2

    @pl.when((i == 0) & (j == 0))
    def _first():
        acc_ref[1] = jnp.zeros(acc_ref.shape[1:], F32)

    def ln_slice():
        r0 = pl.multiple_of(jnp.minimum(j * ln_rows, tm - ln_rows), SUBLANES)
        y = _layer_norm(acc_ref[1 - slot, pl.ds(r0, ln_rows), :], g_ref[...], b_ref[...])
        y_ref[pl.ds(r0, ln_rows), :] = y
        if emit_bf16:
            yb_ref[pl.ds(r0, ln_rows), :] = y.astype(BF16)

    @pl.when((i < n_row) & (j == 0))
    def _init():
        x = x_ref[...]
        xb_ref[...] = x.astype(BF16)
        acc_ref[slot] = alpha * x

    @pl.when(i < n_row)
    def _main():
        ln_slice()
        xb = xb_ref[...]
        hg = jnp.dot(xb, wg_ref[...], preferred_element_type=F32)
        hu = jnp.dot(xb, wu_ref[...], preferred_element_type=F32)
        a = (_silu(hg) * hu).astype(BF16)
        acc_ref[slot] += jnp.dot(a, wo_ref[...], preferred_element_type=F32)

    @pl.when(i == n_row)
    def _tail():
        ln_slice()


def _ffn_ln(x, w_in, w_out_half, g, b, layer, *, alpha, tm, tf, emit_bf16):
    n, d = x.shape
    assert w_in.shape[2:] == (d, tf)
    f = w_out_half.shape[1]
    nf = f // tf
    n_row = n // tm
    ln_rows = -(-tm // (nf * SUBLANES)) * SUBLANES
    assert ln_rows <= tm
    last = n_row - 1
    col = lambda i, j: jnp.where(i < n_row, j, nf - 1)
    prev = pl.BlockSpec((tm, d), lambda i, j: (jnp.maximum(i - 1, 0), 0))
    vec = pl.BlockSpec((None, 1, d), lambda i, j: (layer, 0, 0))
    out_shape = [jax.ShapeDtypeStruct((n, d), F32)]
    out_specs = [prev]
    if emit_bf16:
        out_shape.append(jax.ShapeDtypeStruct((n, d), BF16))
        out_specs.append(prev)
    return pl.pallas_call(
        functools.partial(_ffn_ln_kernel, alpha=alpha, emit_bf16=emit_bf16, n_row=n_row, ln_rows=ln_rows),
        grid=(n_row + 1, nf),
        in_specs=[
            pl.BlockSpec((tm, d), lambda i, j: (jnp.minimum(i, last), 0)),
            pl.BlockSpec((None, None, d, tf), lambda i, j: (layer, col(i, j), 0, 0)),
            pl.BlockSpec((None, None, d, tf), lambda i, j: (layer, col(i, j) + nf, 0, 0)),
            pl.BlockSpec((None, tf, d), lambda i, j: (layer, col(i, j), 0)),
            vec, vec,
        ],
        out_specs=out_specs,
        out_shape=out_shape,
        scratch_shapes=[pltpu.VMEM((tm, d), BF16), pltpu.VMEM((2, tm, d), F32)],
        compiler_params=_params(("arbitrary", "arbitrary")),
        name="ffn_ln",
    )(x, w_in, w_in, w_out_half, g, b)


def _proj_kernel(xb_ref, w_ref, wbd_ref, c_ref, sa_ref, sb_ref,
                 qk_ref, v_ref, dn_ref, bdc_ref, bdr_ref, *, n_q, n_qk, n_v):
    j = pl.program_id(1)
    tm, tn = qk_ref.shape
    half = tm // 2

    @pl.when(j == 0)
    def _gates():
        gates = jnp.dot(xb_ref[...], wbd_ref[...], preferred_element_type=F32)
        bdc_ref[...] = gates
        bdr_ref[...] = gates.T[0:bdr_ref.shape[0], :]

    def tile(store):
        for r in range(2):
            rs = slice(r * half, (r + 1) * half)
            store(jnp.dot(xb_ref[rs, :], w_ref[...], preferred_element_type=F32), rs)

    def rope_store(acc, rs):
        scale = jnp.where(j < n_q, DIFF_HEAD_DIM ** -0.5 * LOG2_E, 1.0)
        c, sa, sb = c_ref[rs, :] * scale, sa_ref[rs, :] * scale, sb_ref[rs, :] * scale
        for t in range(tn // LANES):
            seg = acc[:, t * LANES:(t + 1) * LANES]
            rot = seg * c + pltpu.roll(seg, LANES - ROPE_HALF, 1) * sa + pltpu.roll(seg, ROPE_HALF, 1) * sb
            qk_ref[rs, t * LANES:(t + 1) * LANES] = rot.astype(qk_ref.dtype)

    def v_store(acc, rs):
        v_ref[rs, :] = acc.astype(v_ref.dtype)

    def dn_store(acc, rs):
        dn_ref[rs, :] = acc

    pl.when(j < n_qk)(lambda: tile(rope_store))
    pl.when((j >= n_qk) & (j < n_qk + n_v))(lambda: tile(v_store))
    pl.when(j >= n_qk + n_v)(lambda: tile(dn_store))


def _proj(xb, w_main, w_bd, rope, layer, *, rows, qk_cols, v_cols, dn_cols, tm, tn):
    n, d = xb.shape
    n_qk, n_v, n_dn = qk_cols // tn, v_cols // tn, dn_cols // tn
    c, sa, sb = rope
    tspec = pl.BlockSpec((tm, LANES), lambda i, j: (i, 0))
    return pl.pallas_call(
        functools.partial(_proj_kernel, n_q=n_qk // 2, n_qk=n_qk, n_v=n_v),
        grid=(n // tm, n_qk + n_v + n_dn),
        in_specs=[
            pl.BlockSpec((tm, d), lambda i, j: (i, 0)),
            pl.BlockSpec((None, None, d, tn), lambda i, j: (layer, j, 0, 0)),
            pl.BlockSpec((None, d, LANES), lambda i, j: (layer, 0, 0)),
            tspec, tspec, tspec,
        ],
        out_specs=[
            pl.BlockSpec((tm, tn), lambda i, j: (i, jnp.minimum(j, n_qk - 1))),
            pl.BlockSpec((tm, tn), lambda i, j: (i, jnp.clip(j - n_qk, 0, n_v - 1))),
            pl.BlockSpec((tm, tn), lambda i, j: (i, jnp.clip(j - n_qk - n_v, 0, n_dn - 1))),
            pl.BlockSpec((tm, LANES), lambda i, j: (i, 0)),
            pl.BlockSpec((rows, tm), lambda i, j: (0, i)),
        ],
        out_shape=[
            jax.ShapeDtypeStruct((n, qk_cols), BF16),
            jax.ShapeDtypeStruct((n, v_cols), BF16),
            jax.ShapeDtypeStruct((n, dn_cols), F32),
            jax.ShapeDtypeStruct((n, LANES), F32),
            jax.ShapeDtypeStruct((rows, n), F32),
        ],
        compiler_params=_params(("parallel", "arbitrary")),
        name="proj",
    )(xb, w_main, w_bd, c, sa, sb)


def _attn_kernel(lam_ref, q_ref, k_ref, v_ref, g_ref, o_ref, *, tq, lam_init):
    s_len = q_ref.shape[0]
    lq = lam_ref[...]
    lam = (jnp.exp(jnp.sum(lq[0:1] * lq[1:2], axis=-1, keepdims=True))
           - jnp.exp(jnp.sum(lq[2:3] * lq[3:4], axis=-1, keepdims=True)) + lam_init)
    lane = lax.broadcasted_iota(jnp.int32, (1, LANES), 1)
    map0 = lane < DIFF_HEAD_DIM
    gain = g_ref[...] * (1.0 - lam_init)
    n_blk = s_len // tq

    def scores(i):
        skv = (i + 1) * tq
        q = q_ref[i * tq:(i + 1) * tq, :]
        zero = jnp.zeros_like(q)
        k = k_ref[0:skv, :]
        s0 = lax.dot_general(jnp.where(map0, q, zero), k, _NT, preferred_element_type=F32)
        s1 = lax.dot_general(jnp.where(map0, zero, q), k, _NT, preferred_element_type=F32)
        row = lax.broadcasted_iota(jnp.int32, (tq, skv), 0) + i * tq
        col = lax.broadcasted_iota(jnp.int32, (tq, skv), 1)
        keep = col <= row
        return jnp.where(keep, s0, -jnp.inf), jnp.where(keep, s1, -jnp.inf)

    def probs(s0, s1):
        e0 = jnp.exp2(s0 - jnp.max(s0, axis=-1, keepdims=True))
        e1 = jnp.exp2(s1 - jnp.max(s1, axis=-1, keepdims=True))
        l0 = jnp.sum(e0, axis=-1, keepdims=True)
        l1 = jnp.sum(e1, axis=-1, keepdims=True)
        return (e0 - e1 * (lam * l0 / l1)).astype(BF16), 1.0 / l0

    def emit(i, p_and_r0):
        p, r0 = p_and_r0
        o = jnp.dot(p, v_ref[0:(i + 1) * tq, :], preferred_element_type=F32) * r0
        ms = jnp.mean(o * o, axis=-1, keepdims=True)
        o_ref[i * tq:(i + 1) * tq, :] = (o * lax.rsqrt(ms + SUBLN_EPS) * gain).astype(o_ref.dtype)

    s_next = scores(0)
    for i in range(n_blk):
        s_cur = s_next
        if i + 1 < n_blk:
            s_next = scores(i + 1)
        emit(i, probs(*s_cur))


def _attn(qk, v, lam_vecs, norm_g, layer, *, batch, seq, n_heads, tq, lam_init):
    n = qk.shape[0]
    hd = 2 * DIFF_HEAD_DIM
    return pl.pallas_call(
        functools.partial(_attn_kernel, tq=tq, lam_init=lam_init),
        grid=(batch, n_heads),
        in_specs=[
            pl.BlockSpec((None, 4, DIFF_HEAD_DIM), lambda b, h: (layer, 0, 0)),
            pl.BlockSpec((seq, hd), lambda b, h: (b, h)),
            pl.BlockSpec((seq, hd), lambda b, h: (b, n_heads + h)),
            pl.BlockSpec((seq, hd), lambda b, h: (b, h)),
            pl.BlockSpec((None, 1, hd), lambda b, h: (layer, 0, 0)),
        ],
        out_specs=pl.BlockSpec((seq, hd), lambda b, h: (b, h)),
        out_shape=jax.ShapeDtypeStruct((n, n_heads * hd), BF16),
        compiler_params=_params(("parallel", "parallel")),
        name="diff_attn",
    )(lam_vecs, qk, qk, v, norm_g)


def _chunk_scan(x, pos, axis, reverse=False):
    size = x.shape[axis]
    s = 1
    while s < CHUNK:
        if reverse:
            x = x + jnp.where(pos < CHUNK - s, pltpu.roll(x, size - s, axis), 0.0)
        else:
            x = x + jnp.where(pos >= s, pltpu.roll(x, s, axis), 0.0)
        s *= 2
    return x


def _interleave(lead, fill):
    done = 0
    for idx, item in enumerate(lead):
        item()
        upto = (idx + 1) * len(fill) // len(lead)
        for f in fill[done:upto]:
            f()
        done = upto


def _gdn_kernel(gq_ref, gk_ref, gv_ref, gz_ref, cw_ref, bdc_ref, bdr_ref,
                alc_ref, dtc_ref, alr_ref, dtr_ref, ng_ref, o_ref,
                hist_ref, state_ref, qn_ref, kn_ref, vc_ref, u_ref, wq_ref, akt_ref, cd_ref,
                *, nh, group, nblk):
    rows_blk, width = gq_ref.shape
    n_super = rows_blk // SUPER
    n_chunk = rows_blk // CHUNK
    s_idx = pl.program_id(0)
    wslot = s_idx % 2
    rslot = 1 - wslot

    @pl.when(s_idx == 0)
    def _first():
        hist_ref[...] = jnp.zeros_like(hist_ref)
        state_ref[...] = jnp.zeros_like(state_ref)
        u_ref[1] = jnp.zeros(u_ref.shape[1:], F32)
        wq_ref[1] = jnp.zeros(wq_ref.shape[1:], BF16)
        akt_ref[1] = jnp.zeros(akt_ref.shape[1:], BF16)
        cd_ref[1] = jnp.zeros(cd_ref.shape[1:], F32)

    rows8 = lax.broadcasted_iota(jnp.int32, (SUBLANES, 1), 0)
    seq_start = (s_idx % nblk) == 0

    def conv_silu(u_ref_, idx, dst_ref):
        w = cw_ref[:, idx * width:(idx + 1) * width]
        hist = jnp.where(seq_start, 0.0, hist_ref[idx])
        head = u_ref_[0:SUBLANES, :]
        acc = u_ref_[SUBLANES:rows_blk, :] * w[CONV_K - 1:CONV_K]
        top = head * w[CONV_K - 1:CONV_K]
        for j in range(CONV_K - 1):
            sh = CONV_K - 1 - j
            acc = acc + u_ref_[SUBLANES - sh:rows_blk - sh, :] * w[j:j + 1]
            top = top + jnp.where(rows8 < sh, pltpu.roll(hist, sh, 0), pltpu.roll(head, sh, 0)) * w[j:j + 1]
        hist_ref[idx] = u_ref_[rows_blk - SUBLANES:rows_blk, :]
        dst_ref[0:SUBLANES, :] = _silu(top)
        dst_ref[SUBLANES:rows_blk, :] = _silu(acc)

    def l2_norm_heads():
        for h in range(nh):
            sl = slice(h * DN_HEAD_DIM, (h + 1) * DN_HEAD_DIM)
            qh, kh = qn_ref[:, sl], kn_ref[:, sl]
            qn_ref[:, sl] = qh * (lax.rsqrt(jnp.sum(qh * qh, axis=-1, keepdims=True) + L2_EPS) * DN_HEAD_DIM ** -0.5)
            kn_ref[:, sl] = kh * lax.rsqrt(jnp.sum(kh * kh, axis=-1, keepdims=True) + L2_EPS)

    gates = {}

    def gate_scans():
        rows = lax.broadcasted_iota(jnp.int32, (rows_blk, 1), 0)
        xg = bdc_ref[...]
        gates["beta"] = jax.nn.sigmoid(xg)
        g_raw = -jnp.exp(alc_ref[...]) * jax.nn.softplus(xg + dtc_ref[...])
        g_cum = _chunk_scan(g_raw, rows % CHUNK, 0)
        gates["g"] = g_cum
        eg = jnp.exp(g_cum)
        gates["eg"] = eg
        gates["ekd"] = jnp.exp(_chunk_scan(g_raw, rows % CHUNK, 0, reverse=True) - g_raw)
        cols = lax.broadcasted_iota(jnp.int32, (1, rows_blk), 1)
        gr_raw = -jnp.exp(alr_ref[...]) * jax.nn.softplus(bdr_ref[...] + dtr_ref[...])
        gates["g_row"] = _chunk_scan(gr_raw, cols % CHUNK, 1)
        for ci in range(n_chunk):
            cd_ref[wslot, ci * SUBLANES:(ci + 1) * SUBLANES, :] = eg[(ci + 1) * CHUNK - SUBLANES:(ci + 1) * CHUNK]

    ii = lax.broadcasted_iota(jnp.int32, (SUPER, SUPER), 0)
    jj = lax.broadcasted_iota(jnp.int32, (SUPER, SUPER), 1)
    same = (ii // CHUNK) == (jj // CHUNK)
    incl = same & (ii >= jj)
    strict = same & (ii > jj)
    eye = (ii == jj).astype(F32)

    def lower_left_mask(s):
        return ((ii // (2 * s)) == (jj // (2 * s))) & ((ii // s) % 2 == 1) & ((jj // s) % 2 == 0)

    lower_left = {}
    s = 1
    while s < CHUNK:
        lower_left[s] = lower_left_mask(s)
        s *= 2

    def wy_items(grp):
        st = {}

        def setup():
            st["lmat"], st["t"], st["rhs"] = [], [], []
            for m, h in grp:
                rs = slice(m * SUPER, (m + 1) * SUPER)
                sl = slice(h * DN_HEAD_DIM, (h + 1) * DN_HEAD_DIM)
                gl = nh + h
                qn, kn, v = qn_ref[rs, sl], kn_ref[rs, sl], vc_ref[rs, sl]
                beta, eg, ekd = gates["beta"][rs, h:h + 1], gates["eg"][rs, gl:gl + 1], gates["ekd"][rs, gl:gl + 1]
                decay = jnp.where(incl, jnp.exp(gates["g"][rs, gl:gl + 1] - gates["g_row"][gl:gl + 1, rs]), 0.0)
                kb = kn * beta
                knb = kn.astype(BF16)
                kk = lax.dot_general(kb.astype(BF16), knb, _NT, preferred_element_type=F32)
                qk = lax.dot_general(qn.astype(BF16), knb, _NT, preferred_element_type=F32)
                lmat = jnp.where(strict, kk * decay, 0.0)
                st["lmat"].append(lmat)
                st["t"].append(eye - jnp.where(lower_left[1], lmat, 0.0))
                st["rhs"].append(jnp.concatenate([v * beta, kb * eg], axis=1).astype(BF16))
                qd = (qn * eg).astype(BF16)
                base = 2 * SUPER * m
                for c2 in range(2):
                    wq_ref[wslot, base + c2 * SUPER + CHUNK:base + (c2 + 1) * SUPER, sl] = (
                        qd[c2 * CHUNK:(c2 + 1) * CHUNK])
                akt_ref[wslot, base:base + SUPER, sl] = (qk * decay).astype(BF16)
                akt_ref[wslot, base + SUPER:base + 2 * SUPER, sl] = (kn * ekd).T.astype(BF16)

        def level(s):
            def run():
                ys = [jnp.dot(jnp.where(lower_left[s], lmat, 0.0).astype(BF16), t.astype(BF16),
                              preferred_element_type=F32) for lmat, t in zip(st["lmat"], st["t"])]
                st["t"] = [t - jnp.dot(t.astype(BF16), y.astype(BF16), preferred_element_type=F32)
                           for t, y in zip(st["t"], ys)]
            return run

        def solve():
            for (m, h), t, rhs in zip(grp, st["t"], st["rhs"]):
                rs = slice(m * SUPER, (m + 1) * SUPER)
                sl = slice(h * DN_HEAD_DIM, (h + 1) * DN_HEAD_DIM)
                sol = jnp.dot(t.astype(BF16), rhs, preferred_element_type=F32)
                u_ref[wslot, rs, sl] = sol[:, :DN_HEAD_DIM]
                w = sol[:, DN_HEAD_DIM:].astype(BF16)
                base = 2 * SUPER * m
                for c2 in range(2):
                    wq_ref[wslot, base + c2 * SUPER:base + c2 * SUPER + CHUNK, sl] = w[c2 * CHUNK:(c2 + 1) * CHUNK]

        levels = []
        s = 2
        while s < CHUNK:
            levels.append(level(s))
            s *= 2
        return [setup] + levels + [solve]

    prep_items = [
        lambda: conv_silu(gq_ref, 0, qn_ref),
        lambda: conv_silu(gk_ref, 1, kn_ref),
        lambda: conv_silu(gv_ref, 2, vc_ref),
        l2_norm_heads,
        gate_scans,
    ]
    chains = [(m, h) for m in range(n_super) for h in range(nh)]
    for g0 in range(0, len(chains), group):
        prep_items += wy_items(chains[g0:g0 + group])

    zeros_half = jnp.zeros((CHUNK, DN_HEAD_DIM), BF16)
    keep_state = jnp.where(((s_idx + nblk - 1) % nblk) == 0, 0.0, 1.0)
    states = [state_ref[h] * keep_state for h in range(nh)]

    def chunk_item(ci):
        def run():
            m, c2 = divmod(ci, 2)
            base = 2 * SUPER * m
            r0 = ci * CHUNK
            cd_row = cd_ref[rslot, ci * SUBLANES + SUBLANES - 1:(ci + 1) * SUBLANES, :]
            for h in range(nh):
                sl = slice(h * DN_HEAD_DIM, (h + 1) * DN_HEAD_DIM)
                wq = wq_ref[rslot, base + c2 * SUPER:base + (c2 + 1) * SUPER, sl]
                res1 = jnp.dot(wq, states[h].astype(BF16), preferred_element_type=F32)
                vb = (u_ref[rslot, r0:r0 + CHUNK, sl] - res1[:CHUNK]).astype(BF16)
                rhs = jnp.concatenate([vb, zeros_half] if c2 == 0 else [zeros_half, vb], axis=0)
                res2 = jnp.dot(akt_ref[rslot, base:base + 2 * SUPER, sl], rhs, preferred_element_type=F32)
                states[h] = states[h] * cd_row[:, nh + h:nh + h + 1] + res2[SUPER:]
                o = res1[CHUNK:] + res2[c2 * CHUNK:(c2 + 1) * CHUNK]
                ms = jnp.mean(o * o, axis=-1, keepdims=True)
                o_ref[r0:r0 + CHUNK, sl] = (o * lax.rsqrt(ms + GATED_NORM_EPS) * ng_ref[...]
                                            * _silu(gz_ref[r0:r0 + CHUNK, sl])).astype(o_ref.dtype)
        return run

    _interleave([chunk_item(ci) for ci in range(n_chunk)], prep_items)

    for h in range(nh):
        state_ref[h] = states[h]


def _gdn(proj, conv_w, bdc, bdr, alc, dtc, alr, dtr, norm_g, layer, *, batch, seq, nh, rows_blk):
    n = proj.shape[0]
    width = nh * DN_HEAD_DIM
    nblk = seq // rows_blk
    total = batch * nblk
    rows_r = bdr.shape[0]
    nxt = lambda s: jnp.minimum(s, total - 1)
    cur = lambda s: jnp.maximum(s - 1, 0)

    lane_vec = pl.BlockSpec((None, 1, LANES), lambda s: (layer, 0, 0))
    row_vec = pl.BlockSpec((None, rows_r, 1), lambda s: (layer, 0, 0))
    return pl.pallas_call(
        functools.partial(_gdn_kernel, nh=nh, group=GDN_CHAIN_GROUP, nblk=nblk),
        grid=(total + 1,),
        in_specs=[
            pl.BlockSpec((rows_blk, width), lambda s: (nxt(s), 0)),
            pl.BlockSpec((rows_blk, width), lambda s: (nxt(s), 1)),
            pl.BlockSpec((rows_blk, width), lambda s: (nxt(s), 2)),
            pl.BlockSpec((rows_blk, width), lambda s: (cur(s), 3)),
            pl.BlockSpec((None, CONV_K, 3 * width), lambda s: (layer, 0, 0)),
            pl.BlockSpec((rows_blk, LANES), lambda s: (nxt(s), 0)),
            pl.BlockSpec((rows_r, rows_blk), lambda s: (0, nxt(s))),
            lane_vec, lane_vec, row_vec, row_vec,
            pl.BlockSpec((None, 1, DN_HEAD_DIM), lambda s: (layer, 0, 0)),
        ],
        out_specs=pl.BlockSpec((rows_blk, width), lambda s: (cur(s), 0)),
        out_shape=jax.ShapeDtypeStruct((n, width), BF16),
        scratch_shapes=[
            pltpu.VMEM((3, SUBLANES, width), F32),
            pltpu.VMEM((nh, DN_HEAD_DIM, DN_HEAD_DIM), F32),
            pltpu.VMEM((rows_blk, width), F32),
            pltpu.VMEM((rows_blk, width), F32),
            pltpu.VMEM((rows_blk, width), F32),
            pltpu.VMEM((2, rows_blk, width), F32),
            pltpu.VMEM((2, 2 * rows_blk, width), BF16),
            pltpu.VMEM((2, 2 * rows_blk, width), BF16),
            pltpu.VMEM((2, SUBLANES * (rows_blk // CHUNK), LANES), F32),
        ],
        compiler_params=_params(("arbitrary",)),
        name="gdn",
    )(proj, proj, proj, proj, conv_w, bdc, bdr, alc, dtc, alr, dtr, norm_g)


def _outproj_ln_kernel(x_ref, od_ref, on_ref, w1_ref, w2_ref, g_ref, b_ref, y_ref, *, alpha):
    half = x_ref.shape[0] // 2
    for r in range(2):
        rs = slice(r * half, (r + 1) * half)
        mix = (jnp.dot(od_ref[rs, :], w1_ref[...], preferred_element_type=F32)
               + jnp.dot(on_ref[rs, :], w2_ref[...], preferred_element_type=F32))
        y_ref[rs, :] = _layer_norm(alpha * x_ref[rs, :] + mix, g_ref[...], b_ref[...])


def _outproj_ln(x, o_diff, o_dn, w_out, g, b, layer, *, alpha, tm):
    n, d = x.shape
    k1, k2 = o_diff.shape[1], o_dn.shape[1]
    assert k1 == k2
    vec = pl.BlockSpec((None, 1, d), lambda i: (layer, 0, 0))
    return pl.pallas_call(
        functools.partial(_outproj_ln_kernel, alpha=alpha),
        grid=(n // tm,),
        in_specs=[
            pl.BlockSpec((tm, d), lambda i: (i, 0)),
            pl.BlockSpec((tm, k1), lambda i: (i, 0)),
            pl.BlockSpec((tm, k2), lambda i: (i, 0)),
            pl.BlockSpec((None, k1, d), lambda i: (layer, 0, 0)),
            pl.BlockSpec((None, k2, d), lambda i: (layer, 1, 0)),
            vec, vec,
        ],
        out_specs=pl.BlockSpec((tm, d), lambda i: (i, 0)),
        out_shape=jax.ShapeDtypeStruct((n, d), F32),
        compiler_params=_params(("parallel",)),
        name="outproj_ln",
    )(x, o_diff, o_dn, w_out, w_out, g, b)


def _tile(n, pref):
    t = min(n, pref)
    assert n % t == 0, (n, pref)
    return t


def kernel(x, positions, ffn1_w_in, ffn1_w_out, ln1_g, ln1_b, w_in, conv_w, a_log, dt_bias, lam_q1, lam_k1, lam_q2, lam_k2, diff_norm_g, delta_norm_g, w_out, ln2_g, ln2_b, ffn2_w_in, ffn2_w_out, ln3_g, ln3_b):
    batch, seq, d = x.shape
    depth = ffn1_w_in.shape[0]
    d_ff = ffn1_w_out.shape[1]
    n = batch * seq
    diff_width = d // 2
    dn_width = d - diff_width
    n_diff_heads = diff_width // (2 * DIFF_HEAD_DIM)
    nh = dn_width // DN_HEAD_DIM
    qk_cols = 2 * (2 * n_diff_heads * DIFF_HEAD_DIM)
    v_cols = n_diff_heads * 2 * DIFF_HEAD_DIM
    main_cols = qk_cols + v_cols + 4 * dn_width
    assert w_in.shape[2] == main_cols + 2 * nh and 2 * nh <= LANES
    assert seq % SUPER == 0
    alpha = (2 * depth) ** 0.25

    tm = _tile(n, 512)
    tm_proj = _tile(n, 1024)
    tf = _tile(d_ff, 512)
    tn = _tile(v_cols, 1024)
    tq = _tile(seq, 256)
    gdn_rows = _tile(seq, 512)

    def col_tiles(w, cols, tile):
        return jnp.swapaxes(w[:, :, :cols].reshape(depth, d, cols // tile, tile), 1, 2).astype(BF16)

    ffn1_in, ffn1_out = col_tiles(ffn1_w_in, 2 * d_ff, tf), (0.5 * ffn1_w_out).astype(BF16)
    ffn2_in, ffn2_out = col_tiles(ffn2_w_in, 2 * d_ff, tf), (0.5 * ffn2_w_out).astype(BF16)
    w_main = col_tiles(w_in, main_cols, tn)
    w_gate = w_in[:, :, main_cols:].astype(BF16)
    w_bd = jnp.pad(w_gate, ((0, 0), (0, 0), (0, LANES - 2 * nh)))
    rows_r = -(-2 * nh // SUBLANES) * SUBLANES
    w_out_b = w_out.astype(BF16)
    pad_lane = lambda v: jnp.pad(v.astype(F32), ((0, 0), (nh, LANES - 2 * nh))).reshape(depth, 1, LANES)
    pad_row = lambda v: jnp.pad(v.astype(F32), ((0, 0), (nh, rows_r - 2 * nh))).reshape(depth, rows_r, 1)
    alc, dtc, alr, dtr = pad_lane(a_log), pad_lane(dt_bias), pad_row(a_log), pad_row(dt_bias)
    lam_vecs = jnp.stack([lam_q1, lam_k1, lam_q2, lam_k2], axis=1).astype(F32)
    vec3 = lambda v: v.reshape(depth, 1, v.shape[-1])
    ln1_g, ln1_b, ln2_g, ln2_b, ln3_g, ln3_b = map(vec3, (ln1_g, ln1_b, ln2_g, ln2_b, ln3_g, ln3_b))
    diff_g, delta_g = vec3(diff_norm_g), vec3(delta_norm_g)

    rope_c, rope_sa, rope_sb = _rope_tables(positions, tm)

    h = x.reshape(n, d)
    for l in range(depth):
        lam_init = 0.8 - 0.6 * math.exp(-0.3 * l)
        h, hb16 = _ffn_ln(h, ffn1_in, ffn1_out, ln1_g, ln1_b, l, alpha=alpha, tm=tm, tf=tf, emit_bf16=True)
        qk, v, dn, bdc, bdr = _proj(hb16, w_main, w_bd, (rope_c, rope_sa, rope_sb), l, rows=rows_r,
                                    qk_cols=qk_cols, v_cols=v_cols, dn_cols=4 * dn_width, tm=tm_proj, tn=tn)
        o_diff = _attn(qk, v, lam_vecs, diff_g, l, batch=batch, seq=seq, n_heads=n_diff_heads, tq=tq,
                       lam_init=lam_init)
        o_dn = _gdn(dn, conv_w, bdc, bdr, alc, dtc, alr, dtr, delta_g, l, batch=batch, seq=seq, nh=nh,
                    rows_blk=gdn_rows)
        h = _outproj_ln(h, o_diff, o_dn, w_out_b, ln2_g, ln2_b, l, alpha=alpha, tm=tm)
        (h,) = _ffn_ln(h, ffn2_in, ffn2_out, ln3_g, ln3_b, l, alpha=alpha, tm=tm, tf=tf, emit_bf16=False)
    return h.reshape(batch, seq, d)
```

```python
import functools
import math

import jax
import jax.numpy as jnp
from jax import lax
from jax.experimental import pallas as pl
from jax.experimental.pallas import tpu as pltpu

F32 = jnp.float32
BF16 = jnp.bfloat16

LANES = 128
SUBLANES = 8
DIFF_HEAD_DIM = 64
DN_HEAD_DIM = 128
CONV_K = 4
CHUNK = 64
SUPER = 2 * CHUNK
GDN_CHAIN_GROUP = 16
ROPE_THETA = 500000.0
ROPE_DIM = DIFF_HEAD_DIM // 4
ROPE_HALF = ROPE_DIM // 2
LN_EPS = 1e-5
SUBLN_EPS = 1e-5
GATED_NORM_EPS = 1e-6
L2_EPS = 1e-6
LOG2_E = math.log2(math.e)
VMEM_LIMIT_BYTES = 56 * 1024 * 1024

_NT = (((1,), (1,)), ((), ()))


def _params(sem):
    return pltpu.CompilerParams(dimension_semantics=sem, vmem_limit_bytes=VMEM_LIMIT_BYTES)


def _layer_norm(y, g, b):
    mu = jnp.mean(y, axis=-1, keepdims=True)
    yc = y - mu
    var = jnp.mean(yc * yc, axis=-1, keepdims=True)
    return yc * lax.rsqrt(var + LN_EPS) * g + b


def _silu(x):
    return x * jax.nn.sigmoid(x)


def _rope_table_kernel(pos_ref, inv_ref, c_ref, sa_ref, sb_ref):
    ang = pos_ref[...].astype(F32) * inv_ref[...]
    lane = lax.broadcasted_iota(jnp.int32, ang.shape, 1) % DIFF_HEAD_DIM
    cos, sin = jnp.cos(ang), jnp.sin(ang)
    c_ref[...] = jnp.where(lane < ROPE_DIM, cos, 1.0)
    sa_ref[...] = jnp.where(lane < ROPE_HALF, -sin, 0.0)
    sb_ref[...] = jnp.where((lane >= ROPE_HALF) & (lane < ROPE_DIM), sin, 0.0)


def _rope_tables(positions, tm):
    n = positions.size
    pos = positions.reshape(n, 1)
    inv_freq = ROPE_THETA ** (-jnp.arange(0, ROPE_DIM, 2, dtype=F32) / ROPE_DIM)
    lane = jnp.arange(LANES) % DIFF_HEAD_DIM
    inv = jnp.where(lane < ROPE_DIM, inv_freq[lane % ROPE_HALF], 0.0).reshape(1, LANES).astype(F32)
    out = jax.ShapeDtypeStruct((n, LANES), F32)
    spec = pl.BlockSpec((tm, LANES), lambda i: (i, 0))
    return pl.pallas_call(
        _rope_table_kernel,
        grid=(n // tm,),
        in_specs=[pl.BlockSpec((tm, 1), lambda i: (i, 0)), pl.BlockSpec((1, LANES), lambda i: (0, 0))],
        out_specs=[spec, spec, spec],
        out_shape=[out, out, out],
        compiler_params=_params(("parallel",)),
        name="rope_tables",
    )(pos, inv)


def _ffn_ln_kernel(x_ref, wg_ref, wu_ref, wo_ref, g_ref, b_ref, *rest, alpha, emit_bf16, n_row, ln_rows):
    if emit_bf16:
        y_ref, yb_ref, xb_ref, acc_ref = rest
    else:
        y_ref, xb_ref, acc_ref = rest
    i, j = pl.program_id(0), pl.program_id(1)
    tm = xb_ref.shape[0]
    slot = i % 2

    @pl.when((i == 0) & (j == 0))
    def _first():
        acc_ref[1] = jnp.zeros(acc_ref.shape[1:], F32)

    def ln_slice():
        r0 = pl.multiple_of(jnp.minimum(j * ln_rows, tm - ln_rows), SUBLANES)
        y = _layer_norm(acc_ref[1 - slot, pl.ds(r0, ln_rows), :], g_ref[...], b_ref[...])
        y_ref[pl.ds(r0, ln_rows), :] = y
        if emit_bf16:
            yb_ref[pl.ds(r0, ln_rows), :] = y.astype(BF16)

    @pl.when((i < n_row) & (j == 0))
    def _cast():
        xb_ref[...] = x_ref[...].astype(BF16)

    def main(first):
        ln_slice()
        xb = xb_ref[...]
        hg = jnp.dot(xb, wg_ref[...], preferred_element_type=F32)
        hu = jnp.dot(xb, wu_ref[...], preferred_element_type=F32)
        a = (_silu(hg) * hu).astype(BF16)
        part = jnp.dot(a, wo_ref[...], preferred_element_type=F32)
        acc_ref[slot] = (alpha * x_ref[...] if first else acc_ref[slot]) + part

    pl.when((i < n_row) & (j == 0))(lambda: main(True))
    pl.when((i < n_row) & (j > 0))(lambda: main(False))

    @pl.when(i == n_row)
    def _tail():
        ln_slice()


def _ffn_ln(x, w_in, w_out_half, g, b, layer, *, alpha, tm, tf, emit_bf16):
    n, d = x.shape
    f = w_out_half.shape[1]
    nf = f // tf
    n_row = n // tm
    ln_rows = -(-tm // (nf * SUBLANES)) * SUBLANES
    assert ln_rows <= tm
    last = n_row - 1
    col = lambda i, j: jnp.where(i < n_row, j, nf - 1)
    prev = pl.BlockSpec((tm, d), lambda i, j: (jnp.maximum(i - 1, 0), 0))
    vec = pl.BlockSpec((None, 1, d), lambda i, j: (layer, 0, 0))
    out_shape = [jax.ShapeDtypeStruct((n, d), F32)]
    out_specs = [prev]
    if emit_bf16:
        out_shape.append(jax.ShapeDtypeStruct((n, d), BF16))
        out_specs.append(prev)
    return pl.pallas_call(
        functools.partial(_ffn_ln_kernel, alpha=alpha, emit_bf16=emit_bf16, n_row=n_row, ln_rows=ln_rows),
        grid=(n_row + 1, nf),
        in_specs=[
            pl.BlockSpec((tm, d), lambda i, j: (jnp.minimum(i, last), 0)),
            pl.BlockSpec((None, d, tf), lambda i, j: (layer, 0, col(i, j))),
            pl.BlockSpec((None, d, tf), lambda i, j: (layer, 0, col(i, j) + nf)),
            pl.BlockSpec((None, tf, d), lambda i, j: (layer, col(i, j), 0)),
            vec, vec,
        ],
        out_specs=out_specs,
        out_shape=out_shape,
        scratch_shapes=[pltpu.VMEM((tm, d), BF16), pltpu.VMEM((2, tm, d), F32)],
        compiler_params=_params(("arbitrary", "arbitrary")),
        name="ffn_ln",
    )(x, w_in, w_in, w_out_half, g, b)


def _proj_kernel(xb_ref, w_ref, wbd_ref, c_ref, sa_ref, sb_ref,
                 qk_ref, v_ref, dn_ref, bdc_ref, bdr_ref, *, n_q, n_qk, n_v):
    j = pl.program_id(1)
    tm, tn = qk_ref.shape
    half = tm // 2

    @pl.when(j == 0)
    def _gates():
        gates = jnp.dot(xb_ref[...], wbd_ref[...], preferred_element_type=F32)
        bdc_ref[...] = gates
        bdr_ref[...] = gates.T[0:bdr_ref.shape[0], :]

    def tile(store):
        for r in range(2):
            rs = slice(r * half, (r + 1) * half)
            store(jnp.dot(xb_ref[rs, :], w_ref[...], preferred_element_type=F32), rs)

    def rope_store(acc, rs):
        scale = jnp.where(j < n_q, DIFF_HEAD_DIM ** -0.5 * LOG2_E, 1.0)
        c, sa, sb = c_ref[rs, :] * scale, sa_ref[rs, :] * scale, sb_ref[rs, :] * scale
        for t in range(tn // LANES):
            seg = acc[:, t * LANES:(t + 1) * LANES]
            rot = seg * c + pltpu.roll(seg, LANES - ROPE_HALF, 1) * sa + pltpu.roll(seg, ROPE_HALF, 1) * sb
            qk_ref[rs, t * LANES:(t + 1) * LANES] = rot.astype(qk_ref.dtype)

    def v_store(acc, rs):
        v_ref[rs, :] = acc.astype(v_ref.dtype)

    def dn_store(acc, rs):
        dn_ref[rs, :] = acc

    pl.when(j < n_qk)(lambda: tile(rope_store))
    pl.when((j >= n_qk) & (j < n_qk + n_v))(lambda: tile(v_store))
    pl.when(j >= n_qk + n_v)(lambda: tile(dn_store))


def _proj(xb, w_main, w_bd, rope, layer, *, rows, qk_cols, v_cols, dn_cols, tm, tn):
    n, d = xb.shape
    n_qk, n_v, n_dn = qk_cols // tn, v_cols // tn, dn_cols // tn
    c, sa, sb = rope
    tspec = pl.BlockSpec((tm, LANES), lambda i, j: (i, 0))
    return pl.pallas_call(
        functools.partial(_proj_kernel, n_q=n_qk // 2, n_qk=n_qk, n_v=n_v),
        grid=(n // tm, n_qk + n_v + n_dn),
        in_specs=[
            pl.BlockSpec((tm, d), lambda i, j: (i, 0)),
            pl.BlockSpec((None, d, tn), lambda i, j: (layer, 0, j)),
            pl.BlockSpec((None, d, LANES), lambda i, j: (layer, 0, 0)),
            tspec, tspec, tspec,
        ],
        out_specs=[
            pl.BlockSpec((tm, tn), lambda i, j: (i, jnp.minimum(j, n_qk - 1))),
            pl.BlockSpec((tm, tn), lambda i, j: (i, jnp.clip(j - n_qk, 0, n_v - 1))),
            pl.BlockSpec((tm, tn), lambda i, j: (i, jnp.clip(j - n_qk - n_v, 0, n_dn - 1))),
            pl.BlockSpec((tm, LANES), lambda i, j: (i, 0)),
            pl.BlockSpec((rows, tm), lambda i, j: (0, i)),
        ],
        out_shape=[
            jax.ShapeDtypeStruct((n, qk_cols), BF16),
            jax.ShapeDtypeStruct((n, v_cols), BF16),
            jax.ShapeDtypeStruct((n, dn_cols), F32),
            jax.ShapeDtypeStruct((n, LANES), F32),
            jax.ShapeDtypeStruct((rows, n), F32),
        ],
        compiler_params=_params(("parallel", "arbitrary")),
        name="proj",
    )(xb, w_main, w_bd, c, sa, sb)


def _attn_kernel(lam_ref, q_ref, k_ref, v_ref, g_ref, o_ref, *, tq, lam_init):
    s_len = q_ref.shape[0]
    lq = lam_ref[...]
    lam = (jnp.exp(jnp.sum(lq[0:1] * lq[1:2], axis=-1, keepdims=True))
           - jnp.exp(jnp.sum(lq[2:3] * lq[3:4], axis=-1, keepdims=True)) + lam_init)
    lane = lax.broadcasted_iota(jnp.int32, (1, LANES), 1)
    map0 = lane < DIFF_HEAD_DIM
    gain = g_ref[...] * (1.0 - lam_init)
    n_blk = s_len // tq

    def scores(i):
        skv = (i + 1) * tq
        q = q_ref[i * tq:(i + 1) * tq, :]
        zero = jnp.zeros_like(q)
        k = k_ref[0:skv, :]
        s0 = lax.dot_general(jnp.where(map0, q, zero), k, _NT, preferred_element_type=F32)
        s1 = lax.dot_general(jnp.where(map0, zero, q), k, _NT, preferred_element_type=F32)
        row = lax.broadcasted_iota(jnp.int32, (tq, skv), 0) + i * tq
        col = lax.broadcasted_iota(jnp.int32, (tq, skv), 1)
        keep = col <= row
        return jnp.where(keep, s0, -jnp.inf), jnp.where(keep, s1, -jnp.inf)

    def probs(s0, s1):
        e0 = jnp.exp2(s0 - jnp.max(s0, axis=-1, keepdims=True))
        e1 = jnp.exp2(s1 - jnp.max(s1, axis=-1, keepdims=True))
        l0 = jnp.sum(e0, axis=-1, keepdims=True)
        l1 = jnp.sum(e1, axis=-1, keepdims=True)
        return (e0 - e1 * (lam * l0 / l1)).astype(BF16), 1.0 / l0

    def emit(i, p_and_r0):
        p, r0 = p_and_r0
        o = jnp.dot(p, v_ref[0:(i + 1) * tq, :], preferred_element_type=F32) * r0
        ms = jnp.mean(o * o, axis=-1, keepdims=True)
        o_ref[i * tq:(i + 1) * tq, :] = (o * lax.rsqrt(ms + SUBLN_EPS) * gain).astype(o_ref.dtype)

    s_next = scores(0)
    for i in range(n_blk):
        s_cur = s_next
        if i + 1 < n_blk:
            s_next = scores(i + 1)
        emit(i, probs(*s_cur))


def _attn(qk, v, lam_vecs, norm_g, layer, *, batch, seq, n_heads, tq, lam_init):
    n = qk.shape[0]
    hd = 2 * DIFF_HEAD_DIM
    return pl.pallas_call(
        functools.partial(_attn_kernel, tq=tq, lam_init=lam_init),
        grid=(batch, n_heads),
        in_specs=[
            pl.BlockSpec((None, 4, DIFF_HEAD_DIM), lambda b, h: (layer, 0, 0)),
            pl.BlockSpec((seq, hd), lambda b, h: (b, h)),
            pl.BlockSpec((seq, hd), lambda b, h: (b, n_heads + h)),
            pl.BlockSpec((seq, hd), lambda b, h: (b, h)),
            pl.BlockSpec((None, 1, hd), lambda b, h: (layer, 0, 0)),
        ],
        out_specs=pl.BlockSpec((seq, hd), lambda b, h: (b, h)),
        out_shape=jax.ShapeDtypeStruct((n, n_heads * hd), BF16),
        compiler_params=_params(("parallel", "parallel")),
        name="diff_attn",
    )(lam_vecs, qk, qk, v, norm_g)


def _chunk_scan(x, pos, axis, reverse=False):
    size = x.shape[axis]
    s = 1
    while s < CHUNK:
        if reverse:
            x = x + jnp.where(pos < CHUNK - s, pltpu.roll(x, size - s, axis), 0.0)
        else:
            x = x + jnp.where(pos >= s, pltpu.roll(x, s, axis), 0.0)
        s *= 2
    return x


def _interleave(lead, fill):
    done = 0
    for idx, item in enumerate(lead):
        item()
        upto = (idx + 1) * len(fill) // len(lead)
        for f in fill[done:upto]:
            f()
        done = upto


def _gdn_kernel(gq_ref, gk_ref, gv_ref, gz_ref, cw_ref, bdc_ref, bdr_ref,
                alc_ref, dtc_ref, alr_ref, dtr_ref, ng_ref, o_ref,
                hist_ref, state_ref, qn_ref, kn_ref, vc_ref, u_ref, wq_ref, akt_ref, cd_ref,
                *, nh, group, nblk):
    rows_blk, width = gq_ref.shape
    n_super = rows_blk // SUPER
    n_chunk = rows_blk // CHUNK
    s_idx = pl.program_id(0)
    wslot = s_idx % 2
    rslot = 1 - wslot

    @pl.when(s_idx == 0)
    def _first():
        hist_ref[...] = jnp.zeros_like(hist_ref)
        state_ref[...] = jnp.zeros_like(state_ref)
        u_ref[1] = jnp.zeros(u_ref.shape[1:], F32)
        wq_ref[1] = jnp.zeros(wq_ref.shape[1:], BF16)
        akt_ref[1] = jnp.zeros(akt_ref.shape[1:], BF16)
        cd_ref[1] = jnp.zeros(cd_ref.shape[1:], F32)

    rows8 = lax.broadcasted_iota(jnp.int32, (SUBLANES, 1), 0)
    seq_start = (s_idx % nblk) == 0

    def conv_silu(u_ref_, idx, dst_ref):
        w = cw_ref[:, idx * width:(idx + 1) * width]
        hist = jnp.where(seq_start, 0.0, hist_ref[idx])
        head = u_ref_[0:SUBLANES, :]
        acc = u_ref_[SUBLANES:rows_blk, :] * w[CONV_K - 1:CONV_K]
        top = head * w[CONV_K - 1:CONV_K]
        for j in range(CONV_K - 1):
            sh = CONV_K - 1 - j
            acc = acc + u_ref_[SUBLANES - sh:rows_blk - sh, :] * w[j:j + 1]
            top = top + jnp.where(rows8 < sh, pltpu.roll(hist, sh, 0), pltpu.roll(head, sh, 0)) * w[j:j + 1]
        hist_ref[idx] = u_ref_[rows_blk - SUBLANES:rows_blk, :]
        dst_ref[0:SUBLANES, :] = _silu(top)
        dst_ref[SUBLANES:rows_blk, :] = _silu(acc)

    def l2_norm_heads():
        for h in range(nh):
            sl = slice(h * DN_HEAD_DIM, (h + 1) * DN_HEAD_DIM)
            qh, kh = qn_ref[:, sl], kn_ref[:, sl]
            qn_ref[:, sl] = qh * (lax.rsqrt(jnp.sum(qh * qh, axis=-1, keepdims=True) + L2_EPS) * DN_HEAD_DIM ** -0.5)
            kn_ref[:, sl] = kh * lax.rsqrt(jnp.sum(kh * kh, axis=-1, keepdims=True) + L2_EPS)

    gates = {}

    def gate_scans():
        rows = lax.broadcasted_iota(jnp.int32, (rows_blk, 1), 0)
        xg = bdc_ref[...]
        gates["beta"] = jax.nn.sigmoid(xg)
        g_raw = -jnp.exp(alc_ref[...]) * jax.nn.softplus(xg + dtc_ref[...])
        g_cum = _chunk_scan(g_raw, rows % CHUNK, 0)
        gates["g"] = g_cum
        eg = jnp.exp(g_cum)
        gates["eg"] = eg
        gates["ekd"] = jnp.exp(_chunk_scan(g_raw, rows % CHUNK, 0, reverse=True) - g_raw)
        cols = lax.broadcasted_iota(jnp.int32, (1, rows_blk), 1)
        gr_raw = -jnp.exp(alr_ref[...]) * jax.nn.softplus(bdr_ref[...] + dtr_ref[...])
        gates["g_row"] = _chunk_scan(gr_raw, cols % CHUNK, 1)
        for ci in range(n_chunk):
            cd_ref[wslot, ci * SUBLANES:(ci + 1) * SUBLANES, :] = eg[(ci + 1) * CHUNK - SUBLANES:(ci + 1) * CHUNK]

    ii = lax.broadcasted_iota(jnp.int32, (SUPER, SUPER), 0)
    jj = lax.broadcasted_iota(jnp.int32, (SUPER, SUPER), 1)
    same = (ii // CHUNK) == (jj // CHUNK)
    incl = same & (ii >= jj)
    strict = same & (ii > jj)
    eye = (ii == jj).astype(F32)

    def lower_left_mask(s):
        return ((ii // (2 * s)) == (jj // (2 * s))) & ((ii // s) % 2 == 1) & ((jj // s) % 2 == 0)

    lower_left = {}
    s = 1
    while s < CHUNK:
        lower_left[s] = lower_left_mask(s)
        s *= 2

    def wy_items(grp):
        st = {}

        def setup():
            st["lmat"], st["t"], st["rhs"] = [], [], []
            for m, h in grp:
                rs = slice(m * SUPER, (m + 1) * SUPER)
                sl = slice(h * DN_HEAD_DIM, (h + 1) * DN_HEAD_DIM)
                gl = nh + h
                qn, kn, v = qn_ref[rs, sl], kn_ref[rs, sl], vc_ref[rs, sl]
                beta, eg, ekd = gates["beta"][rs, h:h + 1], gates["eg"][rs, gl:gl + 1], gates["ekd"][rs, gl:gl + 1]
                decay = jnp.where(incl, jnp.exp(gates["g"][rs, gl:gl + 1] - gates["g_row"][gl:gl + 1, rs]), 0.0)
                kb = kn * beta
                knb = kn.astype(BF16)
                kk = lax.dot_general(kb.astype(BF16), knb, _NT, preferred_element_type=F32)
                qk = lax.dot_general(qn.astype(BF16), knb, _NT, preferred_element_type=F32)
                lmat = jnp.where(strict, kk * decay, 0.0)
                st["lmat"].append(lmat)
                st["t"].append(eye - jnp.where(lower_left[1], lmat, 0.0))
                st["rhs"].append(jnp.concatenate([v * beta, kb * eg], axis=1).astype(BF16))
                qd = (qn * eg).astype(BF16)
                base = 2 * SUPER * m
                for c2 in range(2):
                    wq_ref[wslot, base + c2 * SUPER + CHUNK:base + (c2 + 1) * SUPER, sl] = (
                        qd[c2 * CHUNK:(c2 + 1) * CHUNK])
                akt_ref[wslot, base:base + SUPER, sl] = (qk * decay).astype(BF16)
                akt_ref[wslot, base + SUPER:base + 2 * SUPER, sl] = (kn * ekd).T.astype(BF16)

        def level(s):
            def run():
                ys = [jnp.dot(jnp.where(lower_left[s], lmat, 0.0).astype(BF16), t.astype(BF16),
                              preferred_element_type=F32) for lmat, t in zip(st["lmat"], st["t"])]
                st["t"] = [t - jnp.dot(t.astype(BF16), y.astype(BF16), preferred_element_type=F32)
                           for t, y in zip(st["t"], ys)]
            return run

        def solve():
            for (m, h), t, rhs in zip(grp, st["t"], st["rhs"]):
                rs = slice(m * SUPER, (m + 1) * SUPER)
                sl = slice(h * DN_HEAD_DIM, (h + 1) * DN_HEAD_DIM)
                sol = jnp.dot(t.astype(BF16), rhs, preferred_element_type=F32)
                u_ref[wslot, rs, sl] = sol[:, :DN_HEAD_DIM]
                w = sol[:, DN_HEAD_DIM:].astype(BF16)
                base = 2 * SUPER * m
                for c2 in range(2):
                    wq_ref[wslot, base + c2 * SUPER:base + c2 * SUPER + CHUNK, sl] = w[c2 * CHUNK:(c2 + 1) * CHUNK]

        levels = []
        s = 2
        while s < CHUNK:
            levels.append(level(s))
            s *= 2
        return [setup] + levels + [solve]

    prep_items = [
        lambda: conv_silu(gq_ref, 0, qn_ref),
        lambda: conv_silu(gk_ref, 1, kn_ref),
        lambda: conv_silu(gv_ref, 2, vc_ref),
        l2_norm_heads,
        gate_scans,
    ]
    chains = [(m, h) for m in range(n_super) for h in range(nh)]
    for g0 in range(0, len(chains), group):
        prep_items += wy_items(chains[g0:g0 + group])

    zeros_half = jnp.zeros((CHUNK, DN_HEAD_DIM), BF16)
    keep_state = jnp.where(((s_idx + nblk - 1) % nblk) == 0, 0.0, 1.0)
    states = [state_ref[h] * keep_state for h in range(nh)]

    def chunk_item(ci):
        def run():
            m, c2 = divmod(ci, 2)
            base = 2 * SUPER * m
            r0 = ci * CHUNK
            cd_row = cd_ref[rslot, ci * SUBLANES + SUBLANES - 1:(ci + 1) * SUBLANES, :]
            for h in range(nh):
                sl = slice(h * DN_HEAD_DIM, (h + 1) * DN_HEAD_DIM)
                wq = wq_ref[rslot, base + c2 * SUPER:base + (c2 + 1) * SUPER, sl]
                res1 = jnp.dot(wq, states[h].astype(BF16), preferred_element_type=F32)
                vb = (u_ref[rslot, r0:r0 + CHUNK, sl] - res1[:CHUNK]).astype(BF16)
                rhs = jnp.concatenate([vb, zeros_half] if c2 == 0 else [zeros_half, vb], axis=0)
                res2 = jnp.dot(akt_ref[rslot, base:base + 2 * SUPER, sl], rhs, preferred_element_type=F32)
                states[h] = states[h] * cd_row[:, nh + h:nh + h + 1] + res2[SUPER:]
                o = res1[CHUNK:] + res2[c2 * CHUNK:(c2 + 1) * CHUNK]
                ms = jnp.mean(o * o, axis=-1, keepdims=True)
                o_ref[r0:r0 + CHUNK, sl] = (o * lax.rsqrt(ms + GATED_NORM_EPS) * ng_ref[...]
                                            * _silu(gz_ref[r0:r0 + CHUNK, sl])).astype(o_ref.dtype)
        return run

    _interleave([chunk_item(ci) for ci in range(n_chunk)], prep_items)

    for h in range(nh):
        state_ref[h] = states[h]


def _gdn(proj, conv_w, bdc, bdr, alc, dtc, alr, dtr, norm_g, layer, *, batch, seq, nh, rows_blk):
    n = proj.shape[0]
    width = nh * DN_HEAD_DIM
    nblk = seq // rows_blk
    total = batch * nblk
    rows_r = bdr.shape[0]
    nxt = lambda s: jnp.minimum(s, total - 1)
    cur = lambda s: jnp.maximum(s - 1, 0)

    lane_vec = pl.BlockSpec((None, 1, LANES), lambda s: (layer, 0, 0))
    row_vec = pl.BlockSpec((None, rows_r, 1), lambda s: (layer, 0, 0))
    return pl.pallas_call(
        functools.partial(_gdn_kernel, nh=nh, group=GDN_CHAIN_GROUP, nblk=nblk),
        grid=(total + 1,),
        in_specs=[
            pl.BlockSpec((rows_blk, width), lambda s: (nxt(s), 0)),
            pl.BlockSpec((rows_blk, width), lambda s: (nxt(s), 1)),
            pl.BlockSpec((rows_blk, width), lambda s: (nxt(s), 2)),
            pl.BlockSpec((rows_blk, width), lambda s: (cur(s), 3)),
            pl.BlockSpec((None, CONV_K, 3 * width), lambda s: (layer, 0, 0)),
            pl.BlockSpec((rows_blk, LANES), lambda s: (nxt(s), 0)),
            pl.BlockSpec((rows_r, rows_blk), lambda s: (0, nxt(s))),
            lane_vec, lane_vec, row_vec, row_vec,
            pl.BlockSpec((None, 1, DN_HEAD_DIM), lambda s: (layer, 0, 0)),
        ],
        out_specs=pl.BlockSpec((rows_blk, width), lambda s: (cur(s), 0)),
        out_shape=jax.ShapeDtypeStruct((n, width), BF16),
        scratch_shapes=[
            pltpu.VMEM((3, SUBLANES, width), F32),
            pltpu.VMEM((nh, DN_HEAD_DIM, DN_HEAD_DIM), F32),
            pltpu.VMEM((rows_blk, width), F32),
            pltpu.VMEM((rows_blk, width), F32),
            pltpu.VMEM((rows_blk, width), F32),
            pltpu.VMEM((2, rows_blk, width), F32),
            pltpu.VMEM((2, 2 * rows_blk, width), BF16),
            pltpu.VMEM((2, 2 * rows_blk, width), BF16),
            pltpu.VMEM((2, SUBLANES * (rows_blk // CHUNK), LANES), F32),
        ],
        compiler_params=_params(("arbitrary",)),
        name="gdn",
    )(proj, proj, proj, proj, conv_w, bdc, bdr, alc, dtc, alr, dtr, norm_g)


def _outproj_ln_kernel(x_ref, od_ref, on_ref, w1_ref, w2_ref, g_ref, b_ref, y_ref, *, alpha):
    half = x_ref.shape[0] // 2
    for r in range(2):
        rs = slice(r * half, (r + 1) * half)
        mix = (jnp.dot(od_ref[rs, :], w1_ref[...], preferred_element_type=F32)
               + jnp.dot(on_ref[rs, :], w2_ref[...], preferred_element_type=F32))
        y_ref[rs, :] = _layer_norm(alpha * x_ref[rs, :] + mix, g_ref[...], b_ref[...])


def _outproj_ln(x, o_diff, o_dn, w_out, g, b, layer, *, alpha, tm):
    n, d = x.shape
    k1, k2 = o_diff.shape[1], o_dn.shape[1]
    assert k1 == k2
    vec = pl.BlockSpec((None, 1, d), lambda i: (layer, 0, 0))
    return pl.pallas_call(
        functools.partial(_outproj_ln_kernel, alpha=alpha),
        grid=(n // tm,),
        in_specs=[
            pl.BlockSpec((tm, d), lambda i: (i, 0)),
            pl.BlockSpec((tm, k1), lambda i: (i, 0)),
            pl.BlockSpec((tm, k2), lambda i: (i, 0)),
            pl.BlockSpec((None, k1, d), lambda i: (layer, 0, 0)),
            pl.BlockSpec((None, k2, d), lambda i: (layer, 1, 0)),
            vec, vec,
        ],
        out_specs=pl.BlockSpec((tm, d), lambda i: (i, 0)),
        out_shape=jax.ShapeDtypeStruct((n, d), F32),
        compiler_params=_params(("parallel",)),
        name="outproj_ln",
    )(x, o_diff, o_dn, w_out, w_out, g, b)


def _tile(n, pref):
    t = min(n, pref)
    assert n % t == 0, (n, pref)
    return t


def kernel(x, positions, ffn1_w_in, ffn1_w_out, ln1_g, ln1_b, w_in, conv_w, a_log, dt_bias, lam_q1, lam_k1, lam_q2, lam_k2, diff_norm_g, delta_norm_g, w_out, ln2_g, ln2_b, ffn2_w_in, ffn2_w_out, ln3_g, ln3_b):
    batch, seq, d = x.shape
    depth = ffn1_w_in.shape[0]
    d_ff = ffn1_w_out.shape[1]
    n = batch * seq
    diff_width = d // 2
    dn_width = d - diff_width
    n_diff_heads = diff_width // (2 * DIFF_HEAD_DIM)
    nh = dn_width // DN_HEAD_DIM
    qk_cols = 2 * (2 * n_diff_heads * DIFF_HEAD_DIM)
    v_cols = n_diff_heads * 2 * DIFF_HEAD_DIM
    main_cols = qk_cols + v_cols + 4 * dn_width
    assert w_in.shape[2] == main_cols + 2 * nh and 2 * nh <= LANES
    assert seq % SUPER == 0
    alpha = (2 * depth) ** 0.25

    tm = _tile(n, 512)
    tm_proj = _tile(n, 1024)
    tf = _tile(d_ff, 512)
    tn = _tile(v_cols, 1024)
    tq = _tile(seq, 256)
    gdn_rows = _tile(seq, 512)

    ffn1_in, ffn1_out = ffn1_w_in.astype(BF16), (0.5 * ffn1_w_out).astype(BF16)
    ffn2_in, ffn2_out = ffn2_w_in.astype(BF16), (0.5 * ffn2_w_out).astype(BF16)
    w_main = w_in.astype(BF16)
    w_gate = w_main[:, :, main_cols:]
    w_bd = jnp.pad(w_gate, ((0, 0), (0, 0), (0, LANES - 2 * nh)))
    rows_r = -(-2 * nh // SUBLANES) * SUBLANES
    w_out_b = w_out.astype(BF16)
    pad_lane = lambda v: jnp.pad(v.astype(F32), ((0, 0), (nh, LANES - 2 * nh))).reshape(depth, 1, LANES)
    pad_row = lambda v: jnp.pad(v.astype(F32), ((0, 0), (nh, rows_r - 2 * nh))).reshape(depth, rows_r, 1)
    alc, dtc, alr, dtr = pad_lane(a_log), pad_lane(dt_bias), pad_row(a_log), pad_row(dt_bias)
    lam_vecs = jnp.stack([lam_q1, lam_k1, lam_q2, lam_k2], axis=1).astype(F32)
    vec3 = lambda v: v.reshape(depth, 1, v.shape[-1])
    ln1_g, ln1_b, ln2_g, ln2_b, ln3_g, ln3_b = map(vec3, (ln1_g, ln1_b, ln2_g, ln2_b, ln3_g, ln3_b))
    diff_g, delta_g = vec3(diff_norm_g), vec3(delta_norm_g)

    rope_c, rope_sa, rope_sb = _rope_tables(positions, tm)

    h = x.reshape(n, d)
    for l in range(depth):
        lam_init = 0.8 - 0.6 * math.exp(-0.3 * l)
        h, hb16 = _ffn_ln(h, ffn1_in, ffn1_out, ln1_g, ln1_b, l, alpha=alpha, tm=tm, tf=tf, emit_bf16=True)
        qk, v, dn, bdc, bdr = _proj(hb16, w_main, w_bd, (rope_c, rope_sa, rope_sb), l, rows=rows_r,
                                    qk_cols=qk_cols, v_cols=v_cols, dn_cols=4 * dn_width, tm=tm_proj, tn=tn)
        o_diff = _attn(qk, v, lam_vecs, diff_g, l, batch=batch, seq=seq, n_heads=n_diff_heads, tq=tq,
                       lam_init=lam_init)
        o_dn = _gdn(dn, conv_w, bdc, bdr, alc, dtc, alr, dtr, delta_g, l, batch=batch, seq=seq, nh=nh,
                    rows_blk=gdn_rows)
        h = _outproj_ln(h, o_diff, o_dn, w_out_b, ln2_g, ln2_b, l, alpha=alpha, tm=tm)
        (h,) = _ffn_ln(h, ffn2_in, ffn2_out, ln3_g, ln3_b, l, alpha=alpha, tm=tm, tf=tf, emit_bf16=False)
    return h.reshape(batch, seq, d)
```

```python
import functools
import math

import jax
import jax.numpy as jnp
from jax import lax
from jax.experimental import pallas as pl
from jax.experimental.pallas import tpu as pltpu

F32 = jnp.float32
BF16 = jnp.bfloat16

LANES = 128
SUBLANES = 8
DIFF_HEAD_DIM = 64
DN_HEAD_DIM = 128
CONV_K = 4
CHUNK = 128
SUPER = 128
CHUNKS_PER_TILE = SUPER // CHUNK
GDN_CHAIN_GROUP = 16
ROPE_THETA = 500000.0
ROPE_DIM = DIFF_HEAD_DIM // 4
ROPE_HALF = ROPE_DIM // 2
LN_EPS = 1e-5
SUBLN_EPS = 1e-5
GATED_NORM_EPS = 1e-6
L2_EPS = 1e-6
LOG2_E = math.log2(math.e)
VMEM_LIMIT_BYTES = 56 * 1024 * 1024

_NT = (((1,), (1,)), ((), ()))


def _params(sem):
    return pltpu.CompilerParams(dimension_semantics=sem, vmem_limit_bytes=VMEM_LIMIT_BYTES)


def _layer_norm(y, g, b):
    mu = jnp.mean(y, axis=-1, keepdims=True)
    yc = y - mu
    var = jnp.mean(yc * yc, axis=-1, keepdims=True)
    return yc * lax.rsqrt(var + LN_EPS) * g + b


def _silu(x):
    return x * jax.nn.sigmoid(x)


def _rope_table_kernel(pos_ref, inv_ref, c_ref, sa_ref, sb_ref):
    ang = pos_ref[...].astype(F32) * inv_ref[...]
    lane = lax.broadcasted_iota(jnp.int32, ang.shape, 1) % DIFF_HEAD_DIM
    cos, sin = jnp.cos(ang), jnp.sin(ang)
    c_ref[...] = jnp.where(lane < ROPE_DIM, cos, 1.0)
    sa_ref[...] = jnp.where(lane < ROPE_HALF, -sin, 0.0)
    sb_ref[...] = jnp.where((lane >= ROPE_HALF) & (lane < ROPE_DIM), sin, 0.0)


def _rope_tables(positions, tm):
    n = positions.size
    pos = positions.reshape(n, 1)
    inv_freq = ROPE_THETA ** (-jnp.arange(0, ROPE_DIM, 2, dtype=F32) / ROPE_DIM)
    lane = jnp.arange(LANES) % DIFF_HEAD_DIM
    inv = jnp.where(lane < ROPE_DIM, inv_freq[lane % ROPE_HALF], 0.0).reshape(1, LANES).astype(F32)
    out = jax.ShapeDtypeStruct((n, LANES), F32)
    spec = pl.BlockSpec((tm, LANES), lambda i: (i, 0))
    return pl.pallas_call(
        _rope_table_kernel,
        grid=(n // tm,),
        in_specs=[pl.BlockSpec((tm, 1), lambda i: (i, 0)), pl.BlockSpec((1, LANES), lambda i: (0, 0))],
        out_specs=[spec, spec, spec],
        out_shape=[out, out, out],
        compiler_params=_params(("parallel",)),
        name="rope_tables",
    )(pos, inv)


def _ffn_ln_kernel(x_ref, wg_ref, wu_ref, wo_ref, g_ref, b_ref, *rest, alpha, emit_bf16, n_row, ln_rows):
    if emit_bf16:
        y_ref, yb_ref, xb_ref, acc_ref = rest
    else:
        y_ref, xb_ref, acc_ref = rest
    i, j = pl.program_id(0), pl.program_id(1)
    tm = xb_ref.shape[0]
    slot = i % 2

    @pl.when((i == 0) & (j == 0))
    def _first():
        acc_ref[1] = jnp.zeros(acc_ref.shape[1:], F32)

    def ln_slice():
        r0 = pl.multiple_of(jnp.minimum(j * ln_rows, tm - ln_rows), SUBLANES)
        y = _layer_norm(acc_ref[1 - slot, pl.ds(r0, ln_rows), :], g_ref[...], b_ref[...])
        y_ref[pl.ds(r0, ln_rows), :] = y
        if emit_bf16:
            yb_ref[pl.ds(r0, ln_rows), :] = y.astype(BF16)

    @pl.when((i < n_row) & (j == 0))
    def _cast():
        xb_ref[...] = x_ref[...].astype(BF16)

    def main(first):
        ln_slice()
        xb = xb_ref[...]
        hg = jnp.dot(xb, wg_ref[...], preferred_element_type=F32)
        hu = jnp.dot(xb, wu_ref[...], preferred_element_type=F32)
        a = (_silu(hg) * hu).astype(BF16)
        part = jnp.dot(a, wo_ref[...], preferred_element_type=F32)
        acc_ref[slot] = (alpha * x_ref[...] if first else acc_ref[slot]) + part

    pl.when((i < n_row) & (j == 0))(lambda: main(True))
    pl.when((i < n_row) & (j > 0))(lambda: main(False))

    @pl.when(i == n_row)
    def _tail():
        ln_slice()


def _ffn_ln(x, w_in, w_out_half, g, b, layer, *, alpha, tm, tf, emit_bf16):
    n, d = x.shape
    f = w_out_half.shape[1]
    nf = f // tf
    n_row = n // tm
    ln_rows = -(-tm // (nf * SUBLANES)) * SUBLANES
    assert ln_rows <= tm
    last = n_row - 1
    col = lambda i, j: jnp.where(i < n_row, j, nf - 1)
    prev = pl.BlockSpec((tm, d), lambda i, j: (jnp.maximum(i - 1, 0), 0))
    vec = pl.BlockSpec((None, 1, d), lambda i, j: (layer, 0, 0))
    out_shape = [jax.ShapeDtypeStruct((n, d), F32)]
    out_specs = [prev]
    if emit_bf16:
        out_shape.append(jax.ShapeDtypeStruct((n, d), BF16))
        out_specs.append(prev)
    return pl.pallas_call(
        functools.partial(_ffn_ln_kernel, alpha=alpha, emit_bf16=emit_bf16, n_row=n_row, ln_rows=ln_rows),
        grid=(n_row + 1, nf),
        in_specs=[
            pl.BlockSpec((tm, d), lambda i, j: (jnp.minimum(i, last), 0)),
            pl.BlockSpec((None, d, tf), lambda i, j: (layer, 0, col(i, j))),
            pl.BlockSpec((None, d, tf), lambda i, j: (layer, 0, col(i, j) + nf)),
            pl.BlockSpec((None, tf, d), lambda i, j: (layer, col(i, j), 0)),
            vec, vec,
        ],
        out_specs=out_specs,
        out_shape=out_shape,
        scratch_shapes=[pltpu.VMEM((tm, d), BF16), pltpu.VMEM((2, tm, d), F32)],
        compiler_params=_params(("arbitrary", "arbitrary")),
        name="ffn_ln",
    )(x, w_in, w_in, w_out_half, g, b)


def _proj_kernel(xb_ref, w_ref, wbd_ref, c_ref, sa_ref, sb_ref,
                 qk_ref, v_ref, dn_ref, bdc_ref, bdr_ref, *, n_q, n_qk, n_v):
    j = pl.program_id(1)
    tm, tn = qk_ref.shape
    half = tm // 2

    @pl.when(j == 0)
    def _gates():
        gates = jnp.dot(xb_ref[...], wbd_ref[...], preferred_element_type=F32)
        bdc_ref[...] = gates
        bdr_ref[...] = gates.T[0:bdr_ref.shape[0], :]

    def tile(store):
        for r in range(2):
            rs = slice(r * half, (r + 1) * half)
            store(jnp.dot(xb_ref[rs, :], w_ref[...], preferred_element_type=F32), rs)

    def rope_store(acc, rs):
        scale = jnp.where(j < n_q, DIFF_HEAD_DIM ** -0.5 * LOG2_E, 1.0)
        c, sa, sb = c_ref[rs, :] * scale, sa_ref[rs, :] * scale, sb_ref[rs, :] * scale
        for t in range(tn // LANES):
            seg = acc[:, t * LANES:(t + 1) * LANES]
            rot = seg * c + pltpu.roll(seg, LANES - ROPE_HALF, 1) * sa + pltpu.roll(seg, ROPE_HALF, 1) * sb
            qk_ref[rs, t * LANES:(t + 1) * LANES] = rot.astype(qk_ref.dtype)

    def v_store(acc, rs):
        v_ref[rs, :] = acc.astype(v_ref.dtype)

    def dn_store(acc, rs):
        dn_ref[rs, :] = acc

    pl.when(j < n_qk)(lambda: tile(rope_store))
    pl.when((j >= n_qk) & (j < n_qk + n_v))(lambda: tile(v_store))
    pl.when(j >= n_qk + n_v)(lambda: tile(dn_store))


def _proj(xb, w_main, w_bd, rope, layer, *, rows, qk_cols, v_cols, dn_cols, tm, tn):
    n, d = xb.shape
    n_qk, n_v, n_dn = qk_cols // tn, v_cols // tn, dn_cols // tn
    c, sa, sb = rope
    tspec = pl.BlockSpec((tm, LANES), lambda i, j: (i, 0))
    return pl.pallas_call(
        functools.partial(_proj_kernel, n_q=n_qk // 2, n_qk=n_qk, n_v=n_v),
        grid=(n // tm, n_qk + n_v + n_dn),
        in_specs=[
            pl.BlockSpec((tm, d), lambda i, j: (i, 0)),
            pl.BlockSpec((None, d, tn), lambda i, j: (layer, 0, j)),
            pl.BlockSpec((None, d, LANES), lambda i, j: (layer, 0, 0)),
            tspec, tspec, tspec,
        ],
        out_specs=[
            pl.BlockSpec((tm, tn), lambda i, j: (i, jnp.minimum(j, n_qk - 1))),
            pl.BlockSpec((tm, tn), lambda i, j: (i, jnp.clip(j - n_qk, 0, n_v - 1))),
            pl.BlockSpec((tm, tn), lambda i, j: (i, jnp.clip(j - n_qk - n_v, 0, n_dn - 1))),
            pl.BlockSpec((tm, LANES), lambda i, j: (i, 0)),
            pl.BlockSpec((rows, tm), lambda i, j: (0, i)),
        ],
        out_shape=[
            jax.ShapeDtypeStruct((n, qk_cols), BF16),
            jax.ShapeDtypeStruct((n, v_cols), BF16),
            jax.ShapeDtypeStruct((n, dn_cols), F32),
            jax.ShapeDtypeStruct((n, LANES), F32),
            jax.ShapeDtypeStruct((rows, n), F32),
        ],
        compiler_params=_params(("parallel", "arbitrary")),
        name="proj",
    )(xb, w_main, w_bd, c, sa, sb)


def _attn_kernel(lam_ref, q_ref, k_ref, v_ref, g_ref, o_ref, *, tq, lam_init):
    s_len = q_ref.shape[0]
    lq = lam_ref[...]
    lam = (jnp.exp(jnp.sum(lq[0:1] * lq[1:2], axis=-1, keepdims=True))
           - jnp.exp(jnp.sum(lq[2:3] * lq[3:4], axis=-1, keepdims=True)) + lam_init)
    lane = lax.broadcasted_iota(jnp.int32, (1, LANES), 1)
    map0 = lane < DIFF_HEAD_DIM
    gain = g_ref[...] * (1.0 - lam_init)
    n_blk = s_len // tq

    def scores(i):
        skv = (i + 1) * tq
        q = q_ref[i * tq:(i + 1) * tq, :]
        zero = jnp.zeros_like(q)
        k = k_ref[0:skv, :]
        s0 = lax.dot_general(jnp.where(map0, q, zero), k, _NT, preferred_element_type=F32)
        s1 = lax.dot_general(jnp.where(map0, zero, q), k, _NT, preferred_element_type=F32)
        row = lax.broadcasted_iota(jnp.int32, (tq, skv), 0) + i * tq
        col = lax.broadcasted_iota(jnp.int32, (tq, skv), 1)
        keep = col <= row
        return jnp.where(keep, s0, -jnp.inf), jnp.where(keep, s1, -jnp.inf)

    def probs(s0, s1):
        e0 = jnp.exp2(s0 - jnp.max(s0, axis=-1, keepdims=True))
        e1 = jnp.exp2(s1 - jnp.max(s1, axis=-1, keepdims=True))
        l0 = jnp.sum(e0, axis=-1, keepdims=True)
        l1 = jnp.sum(e1, axis=-1, keepdims=True)
        return (e0 - e1 * (lam * l0 / l1)).astype(BF16), 1.0 / l0

    def emit(i, p_and_r0):
        p, r0 = p_and_r0
        o = jnp.dot(p, v_ref[0:(i + 1) * tq, :], preferred_element_type=F32) * r0
        ms = jnp.mean(o * o, axis=-1, keepdims=True)
        o_ref[i * tq:(i + 1) * tq, :] = (o * lax.rsqrt(ms + SUBLN_EPS) * gain).astype(o_ref.dtype)

    s_next = scores(0)
    for i in range(n_blk):
        s_cur = s_next
        if i + 1 < n_blk:
            s_next = scores(i + 1)
        emit(i, probs(*s_cur))


def _attn(qk, v, lam_vecs, norm_g, layer, *, batch, seq, n_heads, tq, lam_init):
    n = qk.shape[0]
    hd = 2 * DIFF_HEAD_DIM
    return pl.pallas_call(
        functools.partial(_attn_kernel, tq=tq, lam_init=lam_init),
        grid=(batch, n_heads),
        in_specs=[
            pl.BlockSpec((None, 4, DIFF_HEAD_DIM), lambda b, h: (layer, 0, 0)),
            pl.BlockSpec((seq, hd), lambda b, h: (b, h)),
            pl.BlockSpec((seq, hd), lambda b, h: (b, n_heads + h)),
            pl.BlockSpec((seq, hd), lambda b, h: (b, h)),
            pl.BlockSpec((None, 1, hd), lambda b, h: (layer, 0, 0)),
        ],
        out_specs=pl.BlockSpec((seq, hd), lambda b, h: (b, h)),
        out_shape=jax.ShapeDtypeStruct((n, n_heads * hd), BF16),
        compiler_params=_params(("parallel", "parallel")),
        name="diff_attn",
    )(lam_vecs, qk, qk, v, norm_g)


def _chunk_scan(x, pos, axis, reverse=False):
    size = x.shape[axis]
    s = 1
    while s < CHUNK:
        if reverse:
            x = x + jnp.where(pos < CHUNK - s, pltpu.roll(x, size - s, axis), 0.0)
        else:
            x = x + jnp.where(pos >= s, pltpu.roll(x, s, axis), 0.0)
        s *= 2
    return x


def _interleave(lead, fill):
    done = 0
    for idx, item in enumerate(lead):
        item()
        upto = (idx + 1) * len(fill) // len(lead)
        for f in fill[done:upto]:
            f()
        done = upto


def _gdn_kernel(gq_ref, gk_ref, gv_ref, gz_ref, cw_ref, bdc_ref, bdr_ref,
                alc_ref, dtc_ref, alr_ref, dtr_ref, ng_ref, o_ref,
                hist_ref, state_ref, qn_ref, kn_ref, vc_ref, u_ref, wq_ref, akt_ref, cd_ref,
                *, nh, group, nblk):
    rows_blk, width = gq_ref.shape
    n_super = rows_blk // SUPER
    n_chunk = rows_blk // CHUNK
    s_idx = pl.program_id(0)
    wslot = s_idx % 2
    rslot = 1 - wslot

    @pl.when(s_idx == 0)
    def _first():
        hist_ref[...] = jnp.zeros_like(hist_ref)
        state_ref[...] = jnp.zeros_like(state_ref)
        u_ref[1] = jnp.zeros(u_ref.shape[1:], F32)
        wq_ref[1] = jnp.zeros(wq_ref.shape[1:], BF16)
        akt_ref[1] = jnp.zeros(akt_ref.shape[1:], BF16)
        cd_ref[1] = jnp.zeros(cd_ref.shape[1:], F32)

    rows8 = lax.broadcasted_iota(jnp.int32, (SUBLANES, 1), 0)
    seq_start = (s_idx % nblk) == 0

    def conv_silu(u_ref_, idx, dst_ref):
        w = cw_ref[:, idx * width:(idx + 1) * width]
        hist = jnp.where(seq_start, 0.0, hist_ref[idx])
        head = u_ref_[0:SUBLANES, :]
        acc = u_ref_[SUBLANES:rows_blk, :] * w[CONV_K - 1:CONV_K]
        top = head * w[CONV_K - 1:CONV_K]
        for j in range(CONV_K - 1):
            sh = CONV_K - 1 - j
            acc = acc + u_ref_[SUBLANES - sh:rows_blk - sh, :] * w[j:j + 1]
            top = top + jnp.where(rows8 < sh, pltpu.roll(hist, sh, 0), pltpu.roll(head, sh, 0)) * w[j:j + 1]
        hist_ref[idx] = u_ref_[rows_blk - SUBLANES:rows_blk, :]
        dst_ref[0:SUBLANES, :] = _silu(top)
        dst_ref[SUBLANES:rows_blk, :] = _silu(acc)

    def l2_norm_heads():
        for h in range(nh):
            sl = slice(h * DN_HEAD_DIM, (h + 1) * DN_HEAD_DIM)
            qh, kh = qn_ref[:, sl], kn_ref[:, sl]
            qn_ref[:, sl] = qh * (lax.rsqrt(jnp.sum(qh * qh, axis=-1, keepdims=True) + L2_EPS) * DN_HEAD_DIM ** -0.5)
            kn_ref[:, sl] = kh * lax.rsqrt(jnp.sum(kh * kh, axis=-1, keepdims=True) + L2_EPS)

    gates = {}

    def gate_scans():
        rows = lax.broadcasted_iota(jnp.int32, (rows_blk, 1), 0)
        xg = bdc_ref[...]
        gates["beta"] = jax.nn.sigmoid(xg)
        g_raw = -jnp.exp(alc_ref[...]) * jax.nn.softplus(xg + dtc_ref[...])
        g_cum = _chunk_scan(g_raw, rows % CHUNK, 0)
        gates["g"] = g_cum
        eg = jnp.exp(g_cum)
        gates["eg"] = eg
        gates["ekd"] = jnp.exp(_chunk_scan(g_raw, rows % CHUNK, 0, reverse=True) - g_raw)
        cols = lax.broadcasted_iota(jnp.int32, (1, rows_blk), 1)
        gr_raw = -jnp.exp(alr_ref[...]) * jax.nn.softplus(bdr_ref[...] + dtr_ref[...])
        gates["g_row"] = _chunk_scan(gr_raw, cols % CHUNK, 1)
        for ci in range(n_chunk):
            cd_ref[wslot, ci * SUBLANES:(ci + 1) * SUBLANES, :] = eg[(ci + 1) * CHUNK - SUBLANES:(ci + 1) * CHUNK]

    ii = lax.broadcasted_iota(jnp.int32, (SUPER, SUPER), 0)
    jj = lax.broadcasted_iota(jnp.int32, (SUPER, SUPER), 1)
    same = (ii // CHUNK) == (jj // CHUNK)
    incl = same & (ii >= jj)
    strict = same & (ii > jj)
    eye = (ii == jj).astype(F32)

    def lower_left_mask(s):
        return ((ii // (2 * s)) == (jj // (2 * s))) & ((ii // s) % 2 == 1) & ((jj // s) % 2 == 0)

    lower_left = {}
    s = 1
    while s < CHUNK:
        lower_left[s] = lower_left_mask(s)
        s *= 2

    def wy_items(grp):
        st = {}

        def setup():
            st["lmat"], st["t"], st["rhs"] = [], [], []
            for m, h in grp:
                rs = slice(m * SUPER, (m + 1) * SUPER)
                sl = slice(h * DN_HEAD_DIM, (h + 1) * DN_HEAD_DIM)
                gl = nh + h
                qn, kn, v = qn_ref[rs, sl], kn_ref[rs, sl], vc_ref[rs, sl]
                beta, eg, ekd = gates["beta"][rs, h:h + 1], gates["eg"][rs, gl:gl + 1], gates["ekd"][rs, gl:gl + 1]
                decay = jnp.where(incl, jnp.exp(gates["g"][rs, gl:gl + 1] - gates["g_row"][gl:gl + 1, rs]), 0.0)
                kb = kn * beta
                knb = kn.astype(BF16)
                kk = lax.dot_general(kb.astype(BF16), knb, _NT, preferred_element_type=F32)
                qk = lax.dot_general(qn.astype(BF16), knb, _NT, preferred_element_type=F32)
                lmat = jnp.where(strict, kk * decay, 0.0)
                st["lmat"].append(lmat)
                st["t"].append(eye - jnp.where(lower_left[1], lmat, 0.0))
                st["rhs"].append(jnp.concatenate([v * beta, kb * eg], axis=1).astype(BF16))
                qd = (qn * eg).astype(BF16)
                base = 2 * SUPER * m
                for c2 in range(CHUNKS_PER_TILE):
                    row0 = base + c2 * 2 * CHUNK
                    wq_ref[wslot, row0 + CHUNK:row0 + 2 * CHUNK, sl] = qd[c2 * CHUNK:(c2 + 1) * CHUNK]
                akt_ref[wslot, base:base + SUPER, sl] = (qk * decay).astype(BF16)
                akt_ref[wslot, base + SUPER:base + 2 * SUPER, sl] = (kn * ekd).T.astype(BF16)

        def level(s):
            def run():
                ys = [jnp.dot(jnp.where(lower_left[s], lmat, 0.0).astype(BF16), t.astype(BF16),
                              preferred_element_type=F32) for lmat, t in zip(st["lmat"], st["t"])]
                st["t"] = [t - jnp.dot(t.astype(BF16), y.astype(BF16), preferred_element_type=F32)
                           for t, y in zip(st["t"], ys)]
            return run

        def solve():
            for (m, h), t, rhs in zip(grp, st["t"], st["rhs"]):
                rs = slice(m * SUPER, (m + 1) * SUPER)
                sl = slice(h * DN_HEAD_DIM, (h + 1) * DN_HEAD_DIM)
                sol = jnp.dot(t.astype(BF16), rhs, preferred_element_type=F32)
                u_ref[wslot, rs, sl] = sol[:, :DN_HEAD_DIM]
                w = sol[:, DN_HEAD_DIM:].astype(BF16)
                base = 2 * SUPER * m
                for c2 in range(CHUNKS_PER_TILE):
                    row0 = base + c2 * 2 * CHUNK
                    wq_ref[wslot, row0:row0 + CHUNK, sl] = w[c2 * CHUNK:(c2 + 1) * CHUNK]

        levels = []
        s = 2
        while s < CHUNK:
            levels.append(level(s))
            s *= 2
        return [setup] + levels + [solve]

    prep_items = [
        lambda: conv_silu(gq_ref, 0, qn_ref),
        lambda: conv_silu(gk_ref, 1, kn_ref),
        lambda: conv_silu(gv_ref, 2, vc_ref),
        l2_norm_heads,
        gate_scans,
    ]
    chains = [(m, h) for m in range(n_super) for h in range(nh)]
    for g0 in range(0, len(chains), group):
        prep_items += wy_items(chains[g0:g0 + group])

    zeros_half = jnp.zeros((CHUNK, DN_HEAD_DIM), BF16)
    keep_state = jnp.where(((s_idx + nblk - 1) % nblk) == 0, 0.0, 1.0)
    states = [state_ref[h] * keep_state for h in range(nh)]

    def chunk_item(ci):
        def run():
            m, c2 = divmod(ci, CHUNKS_PER_TILE)
            base = 2 * SUPER * m
            r0 = ci * CHUNK
            cd_row = cd_ref[rslot, ci * SUBLANES + SUBLANES - 1:(ci + 1) * SUBLANES, :]
            for h in range(nh):
                sl = slice(h * DN_HEAD_DIM, (h + 1) * DN_HEAD_DIM)
                row0 = base + c2 * 2 * CHUNK
                wq = wq_ref[rslot, row0:row0 + 2 * CHUNK, sl]
                res1 = jnp.dot(wq, states[h].astype(BF16), preferred_element_type=F32)
                vb = (u_ref[rslot, r0:r0 + CHUNK, sl] - res1[:CHUNK]).astype(BF16)
                parts = [vb if k == c2 else zeros_half for k in range(CHUNKS_PER_TILE)]
                rhs = parts[0] if CHUNKS_PER_TILE == 1 else jnp.concatenate(parts, axis=0)
                res2 = jnp.dot(akt_ref[rslot, base:base + 2 * SUPER, sl], rhs, preferred_element_type=F32)
                states[h] = states[h] * cd_row[:, nh + h:nh + h + 1] + res2[SUPER:]
                o = res1[CHUNK:] + res2[c2 * CHUNK:(c2 + 1) * CHUNK]
                ms = jnp.mean(o * o, axis=-1, keepdims=True)
                o_ref[r0:r0 + CHUNK, sl] = (o * lax.rsqrt(ms + GATED_NORM_EPS) * ng_ref[...]
                                            * _silu(gz_ref[r0:r0 + CHUNK, sl])).astype(o_ref.dtype)
        return run

    _interleave([chunk_item(ci) for ci in range(n_chunk)], prep_items)

    for h in range(nh):
        state_ref[h] = states[h]


def _gdn(proj, conv_w, bdc, bdr, alc, dtc, alr, dtr, norm_g, layer, *, batch, seq, nh, rows_blk):
    n = proj.shape[0]
    width = nh * DN_HEAD_DIM
    nblk = seq // rows_blk
    total = batch * nblk
    rows_r = bdr.shape[0]
    nxt = lambda s: jnp.minimum(s, total - 1)
    cur = lambda s: jnp.maximum(s - 1, 0)

    lane_vec = pl.BlockSpec((None, 1, LANES), lambda s: (layer, 0, 0))
    row_vec = pl.BlockSpec((None, rows_r, 1), lambda s: (layer, 0, 0))
    return pl.pallas_call(
        functools.partial(_gdn_kernel, nh=nh, group=GDN_CHAIN_GROUP, nblk=nblk),
        grid=(total + 1,),
        in_specs=[
            pl.BlockSpec((rows_blk, width), lambda s: (nxt(s), 0)),
            pl.BlockSpec((rows_blk, width), lambda s: (nxt(s), 1)),
            pl.BlockSpec((rows_blk, width), lambda s: (nxt(s), 2)),
            pl.BlockSpec((rows_blk, width), lambda s: (cur(s), 3)),
            pl.BlockSpec((None, CONV_K, 3 * width), lambda s: (layer, 0, 0)),
            pl.BlockSpec((rows_blk, LANES), lambda s: (nxt(s), 0)),
            pl.BlockSpec((rows_r, rows_blk), lambda s: (0, nxt(s))),
            lane_vec, lane_vec, row_vec, row_vec,
            pl.BlockSpec((None, 1, DN_HEAD_DIM), lambda s: (layer, 0, 0)),
        ],
        out_specs=pl.BlockSpec((rows_blk, width), lambda s: (cur(s), 0)),
        out_shape=jax.ShapeDtypeStruct((n, width), BF16),
        scratch_shapes=[
            pltpu.VMEM((3, SUBLANES, width), F32),
            pltpu.VMEM((nh, DN_HEAD_DIM, DN_HEAD_DIM), F32),
            pltpu.VMEM((rows_blk, width), F32),
            pltpu.VMEM((rows_blk, width), F32),
            pltpu.VMEM((rows_blk, width), F32),
            pltpu.VMEM((2, rows_blk, width), F32),
            pltpu.VMEM((2, 2 * rows_blk, width), BF16),
            pltpu.VMEM((2, 2 * rows_blk, width), BF16),
            pltpu.VMEM((2, SUBLANES * (rows_blk // CHUNK), LANES), F32),
        ],
        compiler_params=_params(("arbitrary",)),
        name="gdn",
    )(proj, proj, proj, proj, conv_w, bdc, bdr, alc, dtc, alr, dtr, norm_g)


def _outproj_ln_kernel(x_ref, od_ref, on_ref, w1_ref, w2_ref, g_ref, b_ref, y_ref, *, alpha):
    half = x_ref.shape[0] // 2
    for r in range(2):
        rs = slice(r * half, (r + 1) * half)
        mix = (jnp.dot(od_ref[rs, :], w1_ref[...], preferred_element_type=F32)
               + jnp.dot(on_ref[rs, :], w2_ref[...], preferred_element_type=F32))
        y_ref[rs, :] = _layer_norm(alpha * x_ref[rs, :] + mix, g_ref[...], b_ref[...])


def _outproj_ln(x, o_diff, o_dn, w_out, g, b, layer, *, alpha, tm):
    n, d = x.shape
    k1, k2 = o_diff.shape[1], o_dn.shape[1]
    assert k1 == k2
    vec = pl.BlockSpec((None, 1, d), lambda i: (layer, 0, 0))
    return pl.pallas_call(
        functools.partial(_outproj_ln_kernel, alpha=alpha),
        grid=(n // tm,),
        in_specs=[
            pl.BlockSpec((tm, d), lambda i: (i, 0)),
            pl.BlockSpec((tm, k1), lambda i: (i, 0)),
            pl.BlockSpec((tm, k2), lambda i: (i, 0)),
            pl.BlockSpec((None, k1, d), lambda i: (layer, 0, 0)),
            pl.BlockSpec((None, k2, d), lambda i: (layer, 1, 0)),
            vec, vec,
        ],
        out_specs=pl.BlockSpec((tm, d), lambda i: (i, 0)),
        out_shape=jax.ShapeDtypeStruct((n, d), F32),
        compiler_params=_params(("parallel",)),
        name="outproj_ln",
    )(x, o_diff, o_dn, w_out, w_out, g, b)


def _tile(n, pref):
    t = min(n, pref)
    assert n % t == 0, (n, pref)
    return t


def kernel(x, positions, ffn1_w_in, ffn1_w_out, ln1_g, ln1_b, w_in, conv_w, a_log, dt_bias, lam_q1, lam_k1, lam_q2, lam_k2, diff_norm_g, delta_norm_g, w_out, ln2_g, ln2_b, ffn2_w_in, ffn2_w_out, ln3_g, ln3_b):
    batch, seq, d = x.shape
    depth = ffn1_w_in.shape[0]
    d_ff = ffn1_w_out.shape[1]
    n = batch * seq
    diff_width = d // 2
    dn_width = d - diff_width
    n_diff_heads = diff_width // (2 * DIFF_HEAD_DIM)
    nh = dn_width // DN_HEAD_DIM
    qk_cols = 2 * (2 * n_diff_heads * DIFF_HEAD_DIM)
    v_cols = n_diff_heads * 2 * DIFF_HEAD_DIM
    main_cols = qk_cols + v_cols + 4 * dn_width
    assert w_in.shape[2] == main_cols + 2 * nh and 2 * nh <= LANES
    assert seq % SUPER == 0
    alpha = (2 * depth) ** 0.25

    tm = _tile(n, 512)
    tm_proj = _tile(n, 1024)
    tf = _tile(d_ff, 512)
    tn = _tile(v_cols, 1024)
    tq = _tile(seq, 256)
    gdn_rows = _tile(seq, 512)

    ffn1_in, ffn1_out = ffn1_w_in.astype(BF16), (0.5 * ffn1_w_out).astype(BF16)
    ffn2_in, ffn2_out = ffn2_w_in.astype(BF16), (0.5 * ffn2_w_out).astype(BF16)
    w_main = w_in.astype(BF16)
    w_gate = w_main[:, :, main_cols:]
    w_bd = jnp.pad(w_gate, ((0, 0), (0, 0), (0, LANES - 2 * nh)))
    rows_r = -(-2 * nh // SUBLANES) * SUBLANES
    w_out_b = w_out.astype(BF16)
    pad_lane = lambda v: jnp.pad(v.astype(F32), ((0, 0), (nh, LANES - 2 * nh))).reshape(depth, 1, LANES)
    pad_row = lambda v: jnp.pad(v.astype(F32), ((0, 0), (nh, rows_r - 2 * nh))).reshape(depth, rows_r, 1)
    alc, dtc, alr, dtr = pad_lane(a_log), pad_lane(dt_bias), pad_row(a_log), pad_row(dt_bias)
    lam_vecs = jnp.stack([lam_q1, lam_k1, lam_q2, lam_k2], axis=1).astype(F32)
    vec3 = lambda v: v.reshape(depth, 1, v.shape[-1])
    ln1_g, ln1_b, ln2_g, ln2_b, ln3_g, ln3_b = map(vec3, (ln1_g, ln1_b, ln2_g, ln2_b, ln3_g, ln3_b))
    diff_g, delta_g = vec3(diff_norm_g), vec3(delta_norm_g)

    rope_c, rope_sa, rope_sb = _rope_tables(positions, tm)

    h = x.reshape(n, d)
    for l in range(depth):
        lam_init = 0.8 - 0.6 * math.exp(-0.3 * l)
        h, hb16 = _ffn_ln(h, ffn1_in, ffn1_out, ln1_g, ln1_b, l, alpha=alpha, tm=tm, tf=tf, emit_bf16=True)
        qk, v, dn, bdc, bdr = _proj(hb16, w_main, w_bd, (rope_c, rope_sa, rope_sb), l, rows=rows_r,
                                    qk_cols=qk_cols, v_cols=v_cols, dn_cols=4 * dn_width, tm=tm_proj, tn=tn)
        o_diff = _attn(qk, v, lam_vecs, diff_g, l, batch=batch, seq=seq, n_heads=n_diff_heads, tq=tq,
                       lam_init=lam_init)
        o_dn = _gdn(dn, conv_w, bdc, bdr, alc, dtc, alr, dtr, delta_g, l, batch=batch, seq=seq, nh=nh,
                    rows_blk=gdn_rows)
        h = _outproj_ln(h, o_diff, o_dn, w_out_b, ln2_g, ln2_b, l, alpha=alpha, tm=tm)
        (h,) = _ffn_ln(h, ffn2_in, ffn2_out, ln3_g, ln3_b, l, alpha=alpha, tm=tm, tf=tf, emit_bf16=False)
    return h.reshape(batch, seq, d)
```

```python
import functools
import math

import jax
import jax.numpy as jnp
from jax import lax
from jax.experimental import pallas as pl
from jax.experimental.pallas import tpu as pltpu

F32 = jnp.float32
BF16 = jnp.bfloat16

LANES = 128
SUBLANES = 8
DIFF_HEAD_DIM = 64
DN_HEAD_DIM = 128
CONV_K = 4
CHUNK = 128
SUPER = 128
CHUNKS_PER_TILE = SUPER // CHUNK
GDN_CHAIN_GROUP = 16
ROPE_THETA = 500000.0
ROPE_DIM = DIFF_HEAD_DIM // 4
ROPE_HALF = ROPE_DIM // 2
LN_EPS = 1e-5
SUBLN_EPS = 1e-5
GATED_NORM_EPS = 1e-6
L2_EPS = 1e-6
LOG2_E = math.log2(math.e)
VMEM_LIMIT_BYTES = 56 * 1024 * 1024

_NT = (((1,), (1,)), ((), ()))


def _params(sem):
    return pltpu.CompilerParams(dimension_semantics=sem, vmem_limit_bytes=VMEM_LIMIT_BYTES)


def _layer_norm(y, g, b):
    mu = jnp.mean(y, axis=-1, keepdims=True)
    yc = y - mu
    var = jnp.mean(yc * yc, axis=-1, keepdims=True)
    return yc * lax.rsqrt(var + LN_EPS) * g + b


def _silu(x):
    return x * jax.nn.sigmoid(x)


def _rope_table_kernel(pos_ref, inv_ref, c_ref, sa_ref, sb_ref):
    ang = pos_ref[...].astype(F32) * inv_ref[...]
    lane = lax.broadcasted_iota(jnp.int32, ang.shape, 1) % DIFF_HEAD_DIM
    cos, sin = jnp.cos(ang), jnp.sin(ang)
    c_ref[...] = jnp.where(lane < ROPE_DIM, cos, 1.0)
    sa_ref[...] = jnp.where(lane < ROPE_HALF, -sin, 0.0)
    sb_ref[...] = jnp.where((lane >= ROPE_HALF) & (lane < ROPE_DIM), sin, 0.0)


def _rope_tables(positions, tm):
    n = positions.size
    pos = positions.reshape(n, 1)
    inv_freq = ROPE_THETA ** (-jnp.arange(0, ROPE_DIM, 2, dtype=F32) / ROPE_DIM)
    lane = jnp.arange(LANES) % DIFF_HEAD_DIM
    inv = jnp.where(lane < ROPE_DIM, inv_freq[lane % ROPE_HALF], 0.0).reshape(1, LANES).astype(F32)
    out = jax.ShapeDtypeStruct((n, LANES), F32)
    spec = pl.BlockSpec((tm, LANES), lambda i: (i, 0))
    return pl.pallas_call(
        _rope_table_kernel,
        grid=(n // tm,),
        in_specs=[pl.BlockSpec((tm, 1), lambda i: (i, 0)), pl.BlockSpec((1, LANES), lambda i: (0, 0))],
        out_specs=[spec, spec, spec],
        out_shape=[out, out, out],
        compiler_params=_params(("parallel",)),
        name="rope_tables",
    )(pos, inv)


def _ffn_ln_kernel(x_ref, wg_ref, wu_ref, wo_ref, g_ref, b_ref, *rest, alpha, emit_bf16, n_row, ln_rows):
    if emit_bf16:
        y_ref, yb_ref, xb_ref, acc_ref = rest
    else:
        y_ref, xb_ref, acc_ref = rest
    i, j = pl.program_id(0), pl.program_id(1)
    tm = xb_ref.shape[0]
    slot = i % 2

    @pl.when((i == 0) & (j == 0))
    def _first():
        acc_ref[1] = jnp.zeros(acc_ref.shape[1:], F32)

    def ln_slice():
        r0 = pl.multiple_of(jnp.minimum(j * ln_rows, tm - ln_rows), SUBLANES)
        y = _layer_norm(acc_ref[1 - slot, pl.ds(r0, ln_rows), :], g_ref[...], b_ref[...])
        y_ref[pl.ds(r0, ln_rows), :] = y
        if emit_bf16:
            yb_ref[pl.ds(r0, ln_rows), :] = y.astype(BF16)

    @pl.when((i < n_row) & (j == 0))
    def _cast():
        xb_ref[...] = x_ref[...].astype(BF16)

    def main(first):
        ln_slice()
        xb = xb_ref[...]
        hg = jnp.dot(xb, wg_ref[...], preferred_element_type=F32)
        hu = jnp.dot(xb, wu_ref[...], preferred_element_type=F32)
        a = (_silu(hg) * hu).astype(BF16)
        part = jnp.dot(a, wo_ref[...], preferred_element_type=F32)
        acc_ref[slot] = (alpha * x_ref[...] if first else acc_ref[slot]) + part

    pl.when((i < n_row) & (j == 0))(lambda: main(True))
    pl.when((i < n_row) & (j > 0))(lambda: main(False))

    @pl.when(i == n_row)
    def _tail():
        ln_slice()


def _ffn_ln(x, w_in, w_out_half, g, b, layer, *, alpha, tm, tf, emit_bf16):
    n, d = x.shape
    f = w_out_half.shape[1]
    nf = f // tf
    n_row = n // tm
    ln_rows = -(-tm // (nf * SUBLANES)) * SUBLANES
    assert ln_rows <= tm
    last = n_row - 1
    col = lambda i, j: jnp.where(i < n_row, j, nf - 1)
    prev = pl.BlockSpec((tm, d), lambda i, j: (jnp.maximum(i - 1, 0), 0))
    vec = pl.BlockSpec((None, 1, d), lambda i, j: (layer, 0, 0))
    out_shape = [jax.ShapeDtypeStruct((n, d), F32)]
    out_specs = [prev]
    if emit_bf16:
        out_shape.append(jax.ShapeDtypeStruct((n, d), BF16))
        out_specs.append(prev)
    return pl.pallas_call(
        functools.partial(_ffn_ln_kernel, alpha=alpha, emit_bf16=emit_bf16, n_row=n_row, ln_rows=ln_rows),
        grid=(n_row + 1, nf),
        in_specs=[
            pl.BlockSpec((tm, d), lambda i, j: (jnp.minimum(i, last), 0)),
            pl.BlockSpec((None, d, tf), lambda i, j: (layer, 0, col(i, j))),
            pl.BlockSpec((None, d, tf), lambda i, j: (layer, 0, col(i, j) + nf)),
            pl.BlockSpec((None, tf, d), lambda i, j: (layer, col(i, j), 0)),
            vec, vec,
        ],
        out_specs=out_specs,
        out_shape=out_shape,
        scratch_shapes=[pltpu.VMEM((tm, d), BF16), pltpu.VMEM((2, tm, d), F32)],
        compiler_params=_params(("arbitrary", "arbitrary")),
        name="ffn_ln",
    )(x, w_in, w_in, w_out_half, g, b)


def _proj_kernel(xb_ref, w_ref, wbd_ref, c_ref, sa_ref, sb_ref,
                 qk_ref, v_ref, dn_ref, bdc_ref, bdr_ref, *, n_q, n_qk, n_v):
    j = pl.program_id(1)
    tm, tn = qk_ref.shape
    half = tm // 2

    @pl.when(j == 0)
    def _gates():
        gates = jnp.dot(xb_ref[...], wbd_ref[...], preferred_element_type=F32)
        bdc_ref[...] = gates
        bdr_ref[...] = gates.T[0:bdr_ref.shape[0], :]

    def tile(store):
        for r in range(2):
            rs = slice(r * half, (r + 1) * half)
            store(jnp.dot(xb_ref[rs, :], w_ref[...], preferred_element_type=F32), rs)

    def rope_store(acc, rs):
        scale = jnp.where(j < n_q, DIFF_HEAD_DIM ** -0.5 * LOG2_E, 1.0)
        c, sa, sb = c_ref[rs, :] * scale, sa_ref[rs, :] * scale, sb_ref[rs, :] * scale
        for t in range(tn // LANES):
            seg = acc[:, t * LANES:(t + 1) * LANES]
            rot = seg * c + pltpu.roll(seg, LANES - ROPE_HALF, 1) * sa + pltpu.roll(seg, ROPE_HALF, 1) * sb
            qk_ref[rs, t * LANES:(t + 1) * LANES] = rot.astype(qk_ref.dtype)

    def v_store(acc, rs):
        v_ref[rs, :] = acc.astype(v_ref.dtype)

    def dn_store(acc, rs):
        dn_ref[rs, :] = acc

    pl.when(j < n_qk)(lambda: tile(rope_store))
    pl.when((j >= n_qk) & (j < n_qk + n_v))(lambda: tile(v_store))
    pl.when(j >= n_qk + n_v)(lambda: tile(dn_store))


def _proj(xb, w_main, w_bd, rope, layer, *, rows, qk_cols, v_cols, dn_cols, tm, tn):
    n, d = xb.shape
    n_qk, n_v, n_dn = qk_cols // tn, v_cols // tn, dn_cols // tn
    c, sa, sb = rope
    tspec = pl.BlockSpec((tm, LANES), lambda i, j: (i, 0))
    return pl.pallas_call(
        functools.partial(_proj_kernel, n_q=n_qk // 2, n_qk=n_qk, n_v=n_v),
        grid=(n // tm, n_qk + n_v + n_dn),
        in_specs=[
            pl.BlockSpec((tm, d), lambda i, j: (i, 0)),
            pl.BlockSpec((None, d, tn), lambda i, j: (layer, 0, j)),
            pl.BlockSpec((None, d, LANES), lambda i, j: (layer, 0, 0)),
            tspec, tspec, tspec,
        ],
        out_specs=[
            pl.BlockSpec((tm, tn), lambda i, j: (i, jnp.minimum(j, n_qk - 1))),
            pl.BlockSpec((tm, tn), lambda i, j: (i, jnp.clip(j - n_qk, 0, n_v - 1))),
            pl.BlockSpec((tm, tn), lambda i, j: (i, jnp.clip(j - n_qk - n_v, 0, n_dn - 1))),
            pl.BlockSpec((tm, LANES), lambda i, j: (i, 0)),
            pl.BlockSpec((rows, tm), lambda i, j: (0, i)),
        ],
        out_shape=[
            jax.ShapeDtypeStruct((n, qk_cols), BF16),
            jax.ShapeDtypeStruct((n, v_cols), BF16),
            jax.ShapeDtypeStruct((n, dn_cols), F32),
            jax.ShapeDtypeStruct((n, LANES), F32),
            jax.ShapeDtypeStruct((rows, n), F32),
        ],
        compiler_params=_params(("parallel", "arbitrary")),
        name="proj",
    )(xb, w_main, w_bd, c, sa, sb)


def _attn_kernel(lam_ref, q_ref, k_ref, v_ref, g_ref, o_ref, *, tq, lam_init):
    s_len = q_ref.shape[0]
    lq = lam_ref[...]
    lam = (jnp.exp(jnp.sum(lq[0:1] * lq[1:2], axis=-1, keepdims=True))
           - jnp.exp(jnp.sum(lq[2:3] * lq[3:4], axis=-1, keepdims=True)) + lam_init)
    lane = lax.broadcasted_iota(jnp.int32, (1, LANES), 1)
    map0 = lane < DIFF_HEAD_DIM
    gain = g_ref[...] * (1.0 - lam_init)
    n_blk = s_len // tq

    def scores(i):
        skv = (i + 1) * tq
        q = q_ref[i * tq:(i + 1) * tq, :]
        zero = jnp.zeros_like(q)
        k = k_ref[0:skv, :]
        s0 = lax.dot_general(jnp.where(map0, q, zero), k, _NT, preferred_element_type=F32)
        s1 = lax.dot_general(jnp.where(map0, zero, q), k, _NT, preferred_element_type=F32)
        row = lax.broadcasted_iota(jnp.int32, (tq, skv), 0) + i * tq
        col = lax.broadcasted_iota(jnp.int32, (tq, skv), 1)
        keep = col <= row
        return jnp.where(keep, s0, -jnp.inf), jnp.where(keep, s1, -jnp.inf)

    def probs(s0, s1):
        e0 = jnp.exp2(s0 - jnp.max(s0, axis=-1, keepdims=True))
        e1 = jnp.exp2(s1 - jnp.max(s1, axis=-1, keepdims=True))
        l0 = jnp.sum(e0, axis=-1, keepdims=True)
        l1 = jnp.sum(e1, axis=-1, keepdims=True)
        return (e0 - e1 * (lam * l0 / l1)).astype(BF16), 1.0 / l0

    def emit(i, p_and_r0):
        p, r0 = p_and_r0
        o = jnp.dot(p, v_ref[0:(i + 1) * tq, :], preferred_element_type=F32) * r0
        ms = jnp.mean(o * o, axis=-1, keepdims=True)
        o_ref[i * tq:(i + 1) * tq, :] = (o * lax.rsqrt(ms + SUBLN_EPS) * gain).astype(o_ref.dtype)

    s_next = scores(0)
    for i in range(n_blk):
        s_cur = s_next
        if i + 1 < n_blk:
            s_next = scores(i + 1)
        emit(i, probs(*s_cur))


def _attn(qk, v, lam_vecs, norm_g, layer, *, batch, seq, n_heads, tq, lam_init):
    n = qk.shape[0]
    hd = 2 * DIFF_HEAD_DIM
    return pl.pallas_call(
        functools.partial(_attn_kernel, tq=tq, lam_init=lam_init),
        grid=(batch, n_heads),
        in_specs=[
            pl.BlockSpec((None, 4, DIFF_HEAD_DIM), lambda b, h: (layer, 0, 0)),
            pl.BlockSpec((seq, hd), lambda b, h: (b, h)),
            pl.BlockSpec((seq, hd), lambda b, h: (b, n_heads + h)),
            pl.BlockSpec((seq, hd), lambda b, h: (b, h)),
            pl.BlockSpec((None, 1, hd), lambda b, h: (layer, 0, 0)),
        ],
        out_specs=pl.BlockSpec((seq, hd), lambda b, h: (b, h)),
        out_shape=jax.ShapeDtypeStruct((n, n_heads * hd), BF16),
        compiler_params=_params(("parallel", "parallel")),
        name="diff_attn",
    )(lam_vecs, qk, qk, v, norm_g)


def _chunk_scan(x, pos, axis, reverse=False):
    size = x.shape[axis]
    s = 1
    while s < CHUNK:
        if reverse:
            x = x + jnp.where(pos < CHUNK - s, pltpu.roll(x, size - s, axis), 0.0)
        else:
            x = x + jnp.where(pos >= s, pltpu.roll(x, s, axis), 0.0)
        s *= 2
    return x


def _interleave(lead, fill):
    done = 0
    for idx, item in enumerate(lead):
        item()
        upto = (idx + 1) * len(fill) // len(lead)
        for f in fill[done:upto]:
            f()
        done = upto


def _gdn_kernel(gq_ref, gk_ref, gv_ref, gz_ref, cw_ref, bdc_ref, bdr_ref,
                alc_ref, dtc_ref, alr_ref, dtr_ref, ng_ref, o_ref,
                hist_ref, state_ref, qn_ref, kn_ref, vc_ref, u_ref, wq_ref, akt_ref, cd_ref,
                *, nh, group, nblk):
    rows_blk, width = gq_ref.shape
    n_super = rows_blk // SUPER
    n_chunk = rows_blk // CHUNK
    s_idx = pl.program_id(0)
    wslot = s_idx % 2
    rslot = 1 - wslot

    @pl.when(s_idx == 0)
    def _first():
        hist_ref[...] = jnp.zeros_like(hist_ref)
        state_ref[...] = jnp.zeros_like(state_ref)
        u_ref[1] = jnp.zeros(u_ref.shape[1:], F32)
        wq_ref[1] = jnp.zeros(wq_ref.shape[1:], BF16)
        akt_ref[1] = jnp.zeros(akt_ref.shape[1:], BF16)
        cd_ref[1] = jnp.zeros(cd_ref.shape[1:], F32)

    rows8 = lax.broadcasted_iota(jnp.int32, (SUBLANES, 1), 0)
    seq_start = (s_idx % nblk) == 0

    def conv_silu(u_ref_, idx, dst_ref):
        w = cw_ref[:, idx * width:(idx + 1) * width]
        hist = jnp.where(seq_start, 0.0, hist_ref[idx])
        head = u_ref_[0:SUBLANES, :]
        acc = u_ref_[SUBLANES:rows_blk, :] * w[CONV_K - 1:CONV_K]
        top = head * w[CONV_K - 1:CONV_K]
        for j in range(CONV_K - 1):
            sh = CONV_K - 1 - j
            acc = acc + u_ref_[SUBLANES - sh:rows_blk - sh, :] * w[j:j + 1]
            top = top + jnp.where(rows8 < sh, pltpu.roll(hist, sh, 0), pltpu.roll(head, sh, 0)) * w[j:j + 1]
        hist_ref[idx] = u_ref_[rows_blk - SUBLANES:rows_blk, :]
        dst_ref[0:SUBLANES, :] = _silu(top)
        dst_ref[SUBLANES:rows_blk, :] = _silu(acc)

    def l2_norm_heads():
        for h in range(nh):
            sl = slice(h * DN_HEAD_DIM, (h + 1) * DN_HEAD_DIM)
            qh, kh = qn_ref[:, sl], kn_ref[:, sl]
            qn_ref[:, sl] = qh * (lax.rsqrt(jnp.sum(qh * qh, axis=-1, keepdims=True) + L2_EPS) * DN_HEAD_DIM ** -0.5)
            kn_ref[:, sl] = kh * lax.rsqrt(jnp.sum(kh * kh, axis=-1, keepdims=True) + L2_EPS)

    gates = {}

    def gate_scans():
        rows = lax.broadcasted_iota(jnp.int32, (rows_blk, 1), 0)
        xg = bdc_ref[...]
        gates["beta"] = jax.nn.sigmoid(xg)
        g_raw = -jnp.exp(alc_ref[...]) * jax.nn.softplus(xg + dtc_ref[...])
        g_cum = _chunk_scan(g_raw, rows % CHUNK, 0)
        gates["g"] = g_cum
        eg = jnp.exp(g_cum)
        gates["eg"] = eg
        gates["ekd"] = jnp.exp(_chunk_scan(g_raw, rows % CHUNK, 0, reverse=True) - g_raw)
        cols = lax.broadcasted_iota(jnp.int32, (1, rows_blk), 1)
        gr_raw = -jnp.exp(alr_ref[...]) * jax.nn.softplus(bdr_ref[...] + dtr_ref[...])
        gates["g_row"] = _chunk_scan(gr_raw, cols % CHUNK, 1)
        for ci in range(n_chunk):
            cd_ref[wslot, ci * SUBLANES:(ci + 1) * SUBLANES, :] = eg[(ci + 1) * CHUNK - SUBLANES:(ci + 1) * CHUNK]

    ii = lax.broadcasted_iota(jnp.int32, (SUPER, SUPER), 0)
    jj = lax.broadcasted_iota(jnp.int32, (SUPER, SUPER), 1)
    same = (ii // CHUNK) == (jj // CHUNK)
    incl = same & (ii >= jj)
    strict = same & (ii > jj)
    eye = (ii == jj).astype(F32)

    def lower_left_mask(s):
        return ((ii // (2 * s)) == (jj // (2 * s))) & ((ii // s) % 2 == 1) & ((jj // s) % 2 == 0)

    lower_left = {}
    s = 1
    while s < CHUNK:
        lower_left[s] = lower_left_mask(s)
        s *= 2

    def wy_items(grp):
        st = {}

        def setup():
            st["lmat"], st["t"], st["rhs"] = [], [], []
            for m, h in grp:
                rs = slice(m * SUPER, (m + 1) * SUPER)
                sl = slice(h * DN_HEAD_DIM, (h + 1) * DN_HEAD_DIM)
                gl = nh + h
                qn, kn, v = qn_ref[rs, sl], kn_ref[rs, sl], vc_ref[rs, sl]
                beta, eg, ekd = gates["beta"][rs, h:h + 1], gates["eg"][rs, gl:gl + 1], gates["ekd"][rs, gl:gl + 1]
                decay = jnp.where(incl, jnp.exp(gates["g"][rs, gl:gl + 1] - gates["g_row"][gl:gl + 1, rs]), 0.0)
                kb = kn * beta
                knb = kn.astype(BF16)
                kk = lax.dot_general(kb.astype(BF16), knb, _NT, preferred_element_type=F32)
                qk = lax.dot_general(qn.astype(BF16), knb, _NT, preferred_element_type=F32)
                lmat = jnp.where(strict, kk * decay, 0.0)
                st["lmat"].append(lmat.astype(BF16))
                st["t"].append((eye - jnp.where(lower_left[1], lmat, 0.0)).astype(BF16))
                st["rhs"].append(jnp.concatenate([v * beta, kb * eg], axis=1).astype(BF16))
                qd = (qn * eg).astype(BF16)
                base = 2 * SUPER * m
                for c2 in range(CHUNKS_PER_TILE):
                    row0 = base + c2 * 2 * CHUNK
                    wq_ref[wslot, row0 + CHUNK:row0 + 2 * CHUNK, sl] = qd[c2 * CHUNK:(c2 + 1) * CHUNK]
                akt_ref[wslot, base:base + SUPER, sl] = (qk * decay).astype(BF16)
                akt_ref[wslot, base + SUPER:base + 2 * SUPER, sl] = (kn * ekd).T.astype(BF16)

        def level(s):
            def run():
                zero = jnp.zeros((SUPER, SUPER), BF16)
                ys = [jnp.dot(jnp.where(lower_left[s], lmat, zero), t, preferred_element_type=F32)
                      for lmat, t in zip(st["lmat"], st["t"])]
                st["t"] = [t - jnp.dot(t, y.astype(BF16), preferred_element_type=F32).astype(BF16)
                           for t, y in zip(st["t"], ys)]
            return run

        def solve():
            for (m, h), t, rhs in zip(grp, st["t"], st["rhs"]):
                rs = slice(m * SUPER, (m + 1) * SUPER)
                sl = slice(h * DN_HEAD_DIM, (h + 1) * DN_HEAD_DIM)
                sol = jnp.dot(t, rhs, preferred_element_type=F32)
                u_ref[wslot, rs, sl] = sol[:, :DN_HEAD_DIM]
                w = sol[:, DN_HEAD_DIM:].astype(BF16)
                base = 2 * SUPER * m
                for c2 in range(CHUNKS_PER_TILE):
                    row0 = base + c2 * 2 * CHUNK
                    wq_ref[wslot, row0:row0 + CHUNK, sl] = w[c2 * CHUNK:(c2 + 1) * CHUNK]

        levels = []
        s = 2
        while s < CHUNK:
            levels.append(level(s))
            s *= 2
        return [setup] + levels + [solve]

    prep_items = [
        lambda: conv_silu(gq_ref, 0, qn_ref),
        lambda: conv_silu(gk_ref, 1, kn_ref),
        lambda: conv_silu(gv_ref, 2, vc_ref),
        l2_norm_heads,
        gate_scans,
    ]
    chains = [(m, h) for m in range(n_super) for h in range(nh)]
    for g0 in range(0, len(chains), group):
        prep_items += wy_items(chains[g0:g0 + group])

    zeros_half = jnp.zeros((CHUNK, DN_HEAD_DIM), BF16)
    keep_state = jnp.where(((s_idx + nblk - 1) % nblk) == 0, 0.0, 1.0)
    states = [state_ref[h] * keep_state for h in range(nh)]

    def chunk_item(ci):
        def run():
            m, c2 = divmod(ci, CHUNKS_PER_TILE)
            base = 2 * SUPER * m
            r0 = ci * CHUNK
            cd_row = cd_ref[rslot, ci * SUBLANES + SUBLANES - 1:(ci + 1) * SUBLANES, :]
            for h in range(nh):
                sl = slice(h * DN_HEAD_DIM, (h + 1) * DN_HEAD_DIM)
                row0 = base + c2 * 2 * CHUNK
                wq = wq_ref[rslot, row0:row0 + 2 * CHUNK, sl]
                res1 = jnp.dot(wq, states[h].astype(BF16), preferred_element_type=F32)
                vb = (u_ref[rslot, r0:r0 + CHUNK, sl] - res1[:CHUNK]).astype(BF16)
                parts = [vb if k == c2 else zeros_half for k in range(CHUNKS_PER_TILE)]
                rhs = parts[0] if CHUNKS_PER_TILE == 1 else jnp.concatenate(parts, axis=0)
                res2 = jnp.dot(akt_ref[rslot, base:base + 2 * SUPER, sl], rhs, preferred_element_type=F32)
                states[h] = states[h] * cd_row[:, nh + h:nh + h + 1] + res2[SUPER:]
                o = res1[CHUNK:] + res2[c2 * CHUNK:(c2 + 1) * CHUNK]
                ms = jnp.mean(o * o, axis=-1, keepdims=True)
                o_ref[r0:r0 + CHUNK, sl] = (o * lax.rsqrt(ms + GATED_NORM_EPS) * ng_ref[...]
                                            * _silu(gz_ref[r0:r0 + CHUNK, sl])).astype(o_ref.dtype)
        return run

    _interleave([chunk_item(ci) for ci in range(n_chunk)], prep_items)

    for h in range(nh):
        state_ref[h] = states[h]


def _gdn(proj, conv_w, bdc, bdr, alc, dtc, alr, dtr, norm_g, layer, *, batch, seq, nh, rows_blk):
    n = proj.shape[0]
    width = nh * DN_HEAD_DIM
    nblk = seq // rows_blk
    total = batch * nblk
    rows_r = bdr.shape[0]
    nxt = lambda s: jnp.minimum(s, total - 1)
    cur = lambda s: jnp.maximum(s - 1, 0)

    lane_vec = pl.BlockSpec((None, 1, LANES), lambda s: (layer, 0, 0))
    row_vec = pl.BlockSpec((None, rows_r, 1), lambda s: (layer, 0, 0))
    return pl.pallas_call(
        functools.partial(_gdn_kernel, nh=nh, group=GDN_CHAIN_GROUP, nblk=nblk),
        grid=(total + 1,),
        in_specs=[
            pl.BlockSpec((rows_blk, width), lambda s: (nxt(s), 0)),
            pl.BlockSpec((rows_blk, width), lambda s: (nxt(s), 1)),
            pl.BlockSpec((rows_blk, width), lambda s: (nxt(s), 2)),
            pl.BlockSpec((rows_blk, width), lambda s: (cur(s), 3)),
            pl.BlockSpec((None, CONV_K, 3 * width), lambda s: (layer, 0, 0)),
            pl.BlockSpec((rows_blk, LANES), lambda s: (nxt(s), 0)),
            pl.BlockSpec((rows_r, rows_blk), lambda s: (0, nxt(s))),
            lane_vec, lane_vec, row_vec, row_vec,
            pl.BlockSpec((None, 1, DN_HEAD_DIM), lambda s: (layer, 0, 0)),
        ],
        out_specs=pl.BlockSpec((rows_blk, width), lambda s: (cur(s), 0)),
        out_shape=jax.ShapeDtypeStruct((n, width), BF16),
        scratch_shapes=[
            pltpu.VMEM((3, SUBLANES, width), F32),
            pltpu.VMEM((nh, DN_HEAD_DIM, DN_HEAD_DIM), F32),
            pltpu.VMEM((rows_blk, width), F32),
            pltpu.VMEM((rows_blk, width), F32),
            pltpu.VMEM((rows_blk, width), F32),
            pltpu.VMEM((2, rows_blk, width), F32),
            pltpu.VMEM((2, 2 * rows_blk, width), BF16),
            pltpu.VMEM((2, 2 * rows_blk, width), BF16),
            pltpu.VMEM((2, SUBLANES * (rows_blk // CHUNK), LANES), F32),
        ],
        compiler_params=_params(("arbitrary",)),
        name="gdn",
    )(proj, proj, proj, proj, conv_w, bdc, bdr, alc, dtc, alr, dtr, norm_g)


def _outproj_ln_kernel(x_ref, od_ref, on_ref, w1_ref, w2_ref, g_ref, b_ref, y_ref, *, alpha):
    half = x_ref.shape[0] // 2
    for r in range(2):
        rs = slice(r * half, (r + 1) * half)
        mix = (jnp.dot(od_ref[rs, :], w1_ref[...], preferred_element_type=F32)
               + jnp.dot(on_ref[rs, :], w2_ref[...], preferred_element_type=F32))
        y_ref[rs, :] = _layer_norm(alpha * x_ref[rs, :] + mix, g_ref[...], b_ref[...])


def _outproj_ln(x, o_diff, o_dn, w_out, g, b, layer, *, alpha, tm):
    n, d = x.shape
    k1, k2 = o_diff.shape[1], o_dn.shape[1]
    assert k1 == k2
    vec = pl.BlockSpec((None, 1, d), lambda i: (layer, 0, 0))
    return pl.pallas_call(
        functools.partial(_outproj_ln_kernel, alpha=alpha),
        grid=(n // tm,),
        in_specs=[
            pl.BlockSpec((tm, d), lambda i: (i, 0)),
            pl.BlockSpec((tm, k1), lambda i: (i, 0)),
            pl.BlockSpec((tm, k2), lambda i: (i, 0)),
            pl.BlockSpec((None, k1, d), lambda i: (layer, 0, 0)),
            pl.BlockSpec((None, k2, d), lambda i: (layer, 1, 0)),
            vec, vec,
        ],
        out_specs=pl.BlockSpec((tm, d), lambda i: (i, 0)),
        out_shape=jax.ShapeDtypeStruct((n, d), F32),
        compiler_params=_params(("parallel",)),
        name="outproj_ln",
    )(x, o_diff, o_dn, w_out, w_out, g, b)


def _tile(n, pref):
    t = min(n, pref)
    assert n % t == 0, (n, pref)
    return t


def kernel(x, positions, ffn1_w_in, ffn1_w_out, ln1_g, ln1_b, w_in, conv_w, a_log, dt_bias, lam_q1, lam_k1, lam_q2, lam_k2, diff_norm_g, delta_norm_g, w_out, ln2_g, ln2_b, ffn2_w_in, ffn2_w_out, ln3_g, ln3_b):
    batch, seq, d = x.shape
    depth = ffn1_w_in.shape[0]
    d_ff = ffn1_w_out.shape[1]
    n = batch * seq
    diff_width = d // 2
    dn_width = d - diff_width
    n_diff_heads = diff_width // (2 * DIFF_HEAD_DIM)
    nh = dn_width // DN_HEAD_DIM
    qk_cols = 2 * (2 * n_diff_heads * DIFF_HEAD_DIM)
    v_cols = n_diff_heads * 2 * DIFF_HEAD_DIM
    main_cols = qk_cols + v_cols + 4 * dn_width
    assert w_in.shape[2] == main_cols + 2 * nh and 2 * nh <= LANES
    assert seq % SUPER == 0
    alpha = (2 * depth) ** 0.25

    tm = _tile(n, 512)
    tm_proj = _tile(n, 1024)
    tf = _tile(d_ff, 512)
    tn = _tile(v_cols, 1024)
    tq = _tile(seq, 256)
    gdn_rows = _tile(seq, 512)

    ffn1_in, ffn1_out = ffn1_w_in.astype(BF16), (0.5 * ffn1_w_out).astype(BF16)
    ffn2_in, ffn2_out = ffn2_w_in.astype(BF16), (0.5 * ffn2_w_out).astype(BF16)
    w_main = w_in.astype(BF16)
    w_gate = w_main[:, :, main_cols:]
    w_bd = jnp.pad(w_gate, ((0, 0), (0, 0), (0, LANES - 2 * nh)))
    rows_r = -(-2 * nh // SUBLANES) * SUBLANES
    w_out_b = w_out.astype(BF16)
    pad_lane = lambda v: jnp.pad(v.astype(F32), ((0, 0), (nh, LANES - 2 * nh))).reshape(depth, 1, LANES)
    pad_row = lambda v: jnp.pad(v.astype(F32), ((0, 0), (nh, rows_r - 2 * nh))).reshape(depth, rows_r, 1)
    alc, dtc, alr, dtr = pad_lane(a_log), pad_lane(dt_bias), pad_row(a_log), pad_row(dt_bias)
    lam_vecs = jnp.stack([lam_q1, lam_k1, lam_q2, lam_k2], axis=1).astype(F32)
    vec3 = lambda v: v.reshape(depth, 1, v.shape[-1])
    ln1_g, ln1_b, ln2_g, ln2_b, ln3_g, ln3_b = map(vec3, (ln1_g, ln1_b, ln2_g, ln2_b, ln3_g, ln3_b))
    diff_g, delta_g = vec3(diff_norm_g), vec3(delta_norm_g)

    rope_c, rope_sa, rope_sb = _rope_tables(positions, tm)

    h = x.reshape(n, d)
    for l in range(depth):
        lam_init = 0.8 - 0.6 * math.exp(-0.3 * l)
        h, hb16 = _ffn_ln(h, ffn1_in, ffn1_out, ln1_g, ln1_b, l, alpha=alpha, tm=tm, tf=tf, emit_bf16=True)
        qk, v, dn, bdc, bdr = _proj(hb16, w_main, w_bd, (rope_c, rope_sa, rope_sb), l, rows=rows_r,
                                    qk_cols=qk_cols, v_cols=v_cols, dn_cols=4 * dn_width, tm=tm_proj, tn=tn)
        o_diff = _attn(qk, v, lam_vecs, diff_g, l, batch=batch, seq=seq, n_heads=n_diff_heads, tq=tq,
                       lam_init=lam_init)
        o_dn = _gdn(dn, conv_w, bdc, bdr, alc, dtc, alr, dtr, delta_g, l, batch=batch, seq=seq, nh=nh,
                    rows_blk=gdn_rows)
        h = _outproj_ln(h, o_diff, o_dn, w_out_b, ln2_g, ln2_b, l, alpha=alpha, tm=tm)
        (h,) = _ffn_ln(h, ffn2_in, ffn2_out, ln3_g, ln3_b, l, alpha=alpha, tm=tm, tf=tf, emit_bf16=False)
    return h.reshape(batch, seq, d)
```
